```python
import jax, jax.numpy as jnp
from jax import lax
import numpy as np

D_MODEL = 1024
BATCH = 8
SEQ = 16384
DEPTH = 2

HEAD_DIM = 64
HEADS_PER_GROUP = 4
ATTN_GROUPS = ((128, 1), (512, 4), (2048, 16))
N_GROUPS = len(ATTN_GROUPS)
ATTN_HEADS = HEADS_PER_GROUP * N_GROUPS
QKV_W = ATTN_HEADS * HEAD_DIM
ATTN_OUT_W = HEADS_PER_GROUP * HEAD_DIM
CONV_W = D_MODEL
CONV_K = 31
D_FF = 2816
ROPE_THETA = 10000.0
EPS = 1e-6
N_BRANCH = 2
N_IN = 3 * QKV_W + 2 * CONV_W + N_BRANCH * D_MODEL
SPLITS = (QKV_W, 2 * QKV_W, 3 * QKV_W, 3 * QKV_W + 2 * CONV_W)
N_MOD = 9
NEG_INF = -1e30

kernel_name = "hybrid_dilated_attn_conformer_conv_macaron"


def rms_norm(x, g):
    xf = x.astype(jnp.float32)
    y = xf * lax.rsqrt(jnp.mean(xf * xf, axis=-1, keepdims=True) + EPS)
    return (y * g.astype(jnp.float32)).astype(x.dtype)


def layer_norm(x, g, b):
    xf = x.astype(jnp.float32)
    mu = jnp.mean(xf, axis=-1, keepdims=True)
    xc = xf - mu
    var = jnp.mean(xc * xc, axis=-1, keepdims=True)
    return (xc * lax.rsqrt(var + EPS) * g.astype(jnp.float32) + b.astype(jnp.float32)).astype(x.dtype)


def rope_tables(s):
    half = HEAD_DIM // 2
    inv_freq = ROPE_THETA ** (-(jnp.arange(half, dtype=jnp.float32) * 2.0 / HEAD_DIM))
    ang = jnp.arange(s, dtype=jnp.float32)[:, None] * inv_freq[None, :]
    return jnp.cos(ang), jnp.sin(ang)


def apply_rope(t, cos, sin):
    half = HEAD_DIM // 2
    tf = t.astype(jnp.float32)
    t1, t2 = tf[..., :half], tf[..., half:]
    c = cos[None, :, None, :]
    s = sin[None, :, None, :]
    return jnp.concatenate([t1 * c - t2 * s, t2 * c + t1 * s], axis=-1).astype(t.dtype)


def dilated_window_attention(q, k, v, window, dilation):
    b, s, h, e = q.shape
    w = window // dilation
    sp = -(-s // window) * window
    pad = sp - s
    nb = sp // window
    seg = sp // dilation

    def to_blocks(t):
        t = jnp.pad(t, ((0, 0), (0, pad), (0, 0), (0, 0)))
        t = t.reshape(b, seg, dilation, h, e).transpose(0, 2, 1, 3, 4)
        return t.reshape(b, dilation, nb, w, h, e)

    def with_prev(t):
        prev = jnp.pad(t, ((0, 0), (0, 0), (1, 0), (0, 0), (0, 0), (0, 0)))[:, :, :-1]
        return jnp.concatenate([prev, t], axis=3)

    qb = to_blocks(q)
    kk = with_prev(to_blocks(k))
    vv = with_prev(to_blocks(v))

    scores = jnp.einsum('bdnqhe,bdnkhe->bdnhqk', qb, kk,
                        preferred_element_type=jnp.float32) * (e ** -0.5)
    qi = jnp.arange(w)[:, None]
    kj = jnp.arange(2 * w)[None, :]
    dist = qi + w - kj
    band = (dist >= 0) & (dist <= w)
    has_prev = (jnp.arange(nb)[:, None, None] > 0) | (kj[None] >= w)
    mask = (band[None] & has_prev)[:, None]
    scores = jnp.where(mask, scores, NEG_INF)
    m = jnp.max(scores, axis=-1, keepdims=True)
    p = jnp.exp(scores - m)
    den = jnp.sum(p, axis=-1)
    o = jnp.einsum('bdnhqk,bdnkhe->bdnqhe', p, vv.astype(jnp.float32))
    o = o / jnp.transpose(den, (0, 1, 2, 4, 3))[..., None]
    lse = m[..., 0] + jnp.log(den)

    o = o.reshape(b, dilation, seg, h, e).transpose(0, 2, 1, 3, 4).reshape(b, sp, h, e)[:, :s]
    lse = jnp.transpose(lse, (0, 1, 2, 4, 3)).reshape(b, dilation, seg, h)
    lse = lse.transpose(0, 2, 1, 3).reshape(b, sp, h)[:, :s]
    return o, lse


def conv_module(u, conv_w, conv_b, ln_g, ln_b, w_o):
    a, g = jnp.split(u, 2, axis=-1)
    h = a * jax.nn.sigmoid(g)
    h = lax.conv_general_dilated(h, conv_w[:, None, :], window_strides=(1,),
                                 padding=[(CONV_K - 1, 0)],
                                 dimension_numbers=('NWC', 'WIO', 'NWC'),
                                 feature_group_count=CONV_W) + conv_b
    h = jax.nn.silu(layer_norm(h, ln_g, ln_b))
    return h @ w_o


def swiglu(h, wg, wu, wd):
    return (jax.nn.silu(h @ wg) * (h @ wu)) @ wd


def _fwd_setup_inputs(seed: int = 0) -> dict:
    key = jax.random.key(seed)
    ks = jax.random.split(key, 20)
    f32 = jnp.float32
    nrm = lambda k, shape, scale: (jax.random.normal(k, shape, f32) * scale)
    return {
        "x": nrm(ks[0], (BATCH, SEQ, D_MODEL), 1.0),
        "c": nrm(ks[1], (BATCH, D_MODEL), 1.0),
        "ada_w": nrm(ks[2], (DEPTH, D_MODEL, N_MOD * D_MODEL), 0.5 * D_MODEL ** -0.5),
        "ada_b": nrm(ks[3], (DEPTH, N_MOD * D_MODEL), 0.02),
        "norm_g": 1.0 + nrm(ks[4], (DEPTH, 3, D_MODEL), 0.02),
        "ffn_wg": nrm(ks[5], (DEPTH, 2, D_MODEL, D_FF), D_MODEL ** -0.5),
        "ffn_wu": nrm(ks[6], (DEPTH, 2, D_MODEL, D_FF), D_MODEL ** -0.5),
        "ffn_wd": nrm(ks[7], (DEPTH, 2, D_FF, D_MODEL), D_FF ** -0.5),
        "w_in": nrm(ks[8], (DEPTH, D_MODEL, N_IN), D_MODEL ** -0.5),
        "attn_wo": nrm(ks[9], (DEPTH, ATTN_OUT_W, D_MODEL), ATTN_OUT_W ** -0.5),
        "conv_w": nrm(ks[10], (DEPTH, CONV_K, CONV_W), CONV_K ** -0.5),
        "conv_b": nrm(ks[11], (DEPTH, CONV_W), 0.02),
        "conv_ln_g": 1.0 + nrm(ks[12], (DEPTH, CONV_W), 0.02),
        "conv_ln_b": nrm(ks[13], (DEPTH, CONV_W), 0.02),
        "conv_wo": nrm(ks[14], (DEPTH, CONV_W, D_MODEL), CONV_W ** -0.5),
        "w_out": nrm(ks[15], (DEPTH, D_MODEL, D_MODEL), D_MODEL ** -0.5),
        "final_g": 1.0 + nrm(ks[16], (D_MODEL,), 0.02),
    }


def _fwd_reference(x, c, ada_w, ada_b, norm_g, ffn_wg, ffn_wu, ffn_wd, w_in, attn_wo,
              conv_w, conv_b, conv_ln_g, conv_ln_b, conv_wo, w_out, final_g):
    b, s, d = x.shape
    cos, sin = rope_tables(s)
    c_act = jax.nn.silu(c)
    for l in range(DEPTH):
        mod = (c_act @ ada_w[l] + ada_b[l]).reshape(b, N_MOD, d)[:, :, None, :]
        shift = lambda i: mod[:, 3 * i]
        scale = lambda i: mod[:, 3 * i + 1]
        gate = lambda i: mod[:, 3 * i + 2]
        modulate = lambda t, i: rms_norm(t, norm_g[l, i]) * (1.0 + scale(i)) + shift(i)

        h = modulate(x, 0)
        x = x + 0.5 * gate(0) * swiglu(h, ffn_wg[l, 0], ffn_wu[l, 0], ffn_wd[l, 0])

        h = modulate(x, 1)
        z = h @ w_in[l]
        q, k, v, u, gates = jnp.split(z, SPLITS, axis=-1)
        q = apply_rope(q.reshape(b, s, ATTN_HEADS, HEAD_DIM), cos, sin)
        k = apply_rope(k.reshape(b, s, ATTN_HEADS, HEAD_DIM), cos, sin)
        v = v.reshape(b, s, ATTN_HEADS, HEAD_DIM)
        outs, lses = [], []
        for g_idx, (win, dil) in enumerate(ATTN_GROUPS):
            sl = slice(g_idx * HEADS_PER_GROUP, (g_idx + 1) * HEADS_PER_GROUP)
            o_g, lse_g = dilated_window_attention(q[:, :, sl], k[:, :, sl], v[:, :, sl], win, dil)
            outs.append(o_g)
            lses.append(lse_g)
        wts = jax.nn.softmax(jnp.stack(lses, axis=0), axis=0)
        o = jnp.sum(wts[..., None] * jnp.stack(outs, axis=0), axis=0)
        y_attn = o.astype(x.dtype).reshape(b, s, ATTN_OUT_W) @ attn_wo[l]
        y_conv = conv_module(u, conv_w[l], conv_b[l], conv_ln_g[l], conv_ln_b[l], conv_wo[l])
        g_attn, g_conv = jnp.split(gates, N_BRANCH, axis=-1)
        y = jax.nn.sigmoid(g_attn) * y_attn + jax.nn.sigmoid(g_conv) * y_conv
        x = x + gate(1) * (y @ w_out[l])

        h = modulate(x, 2)
        x = x + 0.5 * gate(2) * swiglu(h, ffn_wg[l, 1], ffn_wu[l, 1], ffn_wd[l, 1])
    return rms_norm(x, final_g)


import jax as _jax
import jax.numpy as _jnp

TWIN_FORMAT = 'train_step'
FWD_PARAMS = ['x', 'c', 'ada_w', 'ada_b', 'norm_g', 'ffn_wg', 'ffn_wu', 'ffn_wd', 'w_in', 'attn_wo', 'conv_w', 'conv_b', 'conv_ln_g', 'conv_ln_b', 'conv_wo', 'w_out', 'final_g']
TWIN_WEIGHTS = ['ada_w', 'ada_b', 'norm_g', 'ffn_wg', 'ffn_wu', 'ffn_wd', 'w_in', 'attn_wo', 'conv_w', 'conv_b', 'conv_ln_g', 'conv_ln_b', 'conv_wo', 'w_out', 'final_g']
TWIN_DIFF_INPUT = 'x'
TWIN_INPUTS = ['x', 'c', 'ada_w', 'ada_b', 'norm_g', 'ffn_wg', 'ffn_wu', 'ffn_wd', 'w_in', 'attn_wo', 'conv_w', 'conv_b', 'conv_ln_g', 'conv_ln_b', 'conv_wo', 'w_out', 'final_g', 'loss_target', 'm_ada_w', 'm_ada_b', 'm_norm_g', 'm_ffn_wg', 'm_ffn_wu', 'm_ffn_wd', 'm_w_in', 'm_attn_wo', 'm_conv_w', 'm_conv_b', 'm_conv_ln_g', 'm_conv_ln_b', 'm_conv_wo', 'm_w_out', 'm_final_g', 'v_ada_w', 'v_ada_b', 'v_norm_g', 'v_ffn_wg', 'v_ffn_wu', 'v_ffn_wd', 'v_w_in', 'v_attn_wo', 'v_conv_w', 'v_conv_b', 'v_conv_ln_g', 'v_conv_ln_b', 'v_conv_wo', 'v_w_out', 'v_final_g']
TWIN_OUTPUTS = ['loss', 'grad_x', 'grad_ada_w', 'grad_ada_b', 'grad_norm_g', 'grad_ffn_wg', 'grad_ffn_wu', 'grad_ffn_wd', 'grad_w_in', 'grad_attn_wo', 'grad_conv_w', 'grad_conv_b', 'grad_conv_ln_g', 'grad_conv_ln_b', 'grad_conv_wo', 'grad_w_out', 'grad_final_g', 'delta_ada_w', 'delta_ada_b', 'delta_norm_g', 'delta_ffn_wg', 'delta_ffn_wu', 'delta_ffn_wd', 'delta_w_in', 'delta_attn_wo', 'delta_conv_w', 'delta_conv_b', 'delta_conv_ln_g', 'delta_conv_ln_b', 'delta_conv_wo', 'delta_w_out', 'delta_final_g', 'new_m_ada_w', 'new_m_ada_b', 'new_m_norm_g', 'new_m_ffn_wg', 'new_m_ffn_wu', 'new_m_ffn_wd', 'new_m_w_in', 'new_m_attn_wo', 'new_m_conv_w', 'new_m_conv_b', 'new_m_conv_ln_g', 'new_m_conv_ln_b', 'new_m_conv_wo', 'new_m_w_out', 'new_m_final_g', 'new_v_ada_w', 'new_v_ada_b', 'new_v_norm_g', 'new_v_ffn_wg', 'new_v_ffn_wu', 'new_v_ffn_wd', 'new_v_w_in', 'new_v_attn_wo', 'new_v_conv_w', 'new_v_conv_b', 'new_v_conv_ln_g', 'new_v_conv_ln_b', 'new_v_conv_wo', 'new_v_w_out', 'new_v_final_g']
TWIN_LEAF_KINDS = {'loss': 'loss', 'grad_x': 'grad_x', 'grad_ada_w': 'grad_w', 'grad_ada_b': 'grad_w', 'grad_norm_g': 'grad_w', 'grad_ffn_wg': 'grad_w', 'grad_ffn_wu': 'grad_w', 'grad_ffn_wd': 'grad_w', 'grad_w_in': 'grad_w', 'grad_attn_wo': 'grad_w', 'grad_conv_w': 'grad_w', 'grad_conv_b': 'grad_w', 'grad_conv_ln_g': 'grad_w', 'grad_conv_ln_b': 'grad_w', 'grad_conv_wo': 'grad_w', 'grad_w_out': 'grad_w', 'grad_final_g': 'grad_w', 'delta_ada_w': 'delta_w', 'delta_ada_b': 'delta_w', 'delta_norm_g': 'delta_w', 'delta_ffn_wg': 'delta_w', 'delta_ffn_wu': 'delta_w', 'delta_ffn_wd': 'delta_w', 'delta_w_in': 'delta_w', 'delta_attn_wo': 'delta_w', 'delta_conv_w': 'delta_w', 'delta_conv_b': 'delta_w', 'delta_conv_ln_g': 'delta_w', 'delta_conv_ln_b': 'delta_w', 'delta_conv_wo': 'delta_w', 'delta_w_out': 'delta_w', 'delta_final_g': 'delta_w', 'new_m_ada_w': 'new_m', 'new_m_ada_b': 'new_m', 'new_m_norm_g': 'new_m', 'new_m_ffn_wg': 'new_m', 'new_m_ffn_wu': 'new_m', 'new_m_ffn_wd': 'new_m', 'new_m_w_in': 'new_m', 'new_m_attn_wo': 'new_m', 'new_m_conv_w': 'new_m', 'new_m_conv_b': 'new_m', 'new_m_conv_ln_g': 'new_m', 'new_m_conv_ln_b': 'new_m', 'new_m_conv_wo': 'new_m', 'new_m_w_out': 'new_m', 'new_m_final_g': 'new_m', 'new_v_ada_w': 'new_v', 'new_v_ada_b': 'new_v', 'new_v_norm_g': 'new_v', 'new_v_ffn_wg': 'new_v', 'new_v_ffn_wu': 'new_v', 'new_v_ffn_wd': 'new_v', 'new_v_w_in': 'new_v', 'new_v_attn_wo': 'new_v', 'new_v_conv_w': 'new_v', 'new_v_conv_b': 'new_v', 'new_v_conv_ln_g': 'new_v', 'new_v_conv_ln_b': 'new_v', 'new_v_conv_wo': 'new_v', 'new_v_w_out': 'new_v', 'new_v_final_g': 'new_v'}


def _forward(args):
    return _fwd_reference(*[args[k] for k in FWD_PARAMS])


def _output_shape():
    def fwd():
        inp = _fwd_setup_inputs(0)
        return _fwd_reference(*[inp[k] for k in FWD_PARAMS])
    out = _jax.eval_shape(fwd)
    return out.shape, out.dtype

N_MICROBATCH = 1
ADAM_LR = 0.001
ADAM_B1 = 0.9
ADAM_B2 = 0.999
ADAM_EPS = 1e-08
ADAM_WD = 0.01
ADAM_STEP = 10
PER_EXAMPLE_BATCH_AXIS = {'x': 0, 'c': 0, 'loss_target': 0}
SHARED_INPUTS = []
_WEIGHT_DTYPES = {'ada_w': _jnp.float32, 'ada_b': _jnp.float32, 'norm_g': _jnp.float32, 'ffn_wg': _jnp.float32, 'ffn_wu': _jnp.float32, 'ffn_wd': _jnp.float32, 'w_in': _jnp.float32, 'attn_wo': _jnp.float32, 'conv_w': _jnp.float32, 'conv_b': _jnp.float32, 'conv_ln_g': _jnp.float32, 'conv_ln_b': _jnp.float32, 'conv_wo': _jnp.float32, 'w_out': _jnp.float32, 'final_g': _jnp.float32}
MOMENT_SCALE = {'ada_w': 5.130382e-02, 'ada_b': 8.494120e-02, 'norm_g': 4.982216e-02, 'ffn_wg': 2.355680e-02, 'ffn_wu': 2.280900e-02, 'ffn_wd': 3.781114e-02, 'w_in': 2.000565e-02, 'attn_wo': 1.846507e-02, 'conv_w': 3.673650e-02, 'conv_b': 7.113488e-02, 'conv_ln_g': 4.485575e-02, 'conv_ln_b': 3.782341e-02, 'conv_wo': 3.508521e-02, 'w_out': 3.959945e-02, 'final_g': 1.279872e+02}


def _to_microbatches(a, axis):
    t = _jnp.moveaxis(a, axis, 0)
    t = t.reshape((N_MICROBATCH, t.shape[0] // N_MICROBATCH) + t.shape[1:])
    return _jnp.moveaxis(t, 1, axis + 1)


def setup_inputs(seed: int = 0) -> dict:
    inp = _fwd_setup_inputs(seed)
    key = _jax.random.fold_in(_jax.random.key(seed), 7919)
    shape, _ = _output_shape()
    out = dict(inp)
    out["loss_target"] = _jax.random.normal(_jax.random.fold_in(key, 0), shape, _jnp.float32)
    for i, name in enumerate(TWIN_WEIGHTS):
        w = inp[name].astype(_jnp.float32)
        if MOMENT_SCALE is None:
            s = _jnp.sqrt(_jnp.mean(_jnp.square(w)) + 1e-30)
        else:
            s = MOMENT_SCALE[name]
        km, kv = _jax.random.split(_jax.random.fold_in(key, i + 1))
        out[name] = w
        out["m_" + name] = s * _jax.random.normal(km, w.shape, _jnp.float32)
        out["v_" + name] = (s * s) * _jax.random.uniform(kv, w.shape, _jnp.float32, 0.5, 1.5)
    if N_MICROBATCH > 1:
        for name, axis in PER_EXAMPLE_BATCH_AXIS.items():
            out[name] = _to_microbatches(out[name], axis)
    return {'x': out['x'], 'c': out['c'], 'ada_w': out['ada_w'], 'ada_b': out['ada_b'], 'norm_g': out['norm_g'], 'ffn_wg': out['ffn_wg'], 'ffn_wu': out['ffn_wu'], 'ffn_wd': out['ffn_wd'], 'w_in': out['w_in'], 'attn_wo': out['attn_wo'], 'conv_w': out['conv_w'], 'conv_b': out['conv_b'], 'conv_ln_g': out['conv_ln_g'], 'conv_ln_b': out['conv_ln_b'], 'conv_wo': out['conv_wo'], 'w_out': out['w_out'], 'final_g': out['final_g'], 'loss_target': out['loss_target'], 'm_ada_w': out['m_ada_w'], 'm_ada_b': out['m_ada_b'], 'm_norm_g': out['m_norm_g'], 'm_ffn_wg': out['m_ffn_wg'], 'm_ffn_wu': out['m_ffn_wu'], 'm_ffn_wd': out['m_ffn_wd'], 'm_w_in': out['m_w_in'], 'm_attn_wo': out['m_attn_wo'], 'm_conv_w': out['m_conv_w'], 'm_conv_b': out['m_conv_b'], 'm_conv_ln_g': out['m_conv_ln_g'], 'm_conv_ln_b': out['m_conv_ln_b'], 'm_conv_wo': out['m_conv_wo'], 'm_w_out': out['m_w_out'], 'm_final_g': out['m_final_g'], 'v_ada_w': out['v_ada_w'], 'v_ada_b': out['v_ada_b'], 'v_norm_g': out['v_norm_g'], 'v_ffn_wg': out['v_ffn_wg'], 'v_ffn_wu': out['v_ffn_wu'], 'v_ffn_wd': out['v_ffn_wd'], 'v_w_in': out['v_w_in'], 'v_attn_wo': out['v_attn_wo'], 'v_conv_w': out['v_conv_w'], 'v_conv_b': out['v_conv_b'], 'v_conv_ln_g': out['v_conv_ln_g'], 'v_conv_ln_b': out['v_conv_ln_b'], 'v_conv_wo': out['v_conv_wo'], 'v_w_out': out['v_w_out'], 'v_final_g': out['v_final_g']}


def _loss(weights, diff, rest, loss_target):
    with _jax.named_scope("forward"):
        args = {**rest, TWIN_DIFF_INPUT: diff, **{k: w.astype(_WEIGHT_DTYPES[k]) for k, w in weights.items()}}
        y = _forward(args)
    with _jax.named_scope("loss_head"):
        err = _jnp.square(y.astype(_jnp.float32) - loss_target)
        return 0.5 * _jnp.sum(_jnp.mean(err, axis=-1)) if err.ndim else 0.5 * err


def _adamw(w, g, m, v):
    m = ADAM_B1 * m + (1.0 - ADAM_B1) * g
    v = ADAM_B2 * v + (1.0 - ADAM_B2) * _jnp.square(g)
    m_hat = m / (1.0 - ADAM_B1 ** ADAM_STEP)
    v_hat = v / (1.0 - ADAM_B2 ** ADAM_STEP)
    delta = -ADAM_LR * (m_hat / (_jnp.sqrt(v_hat) + ADAM_EPS) + ADAM_WD * w)
    return delta, m, v


def reference(x, c, ada_w, ada_b, norm_g, ffn_wg, ffn_wu, ffn_wd, w_in, attn_wo, conv_w, conv_b, conv_ln_g, conv_ln_b, conv_wo, w_out, final_g, loss_target, m_ada_w, m_ada_b, m_norm_g, m_ffn_wg, m_ffn_wu, m_ffn_wd, m_w_in, m_attn_wo, m_conv_w, m_conv_b, m_conv_ln_g, m_conv_ln_b, m_conv_wo, m_w_out, m_final_g, v_ada_w, v_ada_b, v_norm_g, v_ffn_wg, v_ffn_wu, v_ffn_wd, v_w_in, v_attn_wo, v_conv_w, v_conv_b, v_conv_ln_g, v_conv_ln_b, v_conv_wo, v_w_out, v_final_g):
    given = dict(x=x, c=c, ada_w=ada_w, ada_b=ada_b, norm_g=norm_g, ffn_wg=ffn_wg, ffn_wu=ffn_wu, ffn_wd=ffn_wd, w_in=w_in, attn_wo=attn_wo, conv_w=conv_w, conv_b=conv_b, conv_ln_g=conv_ln_g, conv_ln_b=conv_ln_b, conv_wo=conv_wo, w_out=w_out, final_g=final_g, loss_target=loss_target, m_ada_w=m_ada_w, m_ada_b=m_ada_b, m_norm_g=m_norm_g, m_ffn_wg=m_ffn_wg, m_ffn_wu=m_ffn_wu, m_ffn_wd=m_ffn_wd, m_w_in=m_w_in, m_attn_wo=m_attn_wo, m_conv_w=m_conv_w, m_conv_b=m_conv_b, m_conv_ln_g=m_conv_ln_g, m_conv_ln_b=m_conv_ln_b, m_conv_wo=m_conv_wo, m_w_out=m_w_out, m_final_g=m_final_g, v_ada_w=v_ada_w, v_ada_b=v_ada_b, v_norm_g=v_norm_g, v_ffn_wg=v_ffn_wg, v_ffn_wu=v_ffn_wu, v_ffn_wd=v_ffn_wd, v_w_in=v_w_in, v_attn_wo=v_attn_wo, v_conv_w=v_conv_w, v_conv_b=v_conv_b, v_conv_ln_g=v_conv_ln_g, v_conv_ln_b=v_conv_ln_b, v_conv_wo=v_conv_wo, v_w_out=v_w_out, v_final_g=v_final_g)
    weights = {n: given[n] for n in TWIN_WEIGHTS}
    shared = {n: given[n] for n in SHARED_INPUTS}
    per_example = {n: given[n] for n in ['x', 'c']}
    grad_fn = _jax.value_and_grad(_loss, argnums=(0, 1))

    def one_microbatch(ex, loss_target):
        ex = dict(ex)
        diff = ex.pop(TWIN_DIFF_INPUT)
        return grad_fn(weights, diff, {**shared, **ex}, loss_target)

    if N_MICROBATCH == 1:
        loss, (grad_w, grad_x) = one_microbatch(per_example, given["loss_target"])
    else:
        def body(carry, xs):
            loss_sum, grad_sum = carry
            l_k, (gw_k, gx_k) = one_microbatch(xs[0], xs[1])
            with _jax.named_scope("update"):
                return (loss_sum + l_k, _jax.tree.map(_jnp.add, grad_sum, gw_k)), gx_k

        init = (_jnp.zeros((), _jnp.float32), _jax.tree.map(_jnp.zeros_like, weights))
        (loss, grad_w), grad_x = _jax.lax.scan(body, init, (per_example, given["loss_target"]))
    with _jax.named_scope("update"):
        delta_w, new_m, new_v = {}, {}, {}
        for n in TWIN_WEIGHTS:
            delta_w[n], new_m[n], new_v[n] = _adamw(weights[n], grad_w[n], given["m_" + n], given["v_" + n])
    return (loss, grad_x, *[grad_w[n] for n in TWIN_WEIGHTS], *[delta_w[n] for n in TWIN_WEIGHTS],
            *[new_m[n] for n in TWIN_WEIGHTS], *[new_v[n] for n in TWIN_WEIGHTS])
```

```python
import functools

import jax
import jax.numpy as jnp
from jax import lax
from jax.experimental import pallas as pl
from jax.experimental.pallas import tpu as pltpu

F32 = jnp.float32
BF16 = jnp.bfloat16

N_DEV = 8
D = 1024
D_FF = 2816
HEAD_DIM = 64
GROUP_W = 256
ATTN_DILATIONS = (1, 4, 16)
BLK = 128
QKV_W = 768
CONV_K = 31
ROPE_THETA = 10000.0
EPS = 1e-6
NEG_INF = -1e30
ADAM_LR, ADAM_B1, ADAM_B2, ADAM_EPS, ADAM_WD, ADAM_STEP = 0.001, 0.9, 0.999, 1e-08, 0.01, 10

V7X_VMEM_BYTES = 64 * 1024 * 1024
VMEM_LIMIT = V7X_VMEM_BYTES - 8 * 1024 * 1024
LANES = 128
MESH_AXES = ("x", "y", "c")


def _params(n_grid):
    return pltpu.CompilerParams(vmem_limit_bytes=VMEM_LIMIT, dimension_semantics=("arbitrary",) * n_grid)


def _sigmoid(v):
    return 1.0 / (1.0 + jnp.exp(-v))


def _mm(name, prods, epilogue, out_dtypes, *, nt=False, tiles=(), vecs=(), tm=512, tn=None, a_pre=None):
    s = prods[0][0].shape[0]
    n = prods[0][1].shape[0] if nt else prods[0][1].shape[1]
    tn = n if tn is None else tn
    tm = min(tm, s)
    assert s % tm == 0 and n % tn == 0
    n_p, n_t, n_v = len(prods), len(tiles), len(vecs)
    dn = (((1,), (1,)), ((), ())) if nt else (((1,), (0,)), ((), ()))

    def body(*refs):
        p_refs, rest = refs[:2 * n_p], refs[2 * n_p:]
        t_refs, v_refs, o_refs = rest[:n_t], rest[n_t:n_t + n_v], rest[n_t + n_v:]
        accs = []
        for p in range(n_p):
            a = p_refs[2 * p][...]
            if a_pre is not None:
                a = a_pre(a)
            accs.append(lax.dot_general(a, p_refs[2 * p + 1][...], dn, preferred_element_type=F32))
        outs = epilogue(accs, [t[...] for t in t_refs], [v[...] for v in v_refs])
        for o_ref, o in zip(o_refs, outs, strict=True):
            o_ref[...] = o.astype(o_ref.dtype)

    in_specs = []
    operands = []
    for a, b in prods:
        k = a.shape[1]
        in_specs.append(pl.BlockSpec((tm, k), lambda j, i: (i, 0)))
        in_specs.append(pl.BlockSpec((tn, k), lambda j, i: (j, 0)) if nt else pl.BlockSpec((k, tn), lambda j, i: (0, j)))
        operands += [a, b]
    for arr, off in tiles:
        in_specs.append(pl.BlockSpec((tm, tn), functools.partial(lambda j, i, off: (i, j + off), off=off)))
        operands.append(arr)
    for v in vecs:
        in_specs.append(pl.BlockSpec((1, tn), lambda j, i: (0, j)))
        operands.append(v)
    out = pl.pallas_call(
        body, name=name, grid=(n // tn, s // tm), in_specs=in_specs,
        out_specs=[pl.BlockSpec((tm, tn), lambda j, i: (i, j)) for _ in out_dtypes],
        out_shape=[jax.ShapeDtypeStruct((s, n), dt) for dt in out_dtypes],
        compiler_params=_params(2),
    )(*operands)
    return out


def _mm_tn(name, a, b, *, tk=512, tn=None, out_dtype=BF16):
    s, m = a.shape
    n = b.shape[1]
    tn = n if tn is None else tn
    tk = min(tk, s)
    n_k = s // tk
    assert s % tk == 0 and n % tn == 0

    def body(a_ref, b_ref, o_ref, acc_ref):
        k = pl.program_id(1)

        @pl.when(k == 0)
        def _():
            acc_ref[...] = jnp.zeros_like(acc_ref)

        acc_ref[...] += lax.dot_general(a_ref[...], b_ref[...], (((0,), (0,)), ((), ())), preferred_element_type=F32)

        @pl.when(k == n_k - 1)
        def _():
            o_ref[...] = acc_ref[...].astype(o_ref.dtype)

    return pl.pallas_call(
        body, name=name, grid=(n // tn, n_k),
        in_specs=[pl.BlockSpec((tk, m), lambda j, k: (k, 0)), pl.BlockSpec((tk, tn), lambda j, k: (k, j))],
        out_specs=pl.BlockSpec((m, tn), lambda j, k: (0, j)),
        out_shape=jax.ShapeDtypeStruct((m, n), out_dtype),
        scratch_shapes=[pltpu.VMEM((m, tn), F32)],
        compiler_params=_params(2),
    )(a, b)


def _rowwise(name, body, rows, vecs, outs, accs=(), *, ts=256):
    rows = [r if isinstance(r, tuple) else (r, r.shape[1], 0) for r in rows]
    s = rows[0][0].shape[0]
    ts = min(ts, s)
    assert s % ts == 0
    n_r, n_v, n_o, n_a = len(rows), len(vecs), len(outs), len(accs)

    def kbody(*refs):
        r_refs, v_refs = refs[:n_r], refs[n_r:n_r + n_v]
        o_refs, a_refs = refs[n_r + n_v:n_r + n_v + n_o], refs[n_r + n_v + n_o:]
        res_o, res_a = body([r[...] for r in r_refs], [v[...] for v in v_refs])
        for o_ref, o in zip(o_refs, res_o, strict=True):
            o_ref[...] = o.astype(o_ref.dtype)
        if n_a:
            first = pl.program_id(0) == 0

            @pl.when(first)
            def _():
                for a_ref, a in zip(a_refs, res_a, strict=True):
                    a_ref[...] = a

            @pl.when(jnp.logical_not(first))
            def _():
                for a_ref, a in zip(a_refs, res_a, strict=True):
                    a_ref[...] += a

    in_specs = [pl.BlockSpec((ts, w), functools.partial(lambda i, cb: (i, cb), cb=cb)) for _, w, cb in rows]
    in_specs += [pl.BlockSpec(v.shape, functools.partial(lambda i, nd: (0,) * nd, nd=v.ndim)) for v in vecs]
    out_specs = [pl.BlockSpec((ts, w), lambda i: (i, 0)) for w, _ in outs]
    out_specs += [pl.BlockSpec(shp, functools.partial(lambda i, nd: (0,) * nd, nd=len(shp))) for shp in accs]
    out_shape = [jax.ShapeDtypeStruct((s, w), dt) for w, dt in outs] + [jax.ShapeDtypeStruct(shp, F32) for shp in accs]
    return pl.pallas_call(
        kbody, name=name, grid=(s // ts,), in_specs=in_specs, out_specs=out_specs, out_shape=out_shape,
        compiler_params=_params(1),
    )(*[r[0] for r in rows], *vecs)


def _colsum(v):
    return jnp.sum(v, axis=0, keepdims=True)


def _rms_parts(x):
    r = lax.rsqrt(jnp.mean(x * x, axis=-1, keepdims=True) + EPS)
    return x * r, r


def _rms_bwd(dxhat, xhat, r):
    return r * (dxhat - xhat * jnp.mean(dxhat * xhat, axis=-1, keepdims=True))


def _modulate(name, x, ng, sc, sh):
    def body(rows, vecs):
        (xv,), (g, s_, b) = rows, vecs
        xhat, _ = _rms_parts(xv)
        return [xhat * g * (1.0 + s_) + b], []

    return _rowwise(name, body, [x], [ng, sc, sh], [(D, BF16)])[0]


def _modulate_bwd(name, dh, x, dxo, ng, sc):
    def body(rows, vecs):
        (dhv, xv, dxov), (g, s_) = rows, vecs
        xhat, r = _rms_parts(xv)
        dn = dhv * (1.0 + s_)
        dx = _rms_bwd(dn * g, xhat, r)
        return [dxov + dx], [_colsum(dhv), _colsum(dhv * xhat * g), _colsum(dn * xhat)]

    return _rowwise(name, body, [dh, x, dxo], [ng, sc], [(D, F32)], [(1, D)] * 3)


def _gate_scale(name, dx, f, gate, coef):
    def body(rows, vecs):
        (dxv, fv), (g,) = rows, vecs
        return [coef * g * dxv], [_colsum(coef * dxv * fv.astype(F32))]

    return _rowwise(name, body, [dx, f], [gate], [(D, BF16)], [(1, D)])


def _ffn_fwd(tag, x, ng, sc, sh, gate, wg, wu, wd):
    h = _modulate(f"{tag}_mod", x, ng, sc, sh)

    def up_epi(accs, _t, _v):
        a, u = accs
        return [a, u, a * _sigmoid(a) * u]

    a, u, t = _mm(f"{tag}_up", [(h, wg), (h, wu)], up_epi, [BF16] * 3, tn=D_FF // 2)

    def down_epi(accs, tl, vs):
        return [tl[0] + 0.5 * vs[0] * accs[0], accs[0]]

    x_out, f = _mm(f"{tag}_down", [(t, wd)], down_epi, [F32, BF16], tiles=[(x, 0)], vecs=[gate])
    return x_out, (x, h, a, u, f)


def _ffn_bwd(tag, dxo, saved, ng, sc, gate, wg, wu, wd):
    x, h, a, u, f = saved
    df, dgate = _gate_scale(f"{tag}_gs", dxo, f, gate, 0.5)

    def epi(accs, tl, _v):
        dt = accs[0]
        av, uv = tl[0].astype(F32), tl[1].astype(F32)
        sg = _sigmoid(av)
        sil = av * sg
        return [dt * uv * (sg * (1.0 + av * (1.0 - sg))), dt * sil, sil * uv]

    da, du, t = _mm(f"{tag}_bdown", [(df, wd)], epi, [BF16] * 3, nt=True, tiles=[(a, 0), (u, 0)], tn=D_FF // 2)
    dwd = _mm_tn(f"{tag}_dwd", t, df)
    dwg = _mm_tn(f"{tag}_dwg", h, da)
    dwu = _mm_tn(f"{tag}_dwu", h, du)
    dh = _mm(f"{tag}_bup", [(da, wg), (du, wu)], lambda accs, _t, _v: [accs[0] + accs[1]], [F32], nt=True)[0]
    dx_in, dsh, dsc, dng = _modulate_bwd(f"{tag}_modb", dh, x, dxo, ng, sc)
    return dx_in, (dsh, dsc, dgate, dng), (dwg, dwu, dwd)


def _rope_tables(s):
    half = HEAD_DIM // 2
    inv_freq = ROPE_THETA ** (-(jnp.arange(half, dtype=F32) * 2.0 / HEAD_DIM))
    ang = jnp.arange(s, dtype=F32)[:, None] * inv_freq[None, :]
    cos, sin = jnp.cos(ang), jnp.sin(ang)
    return jnp.tile(jnp.concatenate([cos, cos], axis=1), (1, 2)), jnp.tile(jnp.concatenate([-sin, sin], axis=1), (1, 2))


def _rotate(v, cos2, sin2, sign):
    w = v.shape[1]
    lane = lax.broadcasted_iota(jnp.int32, v.shape, 1)
    partner = jnp.where(lane % HEAD_DIM < HEAD_DIM // 2, pltpu.roll(v, w - HEAD_DIM // 2, 1), pltpu.roll(v, HEAD_DIM // 2, 1))
    reps = w // LANES
    return v * jnp.tile(cos2, (1, reps)) + partner * (sign * jnp.tile(sin2, (1, reps)))


def _rope_fwd(name, qkv, cos2, sin2):
    def body(rows, _v):
        q, k, c2, s2 = rows
        return [_rotate(q.astype(F32), c2, s2, 1.0), _rotate(k.astype(F32), c2, s2, 1.0)], []

    return _rowwise(name, body, [(qkv, QKV_W, 0), (qkv, QKV_W, 1), cos2, sin2], [], [(QKV_W, BF16)] * 2)


def _rope_bwd(name, dq, dk, dv, cos2, sin2):
    def body(rows, _v):
        c2, s2 = rows[9], rows[10]
        dqv = jnp.concatenate([r.astype(F32) for r in rows[0:3]], axis=1)
        dkv = jnp.concatenate([r.astype(F32) for r in rows[3:6]], axis=1)
        dvv = jnp.concatenate([r.astype(F32) for r in rows[6:9]], axis=1)
        return [jnp.concatenate([_rotate(dqv, c2, s2, -1.0), _rotate(dkv, c2, s2, -1.0), dvv], axis=1)], []

    return _rowwise(name, body, [*dq, *dk, *dv, cos2, sin2], [], [(3 * QKV_W, BF16)])[0]


def _band_masks(has_prev):
    qi = lax.broadcasted_iota(jnp.int32, (BLK, BLK), 0)
    kj = lax.broadcasted_iota(jnp.int32, (BLK, BLK), 1)
    return kj <= qi, jnp.logical_and(kj >= qi, has_prev)


def _dot_nt(a, b):
    return lax.dot_general(a, b, (((1,), (1,)), ((), ())), preferred_element_type=F32)


def _dot_tn(a, b):
    return lax.dot_general(a, b, (((0,), (0,)), ((), ())), preferred_element_type=F32)


def _dot(a, b):
    return jnp.dot(a, b, preferred_element_type=F32)


def _attn_specs(g, n_q, n_v, clamp):
    cur = lambda r, n: (clamp(n), r * n_q + g)
    prev = lambda r, n: (jnp.maximum(clamp(n) - 1, 0), r * n_q + g)
    vcur = lambda r, n: (clamp(n), r * n_v + 6 + g)
    vprev = lambda r, n: (jnp.maximum(clamp(n) - 1, 0), r * n_v + 6 + g)
    blk = (BLK, GROUP_W)
    return [pl.BlockSpec(blk, cur), pl.BlockSpec(blk, cur), pl.BlockSpec(blk, prev), pl.BlockSpec(blk, vcur), pl.BlockSpec(blk, vprev)]


def _attn_fwd(name, q_r, k_r, qkv, g, dil):
    s = q_r.shape[0]
    rows = s // dil
    nb = rows // BLK
    scale = HEAD_DIM ** -0.5

    def body(q_ref, kc_ref, kp_ref, vc_ref, vp_ref, o_ref, l_ref):
        mask_c, mask_p = _band_masks(pl.program_id(1) > 0)
        q, kc, kp, vc, vp = q_ref[...], kc_ref[...], kp_ref[...], vc_ref[...], vp_ref[...]
        lane = lax.broadcasted_iota(jnp.int32, (BLK, GROUP_W), 1)
        o_acc = jnp.zeros((BLK, GROUP_W), F32)
        l_acc = jnp.zeros((BLK, GROUP_W), F32)
        for h in range(GROUP_W // HEAD_DIM):
            hm = jnp.logical_and(lane >= h * HEAD_DIM, lane < (h + 1) * HEAD_DIM)
            qm = jnp.where(hm, q, jnp.zeros_like(q))
            sc = jnp.where(mask_c, _dot_nt(qm, kc) * scale, NEG_INF)
            sp = jnp.where(mask_p, _dot_nt(qm, kp) * scale, NEG_INF)
            m = jnp.maximum(jnp.max(sc, axis=1, keepdims=True), jnp.max(sp, axis=1, keepdims=True))
            pc, pp = jnp.exp(sc - m), jnp.exp(sp - m)
            den = jnp.sum(pc, axis=1, keepdims=True) + jnp.sum(pp, axis=1, keepdims=True)
            oh = (_dot(pc.astype(BF16), vc) + _dot(pp.astype(BF16), vp)) / den
            o_acc = jnp.where(hm, oh, o_acc)
            l_acc = jnp.where(hm, m + jnp.log(den), l_acc)
        o_ref[...] = o_acc
        l_ref[...] = l_acc

    ident = lambda n: n
    out_spec = pl.BlockSpec((BLK, GROUP_W), lambda r, n: (n, r))
    o, lse = pl.pallas_call(
        body, name=name, grid=(dil, nb), in_specs=_attn_specs(g, 3, 9, ident), out_specs=[out_spec, out_spec],
        out_shape=[jax.ShapeDtypeStruct((rows, dil * GROUP_W), F32)] * 2, compiler_params=_params(2),
    )(q_r.reshape(rows, dil * QKV_W), k_r.reshape(rows, dil * QKV_W), k_r.reshape(rows, dil * QKV_W),
      qkv.reshape(rows, dil * 3 * QKV_W), qkv.reshape(rows, dil * 3 * QKV_W))
    return o.reshape(s, GROUP_W), lse.reshape(s, GROUP_W)


def _attn_merge(name, os_, ls_):
    def body(rows, _v):
        o0, o1, o2, l0, l1, l2 = rows
        m = jnp.maximum(jnp.maximum(l0, l1), l2)
        e0, e1, e2 = jnp.exp(l0 - m), jnp.exp(l1 - m), jnp.exp(l2 - m)
        tot = e0 + e1 + e2
        return [(e0 * o0 + e1 * o1 + e2 * o2) / tot, m + jnp.log(tot)], []

    return _rowwise(name, body, [*os_, *ls_], [], [(GROUP_W, BF16), (GROUP_W, F32)])


def _attn_bwd(name, q_r, k_r, qkv, do, o, lse, g, dil):
    s = q_r.shape[0]
    rows = s // dil
    nb = rows // BLK
    scale = HEAD_DIM ** -0.5

    def body(q_ref, kc_ref, kp_ref, vc_ref, vp_ref, do_ref, o_ref, l_ref, dq_ref, dk_ref, dv_ref, ck_ref, cv_ref):
        n = pl.program_id(1)

        @pl.when(n == 0)
        def _():
            ck_ref[...] = jnp.zeros_like(ck_ref)
            cv_ref[...] = jnp.zeros_like(cv_ref)

        @pl.when(n < nb)
        def _():
            mask_c, mask_p = _band_masks(n > 0)
            q, kc, kp, vc, vp, dov = q_ref[...], kc_ref[...], kp_ref[...], vc_ref[...], vp_ref[...], do_ref[...]
            lb = l_ref[...]
            prod = dov.astype(F32) * o_ref[...].astype(F32)
            lane = lax.broadcasted_iota(jnp.int32, (BLK, GROUP_W), 1)
            zero = jnp.zeros((BLK, GROUP_W), F32)
            dq_acc, dkc, dkp, dvc, dvp = zero, zero, zero, zero, zero
            for h in range(GROUP_W // HEAD_DIM):
                hm = jnp.logical_and(lane >= h * HEAD_DIM, lane < (h + 1) * HEAD_DIM)
                qm = jnp.where(hm, q, jnp.zeros_like(q))
                dom = jnp.where(hm, dov, jnp.zeros_like(dov))
                lh = jnp.max(jnp.where(hm, lb, NEG_INF), axis=1, keepdims=True)
                delta = jnp.sum(jnp.where(hm, prod, 0.0), axis=1, keepdims=True)
                pc = jnp.exp(jnp.where(mask_c, _dot_nt(qm, kc) * scale, NEG_INF) - lh)
                pp = jnp.exp(jnp.where(mask_p, _dot_nt(qm, kp) * scale, NEG_INF) - lh)
                dsc = (pc * (_dot_nt(dom, vc) - delta) * scale).astype(BF16)
                dsp = (pp * (_dot_nt(dom, vp) - delta) * scale).astype(BF16)
                dq_acc = jnp.where(hm, _dot(dsc, kc) + _dot(dsp, kp), dq_acc)
                dkc += _dot_tn(dsc, qm)
                dkp += _dot_tn(dsp, qm)
                dvc += _dot_tn(pc.astype(BF16), dom)
                dvp += _dot_tn(pp.astype(BF16), dom)
            dq_ref[...] = dq_acc.astype(dq_ref.dtype)
            dk_ref[...] = (ck_ref[...] + dkp).astype(dk_ref.dtype)
            dv_ref[...] = (cv_ref[...] + dvp).astype(dv_ref.dtype)
            ck_ref[...] = dkc
            cv_ref[...] = dvc

        @pl.when(n == nb)
        def _():
            dk_ref[...] = ck_ref[...].astype(dk_ref.dtype)
            dv_ref[...] = cv_ref[...].astype(dv_ref.dtype)

    clamp = lambda n: jnp.minimum(n, nb - 1)
    blk = (BLK, GROUP_W)
    qspec = pl.BlockSpec(blk, lambda r, n: (clamp(n), r))
    kspec = pl.BlockSpec(blk, lambda r, n: (jnp.maximum(n - 1, 0), r))
    shp = jax.ShapeDtypeStruct((rows, dil * GROUP_W), BF16)
    dq, dk, dv = pl.pallas_call(
        body, name=name, grid=(dil, nb + 1), in_specs=_attn_specs(g, 3, 9, clamp) + [qspec, qspec, qspec],
        out_specs=[qspec, kspec, kspec], out_shape=[shp] * 3,
        scratch_shapes=[pltpu.VMEM(blk, F32), pltpu.VMEM(blk, F32)], compiler_params=_params(2),
    )(q_r.reshape(rows, dil * QKV_W), k_r.reshape(rows, dil * QKV_W), k_r.reshape(rows, dil * QKV_W),
      qkv.reshape(rows, dil * 3 * QKV_W), qkv.reshape(rows, dil * 3 * QKV_W),
      do.reshape(rows, dil * GROUP_W), o.reshape(rows, dil * GROUP_W), lse.reshape(rows, dil * GROUP_W))
    return dq.reshape(s, GROUP_W), dk.reshape(s, GROUP_W), dv.reshape(s, GROUP_W)


CONV_TS = 128
HALO = 32


def _conv_fwd(name, u, w, b, lg, lb):
    s = u.shape[0]
    ts, per = CONV_TS, CONV_TS // HALO

    def body(a_ref, g_ref, ap_ref, gp_ref, w_ref, b_ref, lg_ref, lb_ref, c_ref, act_ref, buf, cbuf):
        i = pl.program_id(0)
        buf[pl.ds(HALO, ts), :] = a_ref[...].astype(F32) * _sigmoid(g_ref[...].astype(F32))
        prev = ap_ref[...].astype(F32) * _sigmoid(gp_ref[...].astype(F32))
        buf[pl.ds(0, HALO), :] = jnp.where(i > 0, prev, 0.0)
        for lc in range(D // LANES):
            cols = pl.ds(lc * LANES, LANES)
            acc = jnp.broadcast_to(b_ref[:, cols], (ts, LANES))
            for j in range(CONV_K):
                acc = acc + w_ref[pl.ds(j, 1), cols] * buf[pl.ds(HALO - (CONV_K - 1) + j, ts), cols]
            cbuf[:, cols] = acc
        c = cbuf[...]
        mu = jnp.mean(c, axis=-1, keepdims=True)
        xc = c - mu
        ln = xc * lax.rsqrt(jnp.mean(xc * xc, axis=-1, keepdims=True) + EPS) * lg_ref[...] + lb_ref[...]
        c_ref[...] = c.astype(c_ref.dtype)
        act_ref[...] = (ln * _sigmoid(ln)).astype(act_ref.dtype)

    halo = lambda cb: pl.BlockSpec((HALO, D), functools.partial(lambda i, cb: (jnp.maximum(i * per - 1, 0), cb), cb=cb))
    vec = pl.BlockSpec((1, D), lambda i: (0, 0))
    return pl.pallas_call(
        body, name=name, grid=(s // ts,),
        in_specs=[pl.BlockSpec((ts, D), lambda i: (i, 0)), pl.BlockSpec((ts, D), lambda i: (i, 1)), halo(0), halo(1),
                  pl.BlockSpec((HALO, D), lambda i: (0, 0)), vec, vec, vec],
        out_specs=[pl.BlockSpec((ts, D), lambda i: (i, 0))] * 2,
        out_shape=[jax.ShapeDtypeStruct((s, D), BF16)] * 2,
        scratch_shapes=[pltpu.VMEM((ts + HALO, D), F32), pltpu.VMEM((ts, D), F32)],
        compiler_params=_params(1),
    )(u, u, u, u, w, b, lg, lb)


def _ln_bwd(name, dact, c, lg, lb):
    def body(rows, vecs):
        (dv, cv), (g, b) = rows, vecs
        cv = cv.astype(F32)
        mu = jnp.mean(cv, axis=-1, keepdims=True)
        xc = cv - mu
        rstd = lax.rsqrt(jnp.mean(xc * xc, axis=-1, keepdims=True) + EPS)
        xh = xc * rstd
        ln = xh * g + b
        sg = _sigmoid(ln)
        dln = dv * (sg * (1.0 + ln * (1.0 - sg)))
        dxh = dln * g
        dc = rstd * (dxh - jnp.mean(dxh, axis=-1, keepdims=True) - xh * jnp.mean(dxh * xh, axis=-1, keepdims=True))
        return [dc], [_colsum(dln * xh), _colsum(dln)]

    return _rowwise(name, body, [dact, c], [lg, lb], [(D, F32)], [(1, D), (1, D)])


def _conv_bwd(name, dc, u, w):
    s = u.shape[0]
    ts, per = CONV_TS, CONV_TS // HALO
    n_t = s // ts

    def body(dc_ref, dn_ref, a_ref, g_ref, ap_ref, gp_ref, w_ref, du_ref, dw_ref, db_ref, buf, dbuf, hbuf):
        i = pl.program_id(0)
        a = a_ref[...].astype(F32)
        sg = _sigmoid(g_ref[...].astype(F32))
        buf[pl.ds(HALO, ts), :] = a * sg
        prev = ap_ref[...].astype(F32) * _sigmoid(gp_ref[...].astype(F32))
        buf[pl.ds(0, HALO), :] = jnp.where(i > 0, prev, 0.0)
        dcv = dc_ref[...]
        dbuf[pl.ds(0, ts), :] = dcv
        dbuf[pl.ds(ts, HALO), :] = jnp.where(i < n_t - 1, dn_ref[...], 0.0)

        @pl.when(i == 0)
        def _():
            dw_ref[...] = jnp.zeros_like(dw_ref)
            db_ref[...] = jnp.zeros_like(db_ref)

        db_ref[...] += _colsum(dcv)
        for lc in range(D // LANES):
            cols = pl.ds(lc * LANES, LANES)
            d0 = dbuf[pl.ds(0, ts), cols]
            acc = jnp.zeros((ts, LANES), F32)
            for j in range(CONV_K):
                acc = acc + w_ref[pl.ds(j, 1), cols] * dbuf[pl.ds(CONV_K - 1 - j, ts), cols]
                part = d0 * buf[pl.ds(HALO - (CONV_K - 1) + j, ts), cols]
                dw_ref[pl.ds(8 * j, 8), cols] += jnp.sum(part.reshape(ts // 8, 8, LANES), axis=0)
            hbuf[:, cols] = acc
        dh = hbuf[...]
        du_ref[:, pl.ds(0, D)] = (dh * sg).astype(du_ref.dtype)
        du_ref[:, pl.ds(D, D)] = (dh * a * sg * (1.0 - sg)).astype(du_ref.dtype)

    halo = lambda cb: pl.BlockSpec((HALO, D), functools.partial(lambda i, cb: (jnp.maximum(i * per - 1, 0), cb), cb=cb))
    nxt = pl.BlockSpec((HALO, D), lambda i: (jnp.minimum((i + 1) * per, s // HALO - 1), 0))
    return pl.pallas_call(
        body, name=name, grid=(n_t,),
        in_specs=[pl.BlockSpec((ts, D), lambda i: (i, 0)), nxt, pl.BlockSpec((ts, D), lambda i: (i, 0)),
                  pl.BlockSpec((ts, D), lambda i: (i, 1)), halo(0), halo(1), pl.BlockSpec((HALO, D), lambda i: (0, 0))],
        out_specs=[pl.BlockSpec((ts, 2 * D), lambda i: (i, 0)), pl.BlockSpec((8 * CONV_K, D), lambda i: (0, 0)),
                   pl.BlockSpec((1, D), lambda i: (0, 0))],
        out_shape=[jax.ShapeDtypeStruct((s, 2 * D), BF16), jax.ShapeDtypeStruct((8 * CONV_K, D), F32),
                   jax.ShapeDtypeStruct((1, D), F32)],
        scratch_shapes=[pltpu.VMEM((ts + HALO, D), F32), pltpu.VMEM((ts + HALO, D), F32), pltpu.VMEM((ts, D), F32)],
        compiler_params=_params(1),
    )(dc, dc, u, u, u, u, w)


def _mix_fwd(tag, x1, ng, sc, sh, gate, wts, cos2, sin2):
    h1 = _modulate(f"{tag}_mod", x1, ng, sc, sh)
    ident = lambda accs, _t, _v: accs
    qkv = _mm(f"{tag}_qkv", [(h1, wts["w_qkv"])], ident, [BF16], tn=QKV_W)[0]
    u = _mm(f"{tag}_u", [(h1, wts["w_u"])], ident, [BF16], tn=D)[0]
    ga, gc = _mm(f"{tag}_gates", [(h1, wts["w_ga"]), (h1, wts["w_gc"])], ident, [BF16] * 2)
    q_r, k_r = _rope_fwd(f"{tag}_rope", qkv, cos2, sin2)
    per_group = [_attn_fwd(f"{tag}_attn{g}", q_r, k_r, qkv, g, dil) for g, dil in enumerate(ATTN_DILATIONS)]
    o, lse = _attn_merge(f"{tag}_merge", [p[0] for p in per_group], [p[1] for p in per_group])
    cpre, act = _conv_fwd(f"{tag}_conv", u, wts["conv_w"], wts["conv_b"], wts["ln_g"], wts["ln_b"])

    def gate_epi(accs, tl, _v):
        ya, yc = accs
        return [_sigmoid(tl[0].astype(F32)) * ya + _sigmoid(tl[1].astype(F32)) * yc, ya, yc]

    y, ya, yc = _mm(f"{tag}_branch", [(o, wts["attn_wo"]), (act, wts["conv_wo"])], gate_epi, [BF16] * 3,
                    tiles=[(ga, 0), (gc, 0)])

    def res_epi(accs, tl, vs):
        return [tl[0] + vs[0] * accs[0], accs[0]]

    x2, f1 = _mm(f"{tag}_out", [(y, wts["w_out"])], res_epi, [F32, BF16], tiles=[(x1, 0)], vecs=[gate])
    return x2, (x1, h1, qkv, u, ga, gc, q_r, k_r, o, lse, cpre, act, y, ya, yc, f1)


def _mix_bwd(tag, dx2, saved, ng, sc, gate, wts, cos2, sin2):
    x1, h1, qkv, u, ga, gc, q_r, k_r, o, lse, cpre, act, y, ya, yc, f1 = saved
    dfm, dgate = _gate_scale(f"{tag}_gs", dx2, f1, gate, 1.0)

    def epi(accs, tl, _v):
        dy = accs[0]
        sa, sc_ = _sigmoid(tl[0].astype(F32)), _sigmoid(tl[1].astype(F32))
        return [dy * sa, dy * sc_, dy * tl[2].astype(F32) * sa * (1.0 - sa), dy * tl[3].astype(F32) * sc_ * (1.0 - sc_)]

    dya, dyc, dga, dgc = _mm(f"{tag}_bout", [(dfm, wts["w_out"])], epi, [BF16] * 4, nt=True,
                             tiles=[(ga, 0), (gc, 0), (ya, 0), (yc, 0)])
    grads = {"w_out": _mm_tn(f"{tag}_dwout", y, dfm), "attn_wo": _mm_tn(f"{tag}_dwattn", o, dya),
             "conv_wo": _mm_tn(f"{tag}_dwconv", act, dyc)}
    ident = lambda accs, _t, _v: accs
    do = _mm(f"{tag}_battn", [(dya, wts["attn_wo"])], ident, [BF16], nt=True)[0]
    dact = _mm(f"{tag}_bconv", [(dyc, wts["conv_wo"])], ident, [F32], nt=True)[0]
    dc, dlg, dlb = _ln_bwd(f"{tag}_lnb", dact, cpre, wts["ln_g"], wts["ln_b"])
    du, dw8, dcb = _conv_bwd(f"{tag}_convb", dc, u, wts["conv_w"])
    dqs, dks, dvs = [], [], []
    for g, dil in enumerate(ATTN_DILATIONS):
        dq, dk, dv = _attn_bwd(f"{tag}_attnb{g}", q_r, k_r, qkv, do, o, lse, g, dil)
        dqs.append(dq); dks.append(dk); dvs.append(dv)
    dqkv = _rope_bwd(f"{tag}_ropeb", dqs, dks, dvs, cos2, sin2)
    grads["w_in"] = jnp.concatenate([_mm_tn(f"{tag}_dwqkv", h1, dqkv), _mm_tn(f"{tag}_dwu", h1, du),
                                     _mm_tn(f"{tag}_dwga", h1, dga), _mm_tn(f"{tag}_dwgc", h1, dgc)], axis=1)
    dh1 = _mm(f"{tag}_bin", [(dqkv, wts["w_qkv"]), (du, wts["w_u"]), (dga, wts["w_ga"]), (dgc, wts["w_gc"])],
              lambda accs, _t, _v: [accs[0] + accs[1] + accs[2] + accs[3]], [F32], nt=True)[0]
    dx1, dsh, dsc, dng = _modulate_bwd(f"{tag}_modb", dh1, x1, dx2, ng, sc)
    small = {"conv_w": dw8.reshape(CONV_K, 8, D).sum(axis=1), "conv_b": dcb, "ln_g": dlg, "ln_b": dlb}
    return dx1, (dsh, dsc, dgate, dng), grads, small


def _loss_head(name, x, target, fg):
    def body(rows, vecs):
        (xv, tv), (g,) = rows, vecs
        xhat, r = _rms_parts(xv)
        err = xhat * g - tv
        dy = err * (1.0 / D)
        return [_rms_bwd(dy * g, xhat, r)], [_colsum(err * err), _colsum(dy * xhat)]

    return _rowwise(name, body, [x, target], [fg], [(D, F32)], [(1, D), (1, D)])


def _local_step(x, target, mod, norm_g, wts, final_g):
    s = x.shape[0]
    cos2, sin2 = _rope_tables(s)
    row = lambda a: a[None, :]
    saved = []
    for l in range(2):
        m = lambda i: row(mod[l, i])
        w = wts[l]
        x, s0 = _ffn_fwd(f"l{l}f0", x, row(norm_g[l, 0]), m(1), m(0), m(2), w["wg0"], w["wu0"], w["wd0"])
        x, s1 = _mix_fwd(f"l{l}mx", x, row(norm_g[l, 1]), m(4), m(3), m(5), w, cos2, sin2)
        x, s2 = _ffn_fwd(f"l{l}f1", x, row(norm_g[l, 2]), m(7), m(6), m(8), w["wg1"], w["wu1"], w["wd1"])
        saved.append((s0, s1, s2))
    dx, sq, dfg = _loss_head("loss_head", x, target, row(final_g))
    loss = (0.5 / D) * jnp.sum(sq)
    dmod, dng, big, small = [None, None], [None, None], [None, None], [None, None]
    for l in (1, 0):
        m = lambda i: row(mod[l, i])
        w = wts[l]
        s0, s1, s2 = saved[l]
        dx, v2, (dwg1, dwu1, dwd1) = _ffn_bwd(f"l{l}f1", dx, s2, row(norm_g[l, 2]), m(7), m(8), w["wg1"], w["wu1"], w["wd1"])
        dx, v1, gmix, small[l] = _mix_bwd(f"l{l}mx", dx, s1, row(norm_g[l, 1]), m(4), m(5), w, cos2, sin2)
        dx, v0, (dwg0, dwu0, dwd0) = _ffn_bwd(f"l{l}f0", dx, s0, row(norm_g[l, 0]), m(1), m(2), w["wg0"], w["wu0"], w["wd0"])
        dmod[l] = jnp.concatenate([jnp.concatenate(v[:3], axis=0) for v in (v0, v1, v2)], axis=0)
        dng[l] = jnp.concatenate([v0[3], v1[3], v2[3]], axis=0)
        big[l] = dict(gmix, ffn_wg=jnp.stack([dwg0, dwg1]), ffn_wu=jnp.stack([dwu0, dwu1]), ffn_wd=jnp.stack([dwd0, dwd1]))
    big = {k: jnp.stack([big[0][k], big[1][k]]) for k in big[0]}
    small = {k: jnp.stack([small[0][k].reshape(-1, D), small[1][k].reshape(-1, D)]) for k in small[0]}
    small = dict(small, norm_g=jnp.stack(dng), final_g=dfg.reshape(D))
    return loss, dx, jnp.stack(dmod), small, big


HBM_SPEC = pl.BlockSpec(memory_space=pl.ANY)


def _place():
    return lax.axis_index("x"), lax.axis_index("y"), lax.axis_index("c")


def _all_gather(name, xs):
    def body(x_ref, out_ref, send_sems, recv_sems, local_sem):
        x, y, c = _place()
        me, sibling = (x, y, c), (x, y, 1 - c)
        chips = [(1 - x, y), (x, 1 - y), (1 - x, 1 - y)]

        def blk(px, py, pc):
            return out_ref.at[4 * px + 2 * py + pc]

        def copy(k, block, to, src=None):
            return pltpu.make_async_remote_copy(
                src_ref=blk(*block) if src is None else src, dst_ref=blk(*block), send_sem=send_sems.at[k],
                recv_sem=recv_sems.at[k], device_id=to, device_id_type=pl.DeviceIdType.MESH)

        mine = pltpu.make_async_copy(x_ref, blk(*me), local_sem)
        mine.start()
        first = [copy(0, me, sibling, src=x_ref)] + [copy(1 + j, me, (*chip, c), src=x_ref) for j, chip in enumerate(chips)]
        for cp in first:
            cp.start()
        passed = [copy(4 + j, (*chip, c), sibling) for j, chip in enumerate(chips)]
        for j, chip in enumerate(chips):
            copy(1 + j, (*chip, c), me).wait_recv()
            passed[j].start()
        copy(0, sibling, me).wait_recv()
        for j, chip in enumerate(chips):
            copy(4 + j, (*chip, 1 - c), me).wait_recv()
        for cp in first + passed:
            cp.wait_send()
        mine.wait()

    return pl.pallas_call(
        body, name=name, out_shape=jax.ShapeDtypeStruct((N_DEV, *xs.shape), xs.dtype), in_specs=[HBM_SPEC], out_specs=HBM_SPEC,
        scratch_shapes=[pltpu.SemaphoreType.DMA((7,)), pltpu.SemaphoreType.DMA((7,)), pltpu.SemaphoreType.DMA(())],
    )(xs)


def _exchange_blocks(name, g):
    def body(g_ref, out_ref, send_sems, recv_sems, local_sem):
        x, y, c = _place()
        my = 4 * x + 2 * y + c
        local = pltpu.make_async_copy(g_ref.at[my], out_ref.at[my], local_sem)
        local.start()
        copies = []
        for k in range(1, N_DEV):
            px = 1 - x if k & 4 else x
            py = 1 - y if k & 2 else y
            pc = 1 - c if k & 1 else c
            copies.append(pltpu.make_async_remote_copy(
                src_ref=g_ref.at[4 * px + 2 * py + pc], dst_ref=out_ref.at[my], send_sem=send_sems.at[k - 1],
                recv_sem=recv_sems.at[k - 1], device_id=(px, py, pc), device_id_type=pl.DeviceIdType.MESH))
        for cp in copies:
            cp.start()
        for cp in copies:
            cp.wait()
        local.wait()

    return pl.pallas_call(
        body, name=name, out_shape=jax.ShapeDtypeStruct(g.shape, g.dtype), in_specs=[HBM_SPEC], out_specs=HBM_SPEC,
        scratch_shapes=[pltpu.SemaphoreType.DMA((7,)), pltpu.SemaphoreType.DMA((7,)), pltpu.SemaphoreType.DMA(())],
    )(g)


SLAB_TS = 2048


def _sum_parts(name, parts):
    def body(rows, _v):
        tot = rows[0].astype(F32)
        for r in rows[1:]:
            tot = tot + r.astype(F32)
        return [tot], []

    return _rowwise(name, body, list(parts), [], [(parts[0].shape[1], F32)], ts=SLAB_TS)[0]


def _adamw(name, w, parts, m, v):
    def body(rows, _v):
        wv, mv, vv = rows[0], rows[1], rows[2]
        g = rows[3].astype(F32)
        for r in rows[4:]:
            g = g + r.astype(F32)
        m2 = ADAM_B1 * mv + (1.0 - ADAM_B1) * g
        v2 = ADAM_B2 * vv + (1.0 - ADAM_B2) * (g * g)
        m_hat = m2 / (1.0 - ADAM_B1 ** ADAM_STEP)
        v_hat = v2 / (1.0 - ADAM_B2 ** ADAM_STEP)
        delta = -ADAM_LR * (m_hat / (jnp.sqrt(v_hat) + ADAM_EPS) + ADAM_WD * wv)
        return [g, delta, m2, v2], []

    width = w.shape[1]
    return _rowwise(name, body, [w, m, v, *parts], [], [(width, F32)] * 4, ts=SLAB_TS)


def _to_slab(arrays, dtype):
    flat = jnp.concatenate([a.reshape(-1).astype(dtype) for a in arrays])
    rows = -(-flat.shape[0] // LANES)
    rows = -(-rows // 8) * 8 if rows <= SLAB_TS else -(-rows // SLAB_TS) * SLAB_TS
    return jnp.pad(flat, (0, rows * LANES - flat.shape[0])).reshape(rows, LANES)


def _from_slab(slab, shapes, lead=()):
    flat = slab.reshape(*lead, -1)
    out, at = [], 0
    for shp in shapes:
        size = 1
        for d in shp:
            size *= d
        out.append(flat[..., at:at + size].reshape(*lead, *shp))
        at += size
    return out


BIG = {"ffn_wg": 3, "ffn_wu": 3, "ffn_wd": 2, "w_in": 2, "attn_wo": 2, "conv_wo": 1, "w_out": 1}


def _join(blocks, axis):
    full = jnp.moveaxis(blocks, 0, axis)
    return full.reshape(*full.shape[:axis], -1, *full.shape[axis + 2:])


def _split(full, axis):
    shp = full.shape
    return jnp.moveaxis(full.reshape(*shp[:axis], N_DEV, shp[axis] // N_DEV, *shp[axis + 1:]), axis, 0)


def kernel(x, c, ada_w, ada_b, norm_g, ffn_wg, ffn_wu, ffn_wd, w_in, attn_wo, conv_w, conv_b, conv_ln_g, conv_ln_b, conv_wo, w_out, final_g, loss_target, m_ada_w, m_ada_b, m_norm_g, m_ffn_wg, m_ffn_wu, m_ffn_wd, m_w_in, m_attn_wo, m_conv_w, m_conv_b, m_conv_ln_g, m_conv_ln_b, m_conv_wo, m_w_out, m_final_g, v_ada_w, v_ada_b, v_norm_g, v_ffn_wg, v_ffn_wu, v_ffn_wd, v_w_in, v_attn_wo, v_conv_w, v_conv_b, v_conv_ln_g, v_conv_ln_b, v_conv_wo, v_w_out, v_final_g):
    px, py, pc = _place()
    me = 4 * px + 2 * py + pc
    n_mod = ada_w.shape[2]
    big_w = dict(ffn_wg=ffn_wg, ffn_wu=ffn_wu, ffn_wd=ffn_wd, w_in=w_in, attn_wo=attn_wo, conv_wo=conv_wo, w_out=w_out)
    big_m = dict(ffn_wg=m_ffn_wg, ffn_wu=m_ffn_wu, ffn_wd=m_ffn_wd, w_in=m_w_in, attn_wo=m_attn_wo, conv_wo=m_conv_wo, w_out=m_w_out)
    big_v = dict(ffn_wg=v_ffn_wg, ffn_wu=v_ffn_wu, ffn_wd=v_ffn_wd, w_in=v_w_in, attn_wo=v_attn_wo, conv_wo=v_conv_wo, w_out=v_w_out)
    big_shapes = [big_w[k].shape for k in BIG]

    small_in = [c, norm_g, conv_w]
    g1 = _all_gather("gather_small", _to_slab(small_in, F32))
    c_all, ng_blocks, cw_blocks = _from_slab(g1, [a.shape for a in small_in], lead=(N_DEV,))
    c_all = c_all.reshape(N_DEV, D)
    norm_g_full = _join(ng_blocks, 2)
    conv_w_full = _join(cw_blocks, 2)
    gw = _all_gather("gather_weights", _to_slab([big_w[k] for k in BIG], BF16))
    full = {k: _join(blk, BIG[k]) for k, blk in zip(BIG, _from_slab(gw, big_shapes, lead=(N_DEV,)))}

    c_act = _rowwise("cond_silu", lambda rows, _v: ([rows[0] * _sigmoid(rows[0])], []), [c_all], [], [(D, BF16)])[0]
    c_pad = jnp.pad(c_act, ((0, LANES - N_DEV), (0, 0)))
    mod_cols = []
    for l in range(2):
        bias = lax.dynamic_slice_in_dim(ada_b[l], me * n_mod, n_mod)[None, :]
        out = _mm(f"mod{l}", [(c_pad, ada_w[l].astype(BF16))], lambda accs, _t, vs: [accs[0] + vs[0]], [F32], vecs=[bias])[0]
        mod_cols.append(out[:N_DEV])
    g2 = _all_gather("gather_mod", _to_slab([jnp.stack(mod_cols)], F32))
    mod_all = _from_slab(g2, [(2, N_DEV, n_mod)], lead=(N_DEV,))[0]
    mod = lax.dynamic_index_in_dim(mod_all, me, axis=2, keepdims=False)
    mod = jnp.moveaxis(mod, 0, 1).reshape(2, 9, D)

    wts = []
    for l in range(2):
        w_in_l = full["w_in"][l]
        cw = jnp.pad(conv_w_full[l], ((0, HALO - CONV_K), (0, 0)))
        wts.append(dict(
            wg0=full["ffn_wg"][l, 0], wu0=full["ffn_wu"][l, 0], wd0=full["ffn_wd"][l, 0],
            wg1=full["ffn_wg"][l, 1], wu1=full["ffn_wu"][l, 1], wd1=full["ffn_wd"][l, 1],
            w_qkv=w_in_l[:, :3 * QKV_W], w_u=w_in_l[:, 3 * QKV_W:3 * QKV_W + 2 * D],
            w_ga=w_in_l[:, 3 * QKV_W + 2 * D:3 * QKV_W + 3 * D], w_gc=w_in_l[:, 3 * QKV_W + 3 * D:],
            attn_wo=full["attn_wo"][l], conv_wo=full["conv_wo"][l], w_out=full["w_out"][l],
            conv_w=cw, conv_b=conv_b[l][None, :], ln_g=conv_ln_g[l][None, :], ln_b=conv_ln_b[l][None, :]))

    loss, dx, dmod, small, big = _local_step(x[0], loss_target[0], mod, norm_g_full, wts, final_g)
    loss = lax.psum(loss, MESH_AXES)

    small_names = ["norm_g", "conv_w", "conv_b", "ln_g", "ln_b", "final_g"]
    small_parts = [dmod] + [small[k] for k in small_names]
    g3 = _all_gather("gather_small_grads", _to_slab(small_parts, F32))
    tot = _sum_parts("sum_small_grads", [g3[k] for k in range(N_DEV)])
    _, g_ng, g_cw, g_cb, g_lg, g_lb, g_fg = _from_slab(tot, [a.shape for a in small_parts])
    g_ab = _from_slab(tot, [(2, 9 * D)])[0]
    dmod_all = _from_slab(g3, [dmod.shape], lead=(N_DEV,))[0].reshape(N_DEV, 2, 9 * D)
    dm_mine = lax.dynamic_slice_in_dim(dmod_all, me * n_mod, n_mod, axis=2)
    g_aw = jnp.stack([
        _mm_tn(f"dada_w{l}", c_pad, jnp.pad(dm_mine[:, l], ((0, LANES - N_DEV), (0, 0))).astype(BF16), out_dtype=F32)
        for l in range(2)])
    cols = lambda a: lax.dynamic_slice_in_dim(a, me * (D // N_DEV), D // N_DEV, axis=2)
    small_w = [ada_w, ada_b, norm_g, conv_w, conv_b, conv_ln_g, conv_ln_b, final_g]
    small_m = [m_ada_w, m_ada_b, m_norm_g, m_conv_w, m_conv_b, m_conv_ln_g, m_conv_ln_b, m_final_g]
    small_v = [v_ada_w, v_ada_b, v_norm_g, v_conv_w, v_conv_b, v_conv_ln_g, v_conv_ln_b, v_final_g]
    small_g = [g_aw, g_ab, cols(g_ng), cols(g_cw), g_cb, g_lg, g_lb, g_fg]
    s_shapes = [a.shape for a in small_w]
    s_out = _adamw("adamw_small", _to_slab(small_w, F32), [_to_slab(small_g, F32)], _to_slab(small_m, F32), _to_slab(small_v, F32))
    s_grad, s_delta, s_m, s_v = [_from_slab(o, s_shapes) for o in s_out]

    send = jnp.stack([_to_slab([_split(big[k], BIG[k])[j] for k in BIG], BF16) for j in range(N_DEV)])
    got = _exchange_blocks("exchange_grads", send)
    b_out = _adamw("adamw_big", _to_slab([big_w[k] for k in BIG], F32), [got[k] for k in range(N_DEV)],
                   _to_slab([big_m[k] for k in BIG], F32), _to_slab([big_v[k] for k in BIG], F32))
    b_grad, b_delta, b_m, b_v = [dict(zip(BIG, _from_slab(o, big_shapes))) for o in b_out]

    def ordered(sm, bg):
        aw, ab, ng, cw, cb, lg, lb, fg = sm
        return [aw, ab, ng, bg["ffn_wg"], bg["ffn_wu"], bg["ffn_wd"], bg["w_in"], bg["attn_wo"], cw, cb, lg, lb,
                bg["conv_wo"], bg["w_out"], fg]

    return (loss, dx[None], *ordered(s_grad, b_grad), *ordered(s_delta, b_delta), *ordered(s_m, b_m), *ordered(s_v, b_v))
```

```python
import functools

import jax
import jax.numpy as jnp
from jax import lax
from jax.experimental import pallas as pl
from jax.experimental.pallas import tpu as pltpu

F32 = jnp.float32
BF16 = jnp.bfloat16

N_DEV = 8
D = 1024
D_FF = 2816
HEAD_DIM = 64
GROUP_W = 256
ATTN_DILATIONS = (1, 4, 16)
BLK = 128
QKV_W = 768
CONV_K = 31
ROPE_THETA = 10000.0
EPS = 1e-6
NEG_INF = -1e30
ADAM_LR, ADAM_B1, ADAM_B2, ADAM_EPS, ADAM_WD, ADAM_STEP = 0.001, 0.9, 0.999, 1e-08, 0.01, 10

V7X_VMEM_BYTES = 64 * 1024 * 1024
VMEM_LIMIT = V7X_VMEM_BYTES - 8 * 1024 * 1024
LANES = 128
MESH_AXES = ("x", "y", "c")


def _params(n_grid):
    return pltpu.CompilerParams(vmem_limit_bytes=VMEM_LIMIT, dimension_semantics=("arbitrary",) * n_grid)


def _sigmoid(v):
    return 1.0 / (1.0 + jnp.exp(-v))


def _mm(name, prods, epilogue, out_dtypes, *, nt=False, tiles=(), vecs=(), tm=512, tn=None, a_pre=None):
    s = prods[0][0].shape[0]
    n = prods[0][1].shape[0] if nt else prods[0][1].shape[1]
    tn = n if tn is None else tn
    tm = min(tm, s)
    assert s % tm == 0 and n % tn == 0
    n_p, n_t, n_v = len(prods), len(tiles), len(vecs)
    dn = (((1,), (1,)), ((), ())) if nt else (((1,), (0,)), ((), ()))

    def body(*refs):
        p_refs, rest = refs[:2 * n_p], refs[2 * n_p:]
        t_refs, v_refs, o_refs = rest[:n_t], rest[n_t:n_t + n_v], rest[n_t + n_v:]
        accs = []
        for p in range(n_p):
            a = p_refs[2 * p][...]
            if a_pre is not None:
                a = a_pre(a)
            accs.append(lax.dot_general(a, p_refs[2 * p + 1][...], dn, preferred_element_type=F32))
        outs = epilogue(accs, [t[...] for t in t_refs], [v[...] for v in v_refs])
        for o_ref, o in zip(o_refs, outs, strict=True):
            o_ref[...] = o.astype(o_ref.dtype)

    in_specs = []
    operands = []
    for a, b in prods:
        k = a.shape[1]
        in_specs.append(pl.BlockSpec((tm, k), lambda j, i: (i, 0)))
        in_specs.append(pl.BlockSpec((tn, k), lambda j, i: (j, 0)) if nt else pl.BlockSpec((k, tn), lambda j, i: (0, j)))
        operands += [a, b]
    for arr, off in tiles:
        in_specs.append(pl.BlockSpec((tm, tn), functools.partial(lambda j, i, off: (i, j + off), off=off)))
        operands.append(arr)
    for v in vecs:
        in_specs.append(pl.BlockSpec((1, tn), lambda j, i: (0, j)))
        operands.append(v)
    out = pl.pallas_call(
        body, name=name, grid=(n // tn, s // tm), in_specs=in_specs,
        out_specs=[pl.BlockSpec((tm, tn), lambda j, i: (i, j)) for _ in out_dtypes],
        out_shape=[jax.ShapeDtypeStruct((s, n), dt) for dt in out_dtypes],
        compiler_params=_params(2),
    )(*operands)
    return out


def _mm_tn(name, a, b, *, tk=512, tn=None, out_dtype=BF16):
    s, m = a.shape
    n = b.shape[1]
    tn = n if tn is None else tn
    tk = min(tk, s)
    n_k = s // tk
    assert s % tk == 0 and n % tn == 0

    def body(a_ref, b_ref, o_ref, acc_ref):
        k = pl.program_id(1)

        @pl.when(k == 0)
        def _():
            acc_ref[...] = jnp.zeros_like(acc_ref)

        acc_ref[...] += lax.dot_general(a_ref[...], b_ref[...], (((0,), (0,)), ((), ())), preferred_element_type=F32)

        @pl.when(k == n_k - 1)
        def _():
            o_ref[...] = acc_ref[...].astype(o_ref.dtype)

    return pl.pallas_call(
        body, name=name, grid=(n // tn, n_k),
        in_specs=[pl.BlockSpec((tk, m), lambda j, k: (k, 0)), pl.BlockSpec((tk, tn), lambda j, k: (k, j))],
        out_specs=pl.BlockSpec((m, tn), lambda j, k: (0, j)),
        out_shape=jax.ShapeDtypeStruct((m, n), out_dtype),
        scratch_shapes=[pltpu.VMEM((m, tn), F32)],
        compiler_params=_params(2),
    )(a, b)


def _pick_rows(s, target):
    if s <= target:
        return s
    return max(t for t in range(16, target + 1, 16) if s % t == 0)


def _rowwise(name, body, rows, vecs, outs, accs=(), *, ts=256):
    rows = [(r if isinstance(r, tuple) else (r, r.shape[1], 0)) for r in rows]
    rows = [r if len(r) == 4 else (*r, None) for r in rows]
    s = rows[0][0].shape[0]
    ts = _pick_rows(s, ts)
    n_r, n_v, n_o, n_a = len(rows), len(vecs), len(outs), len(accs)

    def kbody(*refs):
        r_refs, v_refs = refs[:n_r], refs[n_r:n_r + n_v]
        o_refs, a_refs = refs[n_r + n_v:n_r + n_v + n_o], refs[n_r + n_v + n_o:]
        res_o, res_a = body([r[...] for r in r_refs], [v[...] for v in v_refs])
        for o_ref, o in zip(o_refs, res_o, strict=True):
            o_ref[...] = o.astype(o_ref.dtype)
        if n_a:
            first = pl.program_id(0) == 0

            @pl.when(first)
            def _():
                for a_ref, a in zip(a_refs, res_a, strict=True):
                    a_ref[...] = a

            @pl.when(jnp.logical_not(first))
            def _():
                for a_ref, a in zip(a_refs, res_a, strict=True):
                    a_ref[...] += a

    in_specs = [
        pl.BlockSpec((ts, w), functools.partial(lambda i, cb: (i, cb), cb=cb)) if lead is None else
        pl.BlockSpec((None, ts, w), functools.partial(lambda i, cb, lead: (lead, i, cb), cb=cb, lead=lead))
        for _, w, cb, lead in rows]
    in_specs += [pl.BlockSpec(v.shape, functools.partial(lambda i, nd: (0,) * nd, nd=v.ndim)) for v in vecs]
    out_specs = [pl.BlockSpec((ts, w), lambda i: (i, 0)) for w, _ in outs]
    out_specs += [pl.BlockSpec(shp, functools.partial(lambda i, nd: (0,) * nd, nd=len(shp))) for shp in accs]
    out_shape = [jax.ShapeDtypeStruct((s, w), dt) for w, dt in outs] + [jax.ShapeDtypeStruct(shp, F32) for shp in accs]
    return pl.pallas_call(
        kbody, name=name, grid=(s // ts,), in_specs=in_specs, out_specs=out_specs, out_shape=out_shape,
        compiler_params=_params(1),
    )(*[r[0] for r in rows], *vecs)


def _colsum(v):
    return jnp.sum(v, axis=0, keepdims=True)


def _rms_parts(x):
    r = lax.rsqrt(jnp.mean(x * x, axis=-1, keepdims=True) + EPS)
    return x * r, r


def _rms_bwd(dxhat, xhat, r):
    return r * (dxhat - xhat * jnp.mean(dxhat * xhat, axis=-1, keepdims=True))


def _modulate(name, x, ng, sc, sh):
    def body(rows, vecs):
        (xv,), (g, s_, b) = rows, vecs
        xhat, _ = _rms_parts(xv)
        return [xhat * g * (1.0 + s_) + b], []

    return _rowwise(name, body, [x], [ng, sc, sh], [(D, BF16)])[0]


def _modulate_bwd(name, dh, x, dxo, ng, sc):
    def body(rows, vecs):
        (dhv, xv, dxov), (g, s_) = rows, vecs
        xhat, r = _rms_parts(xv)
        dn = dhv * (1.0 + s_)
        dx = _rms_bwd(dn * g, xhat, r)
        return [dxov + dx], [_colsum(dhv), _colsum(dhv * xhat * g), _colsum(dn * xhat)]

    return _rowwise(name, body, [dh, x, dxo], [ng, sc], [(D, F32)], [(1, D)] * 3)


def _gate_scale(name, dx, f, gate, coef):
    def body(rows, vecs):
        (dxv, fv), (g,) = rows, vecs
        return [coef * g * dxv], [_colsum(coef * dxv * fv.astype(F32))]

    return _rowwise(name, body, [dx, f], [gate], [(D, BF16)], [(1, D)])


def _ffn_fwd(tag, x, ng, sc, sh, gate, wg, wu, wd):
    h = _modulate(f"{tag}_mod", x, ng, sc, sh)

    def up_epi(accs, _t, _v):
        a, u = accs
        return [a, u, a * _sigmoid(a) * u]

    a, u, t = _mm(f"{tag}_up", [(h, wg), (h, wu)], up_epi, [BF16] * 3, tn=D_FF // 2)

    def down_epi(accs, tl, vs):
        return [tl[0] + 0.5 * vs[0] * accs[0], accs[0]]

    x_out, f = _mm(f"{tag}_down", [(t, wd)], down_epi, [F32, BF16], tiles=[(x, 0)], vecs=[gate])
    return x_out, (x, h, a, u, f)


def _ffn_bwd(tag, dxo, saved, ng, sc, gate, wg, wu, wd):
    x, h, a, u, f = saved
    df, dgate = _gate_scale(f"{tag}_gs", dxo, f, gate, 0.5)

    def epi(accs, tl, _v):
        dt = accs[0]
        av, uv = tl[0].astype(F32), tl[1].astype(F32)
        sg = _sigmoid(av)
        sil = av * sg
        return [dt * uv * (sg * (1.0 + av * (1.0 - sg))), dt * sil, sil * uv]

    da, du, t = _mm(f"{tag}_bdown", [(df, wd)], epi, [BF16] * 3, nt=True, tiles=[(a, 0), (u, 0)], tn=D_FF // 2)
    dwd = _mm_tn(f"{tag}_dwd", t, df)
    dwg = _mm_tn(f"{tag}_dwg", h, da)
    dwu = _mm_tn(f"{tag}_dwu", h, du)
    dh = _mm(f"{tag}_bup", [(da, wg), (du, wu)], lambda accs, _t, _v: [accs[0] + accs[1]], [F32], nt=True)[0]
    dx_in, dsh, dsc, dng = _modulate_bwd(f"{tag}_modb", dh, x, dxo, ng, sc)
    return dx_in, (dsh, dsc, dgate, dng), (dwg, dwu, dwd)


def _rope_tables(s):
    half = HEAD_DIM // 2
    inv_freq = ROPE_THETA ** (-(jnp.arange(half, dtype=F32) * 2.0 / HEAD_DIM))
    ang = jnp.arange(s, dtype=F32)[:, None] * inv_freq[None, :]
    cos, sin = jnp.cos(ang), jnp.sin(ang)
    return jnp.tile(jnp.concatenate([cos, cos], axis=1), (1, 2)), jnp.tile(jnp.concatenate([-sin, sin], axis=1), (1, 2))


def _rotate(v, cos2, sin2, sign):
    w = v.shape[1]
    lane = lax.broadcasted_iota(jnp.int32, v.shape, 1)
    partner = jnp.where(lane % HEAD_DIM < HEAD_DIM // 2, pltpu.roll(v, w - HEAD_DIM // 2, 1), pltpu.roll(v, HEAD_DIM // 2, 1))
    reps = w // LANES
    return v * jnp.tile(cos2, (1, reps)) + partner * (sign * jnp.tile(sin2, (1, reps)))


SPLIT_TS = 512


def _dilated_spec(dil, ts):
    return pl.BlockSpec((dil, ts // dil, GROUP_W), lambda i: (0, i, 0))


def _dilated_shape(s, dil, dtype):
    return jax.ShapeDtypeStruct((dil, s // dil, GROUP_W), dtype)


CHUNKS_PER_GROUP = GROUP_W // LANES


def _put(buf, chunk0, val):
    for c in range(val.shape[1] // LANES):
        buf[chunk0 + c] = val[:, c * LANES:(c + 1) * LANES]


def _get(buf, chunk0, n):
    return jnp.concatenate([buf[chunk0 + c] for c in range(n)], axis=1)


def _strided_rows(r, dil, ts):
    return pl.ds(r, ts // dil, stride=dil) if dil > 1 else pl.ds(0, ts)


def _deinterleave_one(buf, chunk0, out_ref, dil, ts):
    for half in range(CHUNKS_PER_GROUP):
        for r in range(dil):
            src = buf.at[chunk0 + half][_strided_rows(r, dil, ts), :]
            out_ref.at[r][:, pl.ds(half * LANES, LANES)] = src.astype(out_ref.dtype)


def _interleave_one(in_ref, buf, chunk0, dil, ts):
    for half in range(CHUNKS_PER_GROUP):
        for r in range(dil):
            src = in_ref.at[r][:, pl.ds(half * LANES, LANES)]
            buf.at[chunk0 + half][_strided_rows(r, dil, ts), :] = src.astype(F32)


def _deinterleave(buf, chunk0, out_refs, ts):
    for g, dil in enumerate(ATTN_DILATIONS):
        _deinterleave_one(buf, chunk0 + g * CHUNKS_PER_GROUP, out_refs[g], dil, ts)


def _interleave(in_refs, buf, chunk0, ts):
    for g, dil in enumerate(ATTN_DILATIONS):
        _interleave_one(in_refs[g], buf, chunk0 + g * CHUNKS_PER_GROUP, dil, ts)


def _rope_split(name, qkv, cos2, sin2):
    s, ts = qkv.shape[0], SPLIT_TS

    def body(qkv_ref, c_ref, s_ref, *rest):
        outs, buf = rest[:9], rest[9]
        c2, s2 = c_ref[...], s_ref[...]
        per = QKV_W // LANES
        _put(buf, 0, _rotate(qkv_ref[:, pl.ds(0, QKV_W)].astype(F32), c2, s2, 1.0))
        _put(buf, per, _rotate(qkv_ref[:, pl.ds(QKV_W, QKV_W)].astype(F32), c2, s2, 1.0))
        _put(buf, 2 * per, qkv_ref[:, pl.ds(2 * QKV_W, QKV_W)].astype(F32))
        for t in range(3):
            _deinterleave(buf, t * per, outs[3 * t:3 * t + 3], ts)

    tab = pl.BlockSpec((ts, LANES), lambda i: (i, 0))
    outs = pl.pallas_call(
        body, name=name, grid=(s // ts,), in_specs=[pl.BlockSpec((ts, 3 * QKV_W), lambda i: (i, 0)), tab, tab],
        out_specs=[_dilated_spec(d, ts) for _ in range(3) for d in ATTN_DILATIONS],
        out_shape=[_dilated_shape(s, d, BF16) for _ in range(3) for d in ATTN_DILATIONS],
        scratch_shapes=[pltpu.VMEM((3 * QKV_W // LANES, ts, LANES), F32)], compiler_params=_params(1),
    )(qkv, cos2, sin2)
    return outs[0:3], outs[3:6], outs[6:9]


def _rope_join(name, dq, dk, dv, cos2, sin2):
    s, ts = cos2.shape[0], SPLIT_TS

    def body(*refs):
        ins, c_ref, s_ref, o_ref, buf = refs[:9], refs[9], refs[10], refs[11], refs[12]
        per = QKV_W // LANES
        for t in range(3):
            _interleave(ins[3 * t:3 * t + 3], buf, t * per, ts)
        c2, s2 = c_ref[...], s_ref[...]
        o_ref[:, pl.ds(0, QKV_W)] = _rotate(_get(buf, 0, per), c2, s2, -1.0).astype(o_ref.dtype)
        o_ref[:, pl.ds(QKV_W, QKV_W)] = _rotate(_get(buf, per, per), c2, s2, -1.0).astype(o_ref.dtype)
        o_ref[:, pl.ds(2 * QKV_W, QKV_W)] = _get(buf, 2 * per, per).astype(o_ref.dtype)

    tab = pl.BlockSpec((ts, LANES), lambda i: (i, 0))
    return pl.pallas_call(
        body, name=name, grid=(s // ts,),
        in_specs=[_dilated_spec(d, ts) for _ in range(3) for d in ATTN_DILATIONS] + [tab, tab],
        out_specs=pl.BlockSpec((ts, 3 * QKV_W), lambda i: (i, 0)), out_shape=jax.ShapeDtypeStruct((s, 3 * QKV_W), BF16),
        scratch_shapes=[pltpu.VMEM((3 * QKV_W // LANES, ts, LANES), F32)], compiler_params=_params(1),
    )(*dq, *dk, *dv, cos2, sin2)


def _head_masks(shape):
    lane = lax.broadcasted_iota(jnp.int32, shape, 1)
    return [jnp.logical_and(lane >= h * HEAD_DIM, lane < (h + 1) * HEAD_DIM) for h in range(GROUP_W // HEAD_DIM)]


def _grad_split(name, do, o, lse):
    s, ts = do.shape[0], SPLIT_TS

    def body(do_ref, o_ref, l_ref, *rest):
        outs, buf = rest[:9], rest[9]
        dov = do_ref[...].astype(F32)
        prod = dov * o_ref[...].astype(F32)
        delta = jnp.zeros_like(prod)
        for hm in _head_masks(prod.shape):
            delta = jnp.where(hm, jnp.sum(jnp.where(hm, prod, 0.0), axis=1, keepdims=True), delta)
        _put(buf, 0, dov)
        _put(buf, CHUNKS_PER_GROUP, delta)
        _put(buf, 2 * CHUNKS_PER_GROUP, l_ref[...])
        for t in range(3):
            for g in range(3):
                _deinterleave_one(buf, t * CHUNKS_PER_GROUP, outs[3 * t + g], ATTN_DILATIONS[g], ts)

    nat = pl.BlockSpec((ts, GROUP_W), lambda i: (i, 0))
    dts = [BF16, F32, F32]
    outs = pl.pallas_call(
        body, name=name, grid=(s // ts,), in_specs=[nat, nat, nat],
        out_specs=[_dilated_spec(d, ts) for _ in range(3) for d in ATTN_DILATIONS],
        out_shape=[_dilated_shape(s, d, dt) for dt in dts for d in ATTN_DILATIONS],
        scratch_shapes=[pltpu.VMEM((3 * CHUNKS_PER_GROUP, ts, LANES), F32)], compiler_params=_params(1),
    )(do, o, lse)
    return outs[0:3], outs[3:6], outs[6:9]


def _band_masks(has_prev):
    qi = lax.broadcasted_iota(jnp.int32, (BLK, BLK), 0)
    kj = lax.broadcasted_iota(jnp.int32, (BLK, BLK), 1)
    return kj <= qi, jnp.logical_and(kj >= qi, has_prev)


def _dot_nt(a, b):
    return lax.dot_general(a, b, (((1,), (1,)), ((), ())), preferred_element_type=F32)


def _dot_tn(a, b):
    return lax.dot_general(a, b, (((0,), (0,)), ((), ())), preferred_element_type=F32)


def _dot(a, b):
    return jnp.dot(a, b, preferred_element_type=F32)


ATTN_BLK = (None, BLK, GROUP_W)


def _attn_specs(clamp):
    cur = pl.BlockSpec(ATTN_BLK, lambda r, n: (r, clamp(n), 0))
    prev = pl.BlockSpec(ATTN_BLK, lambda r, n: (r, jnp.maximum(clamp(n) - 1, 0), 0))
    return [cur, cur, prev, cur, prev]


def _attn_fwd(name, q, k, v):
    dil, rows, _ = q.shape
    nb = rows // BLK
    scale = HEAD_DIM ** -0.5

    def body(q_ref, kc_ref, kp_ref, vc_ref, vp_ref, o_ref, l_ref):
        mask_c, mask_p = _band_masks(pl.program_id(1) > 0)
        q, kc, kp, vc, vp = q_ref[...], kc_ref[...], kp_ref[...], vc_ref[...], vp_ref[...]
        o_acc = jnp.zeros((BLK, GROUP_W), F32)
        l_acc = jnp.zeros((BLK, GROUP_W), F32)
        for hm in _head_masks((BLK, GROUP_W)):
            qm = jnp.where(hm, q, jnp.zeros_like(q))
            sc = jnp.where(mask_c, _dot_nt(qm, kc) * scale, NEG_INF)
            sp = jnp.where(mask_p, _dot_nt(qm, kp) * scale, NEG_INF)
            m = jnp.maximum(jnp.max(sc, axis=1, keepdims=True), jnp.max(sp, axis=1, keepdims=True))
            pc, pp = jnp.exp(sc - m), jnp.exp(sp - m)
            den = jnp.sum(pc, axis=1, keepdims=True) + jnp.sum(pp, axis=1, keepdims=True)
            oh = (_dot(pc.astype(BF16), vc) + _dot(pp.astype(BF16), vp)) / den
            o_acc = jnp.where(hm, oh, o_acc)
            l_acc = jnp.where(hm, m + jnp.log(den), l_acc)
        o_ref[...] = o_acc
        l_ref[...] = l_acc

    out_spec = pl.BlockSpec(ATTN_BLK, lambda r, n: (r, n, 0))
    return pl.pallas_call(
        body, name=name, grid=(dil, nb), in_specs=_attn_specs(lambda n: n), out_specs=[out_spec, out_spec],
        out_shape=[jax.ShapeDtypeStruct(q.shape, F32)] * 2, compiler_params=_params(2),
    )(q, k, k, v, v)


def _attn_merge(name, os_, ls_):
    s, ts = os_[0].shape[0] * os_[0].shape[1], SPLIT_TS

    def body(*refs):
        o_refs, l_refs, o_ref, l_ref, buf = refs[0:3], refs[3:6], refs[6], refs[7], refs[8]
        _interleave(o_refs, buf, 0, ts)
        _interleave(l_refs, buf, 3 * CHUNKS_PER_GROUP, ts)
        o0, o1, o2 = [_get(buf, g * CHUNKS_PER_GROUP, CHUNKS_PER_GROUP) for g in range(3)]
        l0, l1, l2 = [_get(buf, (3 + g) * CHUNKS_PER_GROUP, CHUNKS_PER_GROUP) for g in range(3)]
        m = jnp.maximum(jnp.maximum(l0, l1), l2)
        e0, e1, e2 = jnp.exp(l0 - m), jnp.exp(l1 - m), jnp.exp(l2 - m)
        tot = e0 + e1 + e2
        o_ref[...] = ((e0 * o0 + e1 * o1 + e2 * o2) / tot).astype(o_ref.dtype)
        l_ref[...] = m + jnp.log(tot)

    nat = pl.BlockSpec((ts, GROUP_W), lambda i: (i, 0))
    return pl.pallas_call(
        body, name=name, grid=(s // ts,), in_specs=[_dilated_spec(d, ts) for _ in range(2) for d in ATTN_DILATIONS],
        out_specs=[nat, nat], out_shape=[jax.ShapeDtypeStruct((s, GROUP_W), BF16), jax.ShapeDtypeStruct((s, GROUP_W), F32)],
        scratch_shapes=[pltpu.VMEM((6 * CHUNKS_PER_GROUP, ts, LANES), F32)], compiler_params=_params(1),
    )(*os_, *ls_)


def _attn_bwd(name, q, k, v, do, delta, lse):
    dil, rows, _ = q.shape
    nb = rows // BLK
    scale = HEAD_DIM ** -0.5

    def body(q_ref, kc_ref, kp_ref, vc_ref, vp_ref, do_ref, dl_ref, l_ref, dq_ref, dk_ref, dv_ref, ck_ref, cv_ref):
        n = pl.program_id(1)

        @pl.when(n == 0)
        def _():
            ck_ref[...] = jnp.zeros_like(ck_ref)
            cv_ref[...] = jnp.zeros_like(cv_ref)

        @pl.when(n < nb)
        def _():
            mask_c, mask_p = _band_masks(n > 0)
            q, kc, kp, vc, vp, dov = q_ref[...], kc_ref[...], kp_ref[...], vc_ref[...], vp_ref[...], do_ref[...]
            lb, db = l_ref[...], dl_ref[...]
            zero = jnp.zeros((BLK, GROUP_W), F32)
            dq_acc, dkc, dkp, dvc, dvp = zero, zero, zero, zero, zero
            for hm in _head_masks((BLK, GROUP_W)):
                qm = jnp.where(hm, q, jnp.zeros_like(q))
                dom = jnp.where(hm, dov, jnp.zeros_like(dov))
                lh = jnp.max(jnp.where(hm, lb, NEG_INF), axis=1, keepdims=True)
                delta = jnp.max(jnp.where(hm, db, NEG_INF), axis=1, keepdims=True)
                pc = jnp.exp(jnp.where(mask_c, _dot_nt(qm, kc) * scale, NEG_INF) - lh)
                pp = jnp.exp(jnp.where(mask_p, _dot_nt(qm, kp) * scale, NEG_INF) - lh)
                dsc = (pc * (_dot_nt(dom, vc) - delta) * scale).astype(BF16)
                dsp = (pp * (_dot_nt(dom, vp) - delta) * scale).astype(BF16)
                dq_acc = jnp.where(hm, _dot(dsc, kc) + _dot(dsp, kp), dq_acc)
                dkc += _dot_tn(dsc, qm)
                dkp += _dot_tn(dsp, qm)
                dvc += _dot_tn(pc.astype(BF16), dom)
                dvp += _dot_tn(pp.astype(BF16), dom)
            dq_ref[...] = dq_acc.astype(dq_ref.dtype)
            dk_ref[...] = (ck_ref[...] + dkp).astype(dk_ref.dtype)
            dv_ref[...] = (cv_ref[...] + dvp).astype(dv_ref.dtype)
            ck_ref[...] = dkc
            cv_ref[...] = dvc

        @pl.when(n == nb)
        def _():
            dk_ref[...] = ck_ref[...].astype(dk_ref.dtype)
            dv_ref[...] = cv_ref[...].astype(dv_ref.dtype)

    clamp = lambda n: jnp.minimum(n, nb - 1)
    qspec = pl.BlockSpec(ATTN_BLK, lambda r, n: (r, clamp(n), 0))
    kspec = pl.BlockSpec(ATTN_BLK, lambda r, n: (r, jnp.maximum(n - 1, 0), 0))
    return pl.pallas_call(
        body, name=name, grid=(dil, nb + 1), in_specs=_attn_specs(clamp) + [qspec, qspec, qspec],
        out_specs=[qspec, kspec, kspec], out_shape=[jax.ShapeDtypeStruct(q.shape, BF16)] * 3,
        scratch_shapes=[pltpu.VMEM((BLK, GROUP_W), F32), pltpu.VMEM((BLK, GROUP_W), F32)], compiler_params=_params(2),
    )(q, k, k, v, v, do, delta, lse)


CONV_TS = 128
HALO = 32


def _conv_fwd(name, u, w, b, lg, lb):
    s = u.shape[0]
    ts, per = CONV_TS, CONV_TS // HALO

    def body(a_ref, g_ref, ap_ref, gp_ref, w_ref, b_ref, lg_ref, lb_ref, c_ref, act_ref, buf, cbuf):
        i = pl.program_id(0)
        buf[pl.ds(HALO, ts), :] = a_ref[...].astype(F32) * _sigmoid(g_ref[...].astype(F32))
        prev = ap_ref[...].astype(F32) * _sigmoid(gp_ref[...].astype(F32))
        buf[pl.ds(0, HALO), :] = jnp.where(i > 0, prev, 0.0)
        for lc in range(D // LANES):
            cols = pl.ds(lc * LANES, LANES)
            acc = jnp.broadcast_to(b_ref[:, cols], (ts, LANES))
            for j in range(CONV_K):
                acc = acc + w_ref[pl.ds(j, 1), cols] * buf[pl.ds(HALO - (CONV_K - 1) + j, ts), cols]
            cbuf[:, cols] = acc
        c = cbuf[...]
        mu = jnp.mean(c, axis=-1, keepdims=True)
        xc = c - mu
        ln = xc * lax.rsqrt(jnp.mean(xc * xc, axis=-1, keepdims=True) + EPS) * lg_ref[...] + lb_ref[...]
        c_ref[...] = c.astype(c_ref.dtype)
        act_ref[...] = (ln * _sigmoid(ln)).astype(act_ref.dtype)

    halo = lambda cb: pl.BlockSpec((HALO, D), functools.partial(lambda i, cb: (jnp.maximum(i * per - 1, 0), cb), cb=cb))
    vec = pl.BlockSpec((1, D), lambda i: (0, 0))
    return pl.pallas_call(
        body, name=name, grid=(s // ts,),
        in_specs=[pl.BlockSpec((ts, D), lambda i: (i, 0)), pl.BlockSpec((ts, D), lambda i: (i, 1)), halo(0), halo(1),
                  pl.BlockSpec((HALO, D), lambda i: (0, 0)), vec, vec, vec],
        out_specs=[pl.BlockSpec((ts, D), lambda i: (i, 0))] * 2,
        out_shape=[jax.ShapeDtypeStruct((s, D), BF16)] * 2,
        scratch_shapes=[pltpu.VMEM((ts + HALO, D), F32), pltpu.VMEM((ts, D), F32)],
        compiler_params=_params(1),
    )(u, u, u, u, w, b, lg, lb)


def _ln_bwd(name, dact, c, lg, lb):
    def body(rows, vecs):
        (dv, cv), (g, b) = rows, vecs
        cv = cv.astype(F32)
        mu = jnp.mean(cv, axis=-1, keepdims=True)
        xc = cv - mu
        rstd = lax.rsqrt(jnp.mean(xc * xc, axis=-1, keepdims=True) + EPS)
        xh = xc * rstd
        ln = xh * g + b
        sg = _sigmoid(ln)
        dln = dv * (sg * (1.0 + ln * (1.0 - sg)))
        dxh = dln * g
        dc = rstd * (dxh - jnp.mean(dxh, axis=-1, keepdims=True) - xh * jnp.mean(dxh * xh, axis=-1, keepdims=True))
        return [dc], [_colsum(dln * xh), _colsum(dln)]

    return _rowwise(name, body, [dact, c], [lg, lb], [(D, F32)], [(1, D), (1, D)])


def _conv_bwd(name, dc, u, w):
    s = u.shape[0]
    ts, per = CONV_TS, CONV_TS // HALO
    n_t = s // ts

    def body(dc_ref, dn_ref, a_ref, g_ref, ap_ref, gp_ref, w_ref, du_ref, dw_ref, db_ref, buf, dbuf, hbuf):
        i = pl.program_id(0)
        a = a_ref[...].astype(F32)
        sg = _sigmoid(g_ref[...].astype(F32))
        buf[pl.ds(HALO, ts), :] = a * sg
        prev = ap_ref[...].astype(F32) * _sigmoid(gp_ref[...].astype(F32))
        buf[pl.ds(0, HALO), :] = jnp.where(i > 0, prev, 0.0)
        dcv = dc_ref[...]
        dbuf[pl.ds(0, ts), :] = dcv
        dbuf[pl.ds(ts, HALO), :] = jnp.where(i < n_t - 1, dn_ref[...], 0.0)

        @pl.when(i == 0)
        def _():
            dw_ref[...] = jnp.zeros_like(dw_ref)
            db_ref[...] = jnp.zeros_like(db_ref)

        db_ref[...] += _colsum(dcv)
        for lc in range(D // LANES):
            cols = pl.ds(lc * LANES, LANES)
            d0 = dbuf[pl.ds(0, ts), cols]
            acc = jnp.zeros((ts, LANES), F32)
            for j in range(CONV_K):
                acc = acc + w_ref[pl.ds(j, 1), cols] * dbuf[pl.ds(CONV_K - 1 - j, ts), cols]
                part = d0 * buf[pl.ds(HALO - (CONV_K - 1) + j, ts), cols]
                dw_ref[pl.ds(8 * j, 8), cols] += jnp.sum(part.reshape(ts // 8, 8, LANES), axis=0)
            hbuf[:, cols] = acc
        dh = hbuf[...]
        du_ref[:, pl.ds(0, D)] = (dh * sg).astype(du_ref.dtype)
        du_ref[:, pl.ds(D, D)] = (dh * a * sg * (1.0 - sg)).astype(du_ref.dtype)

    halo = lambda cb: pl.BlockSpec((HALO, D), functools.partial(lambda i, cb: (jnp.maximum(i * per - 1, 0), cb), cb=cb))
    nxt = pl.BlockSpec((HALO, D), lambda i: (jnp.minimum((i + 1) * per, s // HALO - 1), 0))
    return pl.pallas_call(
        body, name=name, grid=(n_t,),
        in_specs=[pl.BlockSpec((ts, D), lambda i: (i, 0)), nxt, pl.BlockSpec((ts, D), lambda i: (i, 0)),
                  pl.BlockSpec((ts, D), lambda i: (i, 1)), halo(0), halo(1), pl.BlockSpec((HALO, D), lambda i: (0, 0))],
        out_specs=[pl.BlockSpec((ts, 2 * D), lambda i: (i, 0)), pl.BlockSpec((8 * CONV_K, D), lambda i: (0, 0)),
                   pl.BlockSpec((1, D), lambda i: (0, 0))],
        out_shape=[jax.ShapeDtypeStruct((s, 2 * D), BF16), jax.ShapeDtypeStruct((8 * CONV_K, D), F32),
                   jax.ShapeDtypeStruct((1, D), F32)],
        scratch_shapes=[pltpu.VMEM((ts + HALO, D), F32), pltpu.VMEM((ts + HALO, D), F32), pltpu.VMEM((ts, D), F32)],
        compiler_params=_params(1),
    )(dc, dc, u, u, u, u, w)


def _mix_fwd(tag, x1, ng, sc, sh, gate, wts, cos2, sin2):
    h1 = _modulate(f"{tag}_mod", x1, ng, sc, sh)
    ident = lambda accs, _t, _v: accs
    qkv = _mm(f"{tag}_qkv", [(h1, wts["w_qkv"])], ident, [BF16], tn=QKV_W)[0]
    u = _mm(f"{tag}_u", [(h1, wts["w_u"])], ident, [BF16], tn=D)[0]
    ga, gc = _mm(f"{tag}_gates", [(h1, wts["w_ga"]), (h1, wts["w_gc"])], ident, [BF16] * 2)
    qd, kd, vd = _rope_split(f"{tag}_rope", qkv, cos2, sin2)
    per_group = [_attn_fwd(f"{tag}_attn{g}", qd[g], kd[g], vd[g]) for g in range(len(ATTN_DILATIONS))]
    o, lse = _attn_merge(f"{tag}_merge", [p[0] for p in per_group], [p[1] for p in per_group])
    cpre, act = _conv_fwd(f"{tag}_conv", u, wts["conv_w"], wts["conv_b"], wts["ln_g"], wts["ln_b"])

    def gate_epi(accs, tl, _v):
        ya, yc = accs
        return [_sigmoid(tl[0].astype(F32)) * ya + _sigmoid(tl[1].astype(F32)) * yc, ya, yc]

    y, ya, yc = _mm(f"{tag}_branch", [(o, wts["attn_wo"]), (act, wts["conv_wo"])], gate_epi, [BF16] * 3,
                    tiles=[(ga, 0), (gc, 0)])

    def res_epi(accs, tl, vs):
        return [tl[0] + vs[0] * accs[0], accs[0]]

    x2, f1 = _mm(f"{tag}_out", [(y, wts["w_out"])], res_epi, [F32, BF16], tiles=[(x1, 0)], vecs=[gate])
    return x2, (x1, h1, u, ga, gc, qd, kd, vd, o, lse, cpre, act, y, ya, yc, f1)


def _mix_bwd(tag, dx2, saved, ng, sc, gate, wts, cos2, sin2):
    x1, h1, u, ga, gc, qd, kd, vd, o, lse, cpre, act, y, ya, yc, f1 = saved
    dfm, dgate = _gate_scale(f"{tag}_gs", dx2, f1, gate, 1.0)

    def epi(accs, tl, _v):
        dy = accs[0]
        sa, sc_ = _sigmoid(tl[0].astype(F32)), _sigmoid(tl[1].astype(F32))
        return [dy * sa, dy * sc_, dy * tl[2].astype(F32) * sa * (1.0 - sa), dy * tl[3].astype(F32) * sc_ * (1.0 - sc_)]

    dya, dyc, dga, dgc = _mm(f"{tag}_bout", [(dfm, wts["w_out"])], epi, [BF16] * 4, nt=True,
                             tiles=[(ga, 0), (gc, 0), (ya, 0), (yc, 0)])
    grads = {"w_out": _mm_tn(f"{tag}_dwout", y, dfm), "attn_wo": _mm_tn(f"{tag}_dwattn", o, dya),
             "conv_wo": _mm_tn(f"{tag}_dwconv", act, dyc)}
    ident = lambda accs, _t, _v: accs
    do = _mm(f"{tag}_battn", [(dya, wts["attn_wo"])], ident, [BF16], nt=True)[0]
    dact = _mm(f"{tag}_bconv", [(dyc, wts["conv_wo"])], ident, [F32], nt=True)[0]
    dc, dlg, dlb = _ln_bwd(f"{tag}_lnb", dact, cpre, wts["ln_g"], wts["ln_b"])
    du, dw8, dcb = _conv_bwd(f"{tag}_convb", dc, u, wts["conv_w"])
    dod, deltad, lsed = _grad_split(f"{tag}_gsplit", do, o, lse)
    dqs, dks, dvs = [], [], []
    for g in range(len(ATTN_DILATIONS)):
        dq, dk, dv = _attn_bwd(f"{tag}_attnb{g}", qd[g], kd[g], vd[g], dod[g], deltad[g], lsed[g])
        dqs.append(dq); dks.append(dk); dvs.append(dv)
    dqkv = _rope_join(f"{tag}_ropeb", dqs, dks, dvs, cos2, sin2)
    grads["w_in"] = jnp.concatenate([_mm_tn(f"{tag}_dwqkv", h1, dqkv), _mm_tn(f"{tag}_dwu", h1, du),
                                     _mm_tn(f"{tag}_dwga", h1, dga), _mm_tn(f"{tag}_dwgc", h1, dgc)], axis=1)
    dh1 = _mm(f"{tag}_bin", [(dqkv, wts["w_qkv"]), (du, wts["w_u"]), (dga, wts["w_ga"]), (dgc, wts["w_gc"])],
              lambda accs, _t, _v: [accs[0] + accs[1] + accs[2] + accs[3]], [F32], nt=True)[0]
    dx1, dsh, dsc, dng = _modulate_bwd(f"{tag}_modb", dh1, x1, dx2, ng, sc)
    small = {"conv_w": dw8.reshape(CONV_K, 8, D).sum(axis=1), "conv_b": dcb, "ln_g": dlg, "ln_b": dlb}
    return dx1, (dsh, dsc, dgate, dng), grads, small


def _loss_head(name, x, target, fg):
    def body(rows, vecs):
        (xv, tv), (g,) = rows, vecs
        xhat, r = _rms_parts(xv)
        err = xhat * g - tv
        dy = err * (1.0 / D)
        return [_rms_bwd(dy * g, xhat, r)], [_colsum(err * err), _colsum(dy * xhat)]

    return _rowwise(name, body, [x, target], [fg], [(D, F32)], [(1, D), (1, D)])


def _local_step(x, target, mod, norm_g, wts, final_g):
    s = x.shape[0]
    cos2, sin2 = _rope_tables(s)
    row = lambda a: a[None, :]
    saved = []
    for l in range(2):
        m = lambda i: row(mod[l, i])
        w = wts[l]
        x, s0 = _ffn_fwd(f"l{l}f0", x, row(norm_g[l, 0]), m(1), m(0), m(2), w["wg0"], w["wu0"], w["wd0"])
        x, s1 = _mix_fwd(f"l{l}mx", x, row(norm_g[l, 1]), m(4), m(3), m(5), w, cos2, sin2)
        x, s2 = _ffn_fwd(f"l{l}f1", x, row(norm_g[l, 2]), m(7), m(6), m(8), w["wg1"], w["wu1"], w["wd1"])
        saved.append((s0, s1, s2))
    dx, sq, dfg = _loss_head("loss_head", x, target, row(final_g))
    loss = (0.5 / D) * jnp.sum(sq)
    dmod, dng, big, small = [None, None], [None, None], [None, None], [None, None]
    for l in (1, 0):
        m = lambda i: row(mod[l, i])
        w = wts[l]
        s0, s1, s2 = saved[l]
        dx, v2, (dwg1, dwu1, dwd1) = _ffn_bwd(f"l{l}f1", dx, s2, row(norm_g[l, 2]), m(7), m(8), w["wg1"], w["wu1"], w["wd1"])
        dx, v1, gmix, small[l] = _mix_bwd(f"l{l}mx", dx, s1, row(norm_g[l, 1]), m(4), m(5), w, cos2, sin2)
        dx, v0, (dwg0, dwu0, dwd0) = _ffn_bwd(f"l{l}f0", dx, s0, row(norm_g[l, 0]), m(1), m(2), w["wg0"], w["wu0"], w["wd0"])
        dmod[l] = jnp.concatenate([jnp.concatenate(v[:3], axis=0) for v in (v0, v1, v2)], axis=0)
        dng[l] = jnp.concatenate([v0[3], v1[3], v2[3]], axis=0)
        big[l] = dict(gmix, ffn_wg=jnp.stack([dwg0, dwg1]), ffn_wu=jnp.stack([dwu0, dwu1]), ffn_wd=jnp.stack([dwd0, dwd1]))
    big = {k: jnp.stack([big[0][k], big[1][k]]) for k in big[0]}
    small = {k: jnp.stack([small[0][k].reshape(-1, D), small[1][k].reshape(-1, D)]) for k in small[0]}
    small = dict(small, norm_g=jnp.stack(dng), final_g=dfg.reshape(D))
    return loss, dx, jnp.stack(dmod), small, big


HBM_SPEC = pl.BlockSpec(memory_space=pl.ANY)


def _place():
    return lax.axis_index("x"), lax.axis_index("y"), lax.axis_index("c")


N_PEERS = N_DEV - 1


def _all_gather(name, arrays):
    n_a = len(arrays)

    def body(*refs):
        x_refs, out_refs = refs[:n_a], refs[n_a:2 * n_a]
        send_sems, recv_sems, local_sems = refs[2 * n_a:]
        x, y, c = _place()
        me, sibling = (x, y, c), (x, y, 1 - c)
        chips = [(1 - x, y), (x, 1 - y), (1 - x, 1 - y)]

        def copy(a, k, block, to, from_input=False):
            px, py, pc = block
            rows = out_refs[a].at[4 * px + 2 * py + pc]
            return pltpu.make_async_remote_copy(
                src_ref=x_refs[a] if from_input else rows, dst_ref=rows, send_sem=send_sems.at[a * N_PEERS + k],
                recv_sem=recv_sems.at[a * N_PEERS + k], device_id=to, device_id_type=pl.DeviceIdType.MESH)

        mine = [pltpu.make_async_copy(x_refs[a], out_refs[a].at[4 * x + 2 * y + c], local_sems.at[a]) for a in range(n_a)]
        for cp in mine:
            cp.start()
        first = []
        for j, chip in enumerate(chips):
            first += [copy(a, 1 + j, me, (*chip, c), from_input=True) for a in range(n_a)]
        first += [copy(a, 0, me, sibling, from_input=True) for a in range(n_a)]
        for cp in first:
            cp.start()
        passed = []
        for j, chip in enumerate(chips):
            for a in range(n_a):
                copy(a, 1 + j, (*chip, c), me).wait_recv()
                passed.append(copy(a, 4 + j, (*chip, c), sibling))
                passed[-1].start()
        for a in range(n_a):
            copy(a, 0, sibling, me).wait_recv()
        for j, chip in enumerate(chips):
            for a in range(n_a):
                copy(a, 4 + j, (*chip, 1 - c), me).wait_recv()
        for cp in first + passed:
            cp.wait_send()
        for cp in mine:
            cp.wait()

    return pl.pallas_call(
        body, name=name, out_shape=[jax.ShapeDtypeStruct((N_DEV, *a.shape), a.dtype) for a in arrays],
        in_specs=[HBM_SPEC] * n_a, out_specs=[HBM_SPEC] * n_a,
        scratch_shapes=[pltpu.SemaphoreType.DMA((n_a * N_PEERS,)), pltpu.SemaphoreType.DMA((n_a * N_PEERS,)),
                        pltpu.SemaphoreType.DMA((n_a,))],
    )(*arrays)


def _exchange_blocks(name, arrays):
    n_a = len(arrays)

    def body(*refs):
        g_refs, out_refs = refs[:n_a], refs[n_a:2 * n_a]
        send_sems, recv_sems, local_sems = refs[2 * n_a:]
        x, y, c = _place()
        my = 4 * x + 2 * y + c
        local = [pltpu.make_async_copy(g_refs[a].at[my], out_refs[a].at[my], local_sems.at[a]) for a in range(n_a)]
        for cp in local:
            cp.start()
        copies = []
        for k in (4, 2, 6, 1, 5, 3, 7):
            px = 1 - x if k & 4 else x
            py = 1 - y if k & 2 else y
            pc = 1 - c if k & 1 else c
            for a in range(n_a):
                copies.append(pltpu.make_async_remote_copy(
                    src_ref=g_refs[a].at[4 * px + 2 * py + pc], dst_ref=out_refs[a].at[my],
                    send_sem=send_sems.at[a * N_PEERS + k - 1], recv_sem=recv_sems.at[a * N_PEERS + k - 1],
                    device_id=(px, py, pc), device_id_type=pl.DeviceIdType.MESH))
        for cp in copies:
            cp.start()
        for cp in copies:
            cp.wait()
        for cp in local:
            cp.wait()

    return pl.pallas_call(
        body, name=name, out_shape=[jax.ShapeDtypeStruct(a.shape, a.dtype) for a in arrays],
        in_specs=[HBM_SPEC] * n_a, out_specs=[HBM_SPEC] * n_a,
        scratch_shapes=[pltpu.SemaphoreType.DMA((n_a * N_PEERS,)), pltpu.SemaphoreType.DMA((n_a * N_PEERS,)),
                        pltpu.SemaphoreType.DMA((n_a,))],
    )(*arrays)


SLAB_TS = 2048


def _sum_parts(name, parts):
    def body(rows, _v):
        tot = rows[0].astype(F32)
        for r in rows[1:]:
            tot = tot + r.astype(F32)
        return [tot], []

    return _rowwise(name, body, list(parts), [], [(parts[0].shape[1], F32)], ts=SLAB_TS)[0]


def _adamw(name, w, parts, m, v):
    def body(rows, _v):
        wv, mv, vv = rows[0], rows[1], rows[2]
        g = rows[3].astype(F32)
        for r in rows[4:]:
            g = g + r.astype(F32)
        m2 = ADAM_B1 * mv + (1.0 - ADAM_B1) * g
        v2 = ADAM_B2 * vv + (1.0 - ADAM_B2) * (g * g)
        m_hat = m2 / (1.0 - ADAM_B1 ** ADAM_STEP)
        v_hat = v2 / (1.0 - ADAM_B2 ** ADAM_STEP)
        delta = -ADAM_LR * (m_hat / (jnp.sqrt(v_hat) + ADAM_EPS) + ADAM_WD * wv)
        return [g, delta, m2, v2], []

    width = w.shape[1]
    return _rowwise(name, body, [w, m, v, *parts], [], [(width, F32)] * 4, ts=max(16, SLAB_TS * LANES // width))


def _to_slab(arrays, dtype):
    flat = jnp.concatenate([a.reshape(-1).astype(dtype) for a in arrays])
    rows = -(-flat.shape[0] // LANES)
    rows = -(-rows // 8) * 8 if rows <= SLAB_TS else -(-rows // SLAB_TS) * SLAB_TS
    return jnp.pad(flat, (0, rows * LANES - flat.shape[0])).reshape(rows, LANES)


def _from_slab(slab, shapes, lead=()):
    flat = slab.reshape(*lead, -1)
    out, at = [], 0
    for shp in shapes:
        size = 1
        for d in shp:
            size *= d
        out.append(flat[..., at:at + size].reshape(*lead, *shp))
        at += size
    return out


BIG = {"ffn_wg": 3, "ffn_wu": 3, "ffn_wd": 2, "w_in": 2, "attn_wo": 2, "conv_wo": 1, "w_out": 1}


def _join(blocks, axis):
    full = jnp.moveaxis(blocks, 0, axis)
    return full.reshape(*full.shape[:axis], -1, *full.shape[axis + 2:])


def _split(full, axis):
    shp = full.shape
    return jnp.moveaxis(full.reshape(*shp[:axis], N_DEV, shp[axis] // N_DEV, *shp[axis + 1:]), axis, 0)


def kernel(x, c, ada_w, ada_b, norm_g, ffn_wg, ffn_wu, ffn_wd, w_in, attn_wo, conv_w, conv_b, conv_ln_g, conv_ln_b, conv_wo, w_out, final_g, loss_target, m_ada_w, m_ada_b, m_norm_g, m_ffn_wg, m_ffn_wu, m_ffn_wd, m_w_in, m_attn_wo, m_conv_w, m_conv_b, m_conv_ln_g, m_conv_ln_b, m_conv_wo, m_w_out, m_final_g, v_ada_w, v_ada_b, v_norm_g, v_ffn_wg, v_ffn_wu, v_ffn_wd, v_w_in, v_attn_wo, v_conv_w, v_conv_b, v_conv_ln_g, v_conv_ln_b, v_conv_wo, v_w_out, v_final_g):
    px, py, pc = _place()
    me = 4 * px + 2 * py + pc
    n_mod = ada_w.shape[2]
    big_w = dict(ffn_wg=ffn_wg, ffn_wu=ffn_wu, ffn_wd=ffn_wd, w_in=w_in, attn_wo=attn_wo, conv_wo=conv_wo, w_out=w_out)
    big_m = dict(ffn_wg=m_ffn_wg, ffn_wu=m_ffn_wu, ffn_wd=m_ffn_wd, w_in=m_w_in, attn_wo=m_attn_wo, conv_wo=m_conv_wo, w_out=m_w_out)
    big_v = dict(ffn_wg=v_ffn_wg, ffn_wu=v_ffn_wu, ffn_wd=v_ffn_wd, w_in=v_w_in, attn_wo=v_attn_wo, conv_wo=v_conv_wo, w_out=v_w_out)
    big_shapes = [big_w[k].shape for k in BIG]

    small_in = [c, norm_g, conv_w]
    g1 = _all_gather("gather_small", [_to_slab(small_in, F32)])[0]
    c_all, ng_blocks, cw_blocks = _from_slab(g1, [a.shape for a in small_in], lead=(N_DEV,))
    c_all = c_all.reshape(N_DEV, D)
    norm_g_full = _join(ng_blocks, 2)
    conv_w_full = _join(cw_blocks, 2)
    as2d = lambda a: a.reshape(-1, a.shape[-1])
    gw = _all_gather("gather_weights", [as2d(big_w[k]).astype(BF16) for k in BIG])
    full = {k: _join(blk.reshape(N_DEV, *big_w[k].shape), BIG[k]) for k, blk in zip(BIG, gw)}

    c_act = _rowwise("cond_silu", lambda rows, _v: ([rows[0] * _sigmoid(rows[0])], []), [c_all], [], [(D, BF16)])[0]
    c_pad = jnp.pad(c_act, ((0, LANES - N_DEV), (0, 0)))
    mod_cols = []
    for l in range(2):
        bias = lax.dynamic_slice_in_dim(ada_b[l], me * n_mod, n_mod)[None, :]
        out = _mm(f"mod{l}", [(c_pad, ada_w[l].astype(BF16))], lambda accs, _t, vs: [accs[0] + vs[0]], [F32], vecs=[bias])[0]
        mod_cols.append(out[:N_DEV])
    g2 = _all_gather("gather_mod", [_to_slab([jnp.stack(mod_cols)], F32)])[0]
    mod_all = _from_slab(g2, [(2, N_DEV, n_mod)], lead=(N_DEV,))[0]
    mod = lax.dynamic_index_in_dim(mod_all, me, axis=2, keepdims=False)
    mod = jnp.moveaxis(mod, 0, 1).reshape(2, 9, D)

    wts = []
    for l in range(2):
        w_in_l = full["w_in"][l]
        cw = jnp.pad(conv_w_full[l], ((0, HALO - CONV_K), (0, 0)))
        wts.append(dict(
            wg0=full["ffn_wg"][l, 0], wu0=full["ffn_wu"][l, 0], wd0=full["ffn_wd"][l, 0],
            wg1=full["ffn_wg"][l, 1], wu1=full["ffn_wu"][l, 1], wd1=full["ffn_wd"][l, 1],
            w_qkv=w_in_l[:, :3 * QKV_W], w_u=w_in_l[:, 3 * QKV_W:3 * QKV_W + 2 * D],
            w_ga=w_in_l[:, 3 * QKV_W + 2 * D:3 * QKV_W + 3 * D], w_gc=w_in_l[:, 3 * QKV_W + 3 * D:],
            attn_wo=full["attn_wo"][l], conv_wo=full["conv_wo"][l], w_out=full["w_out"][l],
            conv_w=cw, conv_b=conv_b[l][None, :], ln_g=conv_ln_g[l][None, :], ln_b=conv_ln_b[l][None, :]))

    loss, dx, dmod, small, big = _local_step(x[0], loss_target[0], mod, norm_g_full, wts, final_g)
    loss = lax.psum(loss, MESH_AXES)

    small_names = ["norm_g", "conv_w", "conv_b", "ln_g", "ln_b", "final_g"]
    small_parts = [dmod] + [small[k] for k in small_names]
    g3 = _all_gather("gather_small_grads", [_to_slab(small_parts, F32)])[0]
    tot = _sum_parts("sum_small_grads", [g3[k] for k in range(N_DEV)])
    _, g_ng, g_cw, g_cb, g_lg, g_lb, g_fg = _from_slab(tot, [a.shape for a in small_parts])
    g_ab = _from_slab(tot, [(2, 9 * D)])[0]
    dmod_all = _from_slab(g3, [dmod.shape], lead=(N_DEV,))[0].reshape(N_DEV, 2, 9 * D)
    dm_mine = lax.dynamic_slice_in_dim(dmod_all, me * n_mod, n_mod, axis=2)
    g_aw = jnp.stack([
        _mm_tn(f"dada_w{l}", c_pad, jnp.pad(dm_mine[:, l], ((0, LANES - N_DEV), (0, 0))).astype(BF16), out_dtype=F32)
        for l in range(2)])
    cols = lambda a: lax.dynamic_slice_in_dim(a, me * (D // N_DEV), D // N_DEV, axis=2)
    small_w = [ada_b, norm_g, conv_w, conv_b, conv_ln_g, conv_ln_b, final_g]
    small_m = [m_ada_b, m_norm_g, m_conv_w, m_conv_b, m_conv_ln_g, m_conv_ln_b, m_final_g]
    small_v = [v_ada_b, v_norm_g, v_conv_w, v_conv_b, v_conv_ln_g, v_conv_ln_b, v_final_g]
    small_g = [g_ab, cols(g_ng), cols(g_cw), g_cb, g_lg, g_lb, g_fg]
    s_shapes = [a.shape for a in small_w]
    s_out = _adamw("adamw_small", _to_slab(small_w, F32), [_to_slab(small_g, F32)], _to_slab(small_m, F32), _to_slab(small_v, F32))
    aw_out = [o.reshape(ada_w.shape) for o in _adamw("adamw_ada_w", as2d(ada_w), [as2d(g_aw)], as2d(m_ada_w), as2d(v_ada_w))]

    send = [_split(big[k], BIG[k]).reshape(N_DEV, -1, big_w[k].shape[-1]) for k in BIG]
    got = _exchange_blocks("exchange_grads", send)
    b_out = {}
    for k, blocks in zip(BIG, got):
        parts = [(blocks, blocks.shape[2], 0, j) for j in range(N_DEV)]
        outs = _adamw(f"adamw_{k}", as2d(big_w[k]), parts, as2d(big_m[k]), as2d(big_v[k]))
        b_out[k] = [o.reshape(big_w[k].shape) for o in outs]

    def ordered(i):
        ab, ng, cw, cb, lg, lb, fg = _from_slab(s_out[i], s_shapes)
        bg = {k: b_out[k][i] for k in BIG}
        return [aw_out[i], ab, ng, bg["ffn_wg"], bg["ffn_wu"], bg["ffn_wd"], bg["w_in"], bg["attn_wo"], cw, cb, lg, lb,
                bg["conv_wo"], bg["w_out"], fg]

    return (loss, dx[None], *ordered(0), *ordered(1), *ordered(2), *ordered(3))
```

```python
import functools

import jax
import jax.numpy as jnp
from jax import lax
from jax.experimental import pallas as pl
from jax.experimental.pallas import tpu as pltpu

F32 = jnp.float32
BF16 = jnp.bfloat16

N_DEV = 8
D = 1024
D_FF = 2816
HEAD_DIM = 64
GROUP_W = 256
ATTN_DILATIONS = (1, 4, 16)
BLK = 128
QKV_W = 768
CONV_K = 31
ROPE_THETA = 10000.0
EPS = 1e-6
NEG_INF = -1e30
ADAM_LR, ADAM_B1, ADAM_B2, ADAM_EPS, ADAM_WD, ADAM_STEP = 0.001, 0.9, 0.999, 1e-08, 0.01, 10

V7X_VMEM_BYTES = 64 * 1024 * 1024
VMEM_LIMIT = V7X_VMEM_BYTES - 8 * 1024 * 1024
LANES = 128
MESH_AXES = ("x", "y", "c")


def _params(n_grid):
    return pltpu.CompilerParams(vmem_limit_bytes=VMEM_LIMIT, dimension_semantics=("arbitrary",) * n_grid)


def _sigmoid(v):
    return 1.0 / (1.0 + jnp.exp(-v))


def _mm(name, prods, epilogue, out_dtypes, *, nt=False, tiles=(), vecs=(), tm=512, tn=None, a_pre=None, chunk=None):
    s = prods[0][0].shape[0]
    n = prods[0][1].shape[0] if nt else prods[0][1].shape[1]
    tn = n if tn is None else tn
    tm = min(tm, s)
    assert s % tm == 0 and n % tn == 0
    n_p, n_t, n_v = len(prods), len(tiles), len(vecs)
    dn = (((1,), (1,)), ((), ())) if nt else (((1,), (0,)), ((), ()))

    chunk = tn if chunk is None else chunk
    bounds = [(c0, min(chunk, tn - c0)) for c0 in range(0, tn, chunk)]

    def body(*refs):
        p_refs, rest = refs[:2 * n_p], refs[2 * n_p:]
        t_refs, v_refs, o_refs = rest[:n_t], rest[n_t:n_t + n_v], rest[n_t + n_v:]
        lhs = []
        for p in range(n_p):
            a = p_refs[2 * p][...]
            lhs.append(a if a_pre is None else a_pre(a))
        for c0, cw in bounds:
            cols = pl.ds(c0, cw)
            accs = []
            for p in range(n_p):
                b = p_refs[2 * p + 1][cols, :] if nt else p_refs[2 * p + 1][:, cols]
                accs.append(lax.dot_general(lhs[p], b, dn, preferred_element_type=F32))
            outs = epilogue(accs, [t[:, cols] for t in t_refs], [v[:, cols] for v in v_refs])
            for o_ref, o in zip(o_refs, outs, strict=True):
                o_ref[:, cols] = o.astype(o_ref.dtype)

    in_specs = []
    operands = []
    for a, b in prods:
        k = a.shape[1]
        in_specs.append(pl.BlockSpec((tm, k), lambda j, i: (i, 0)))
        in_specs.append(pl.BlockSpec((tn, k), lambda j, i: (j, 0)) if nt else pl.BlockSpec((k, tn), lambda j, i: (0, j)))
        operands += [a, b]
    for arr, off in tiles:
        in_specs.append(pl.BlockSpec((tm, tn), functools.partial(lambda j, i, off: (i, j + off), off=off)))
        operands.append(arr)
    for v in vecs:
        in_specs.append(pl.BlockSpec((1, tn), lambda j, i: (0, j)))
        operands.append(v)
    out = pl.pallas_call(
        body, name=name, grid=(n // tn, s // tm), in_specs=in_specs,
        out_specs=[pl.BlockSpec((tm, tn), lambda j, i: (i, j)) for _ in out_dtypes],
        out_shape=[jax.ShapeDtypeStruct((s, n), dt) for dt in out_dtypes],
        compiler_params=_params(2),
    )(*operands)
    return out


def _mm_tn(name, a, b, *, tk=512, tn=None, out_dtype=BF16):
    s, m = a.shape
    n = b.shape[1]
    tn = n if tn is None else tn
    tk = min(tk, s)
    n_k = s // tk
    assert s % tk == 0 and n % tn == 0

    def body(a_ref, b_ref, o_ref, acc_ref):
        k = pl.program_id(1)

        @pl.when(k == 0)
        def _():
            acc_ref[...] = jnp.zeros_like(acc_ref)

        acc_ref[...] += lax.dot_general(a_ref[...], b_ref[...], (((0,), (0,)), ((), ())), preferred_element_type=F32)

        @pl.when(k == n_k - 1)
        def _():
            o_ref[...] = acc_ref[...].astype(o_ref.dtype)

    return pl.pallas_call(
        body, name=name, grid=(n // tn, n_k),
        in_specs=[pl.BlockSpec((tk, m), lambda j, k: (k, 0)), pl.BlockSpec((tk, tn), lambda j, k: (k, j))],
        out_specs=pl.BlockSpec((m, tn), lambda j, k: (0, j)),
        out_shape=jax.ShapeDtypeStruct((m, n), out_dtype),
        scratch_shapes=[pltpu.VMEM((m, tn), F32)],
        compiler_params=_params(2),
    )(a, b)


def _pick_rows(s, target):
    if s <= target:
        return s
    return max(t for t in range(16, target + 1, 16) if s % t == 0)


def _rowwise(name, body, rows, vecs, outs, accs=(), *, ts=512):
    rows = [(r if isinstance(r, tuple) else (r, r.shape[1], 0)) for r in rows]
    rows = [r if len(r) == 4 else (*r, None) for r in rows]
    s = rows[0][0].shape[0]
    ts = _pick_rows(s, ts)
    n_r, n_v, n_o, n_a = len(rows), len(vecs), len(outs), len(accs)

    def kbody(*refs):
        r_refs, v_refs = refs[:n_r], refs[n_r:n_r + n_v]
        o_refs, a_refs = refs[n_r + n_v:n_r + n_v + n_o], refs[n_r + n_v + n_o:]
        res_o, res_a = body([r[...] for r in r_refs], [v[...] for v in v_refs])
        for o_ref, o in zip(o_refs, res_o, strict=True):
            o_ref[...] = o.astype(o_ref.dtype)
        if n_a:
            first = pl.program_id(0) == 0

            @pl.when(first)
            def _():
                for a_ref, a in zip(a_refs, res_a, strict=True):
                    a_ref[...] = a

            @pl.when(jnp.logical_not(first))
            def _():
                for a_ref, a in zip(a_refs, res_a, strict=True):
                    a_ref[...] += a

    in_specs = [
        pl.BlockSpec((ts, w), functools.partial(lambda i, cb: (i, cb), cb=cb)) if lead is None else
        pl.BlockSpec((None, ts, w), functools.partial(lambda i, cb, lead: (lead, i, cb), cb=cb, lead=lead))
        for _, w, cb, lead in rows]
    in_specs += [pl.BlockSpec(v.shape, functools.partial(lambda i, nd: (0,) * nd, nd=v.ndim)) for v in vecs]
    out_specs = [pl.BlockSpec((ts, w), lambda i: (i, 0)) for w, _ in outs]
    out_specs += [pl.BlockSpec(shp, functools.partial(lambda i, nd: (0,) * nd, nd=len(shp))) for shp in accs]
    out_shape = [jax.ShapeDtypeStruct((s, w), dt) for w, dt in outs] + [jax.ShapeDtypeStruct(shp, F32) for shp in accs]
    return pl.pallas_call(
        kbody, name=name, grid=(s // ts,), in_specs=in_specs, out_specs=out_specs, out_shape=out_shape,
        compiler_params=_params(1),
    )(*[r[0] for r in rows], *vecs)


def _colsum(v):
    return jnp.sum(v, axis=0, keepdims=True)


def _rms_parts(x):
    r = lax.rsqrt(jnp.mean(x * x, axis=-1, keepdims=True) + EPS)
    return x * r, r


def _rms_bwd(dxhat, xhat, r):
    return r * (dxhat - xhat * jnp.mean(dxhat * xhat, axis=-1, keepdims=True))


def _modulate(name, x, ng, sc, sh):
    def body(rows, vecs):
        (xv,), (g, s_, b) = rows, vecs
        xhat, _ = _rms_parts(xv)
        return [xhat * g * (1.0 + s_) + b], []

    return _rowwise(name, body, [x], [ng, sc, sh], [(D, BF16)])[0]


def _modulate_bwd(name, dh, x, dxo, ng, sc):
    def body(rows, vecs):
        (dhv, xv, dxov), (g, s_) = rows, vecs
        xhat, r = _rms_parts(xv)
        dn = dhv * (1.0 + s_)
        dx = _rms_bwd(dn * g, xhat, r)
        return [dxov + dx], [_colsum(dhv), _colsum(dhv * xhat * g), _colsum(dn * xhat)]

    return _rowwise(name, body, [dh, x, dxo], [ng, sc], [(D, F32)], [(1, D)] * 3)


def _gate_scale(name, dx, f, gate, coef):
    def body(rows, vecs):
        (dxv, fv), (g,) = rows, vecs
        return [coef * g * dxv], [_colsum(coef * dxv * fv.astype(F32))]

    return _rowwise(name, body, [dx, f], [gate], [(D, BF16)], [(1, D)])


def _ffn_fwd(tag, x, ng, sc, sh, gate, wg, wu, wd):
    h = _modulate(f"{tag}_mod", x, ng, sc, sh)

    def up_epi(accs, _t, _v):
        a, u = accs
        return [a, u, a * _sigmoid(a) * u]

    a, u, t = _mm(f"{tag}_up", [(h, wg), (h, wu)], up_epi, [BF16] * 3, tn=D_FF // 2)

    def down_epi(accs, tl, vs):
        return [tl[0] + 0.5 * vs[0] * accs[0], accs[0]]

    x_out, f = _mm(f"{tag}_down", [(t, wd)], down_epi, [F32, BF16], tiles=[(x, 0)], vecs=[gate])
    return x_out, (x, h, a, u, f)


def _ffn_bwd(tag, dxo, saved, ng, sc, gate, wg, wu, wd):
    x, h, a, u, f = saved
    df, dgate = _gate_scale(f"{tag}_gs", dxo, f, gate, 0.5)

    def epi(accs, tl, _v):
        dt = accs[0]
        av, uv = tl[0].astype(F32), tl[1].astype(F32)
        sg = _sigmoid(av)
        sil = av * sg
        return [dt * uv * (sg * (1.0 + av * (1.0 - sg))), dt * sil, sil * uv]

    da, du, t = _mm(f"{tag}_bdown", [(df, wd)], epi, [BF16] * 3, nt=True, tiles=[(a, 0), (u, 0)], tn=D_FF // 2,
                    chunk=3 * LANES)
    dwd = _mm_tn(f"{tag}_dwd", t, df)
    dwg = _mm_tn(f"{tag}_dwg", h, da)
    dwu = _mm_tn(f"{tag}_dwu", h, du)
    dh = _mm(f"{tag}_bup", [(da, wg), (du, wu)], lambda accs, _t, _v: [accs[0] + accs[1]], [F32], nt=True)[0]
    dx_in, dsh, dsc, dng = _modulate_bwd(f"{tag}_modb", dh, x, dxo, ng, sc)
    return dx_in, (dsh, dsc, dgate, dng), (dwg, dwu, dwd)


def _rope_tables(s):
    half = HEAD_DIM // 2
    inv_freq = ROPE_THETA ** (-(jnp.arange(half, dtype=F32) * 2.0 / HEAD_DIM))
    ang = jnp.arange(s, dtype=F32)[:, None] * inv_freq[None, :]
    cos, sin = jnp.cos(ang), jnp.sin(ang)
    return jnp.tile(jnp.concatenate([cos, cos], axis=1), (1, 2)), jnp.tile(jnp.concatenate([-sin, sin], axis=1), (1, 2))


def _rotate(v, cos2, sin2, sign):
    w = v.shape[1]
    lane = lax.broadcasted_iota(jnp.int32, v.shape, 1)
    partner = jnp.where(lane % HEAD_DIM < HEAD_DIM // 2, pltpu.roll(v, w - HEAD_DIM // 2, 1), pltpu.roll(v, HEAD_DIM // 2, 1))
    reps = w // LANES
    return v * jnp.tile(cos2, (1, reps)) + partner * (sign * jnp.tile(sin2, (1, reps)))


SPLIT_TS = 512


def _dilated_spec(dil, ts):
    return pl.BlockSpec((dil, ts // dil, GROUP_W), lambda i: (0, i, 0))


def _dilated_shape(s, dil, dtype):
    return jax.ShapeDtypeStruct((dil, s // dil, GROUP_W), dtype)


CHUNKS_PER_GROUP = GROUP_W // LANES


def _put(buf, chunk0, val):
    for c in range(val.shape[1] // LANES):
        buf[chunk0 + c] = val[:, c * LANES:(c + 1) * LANES]


def _get(buf, chunk0, n):
    return jnp.concatenate([buf[chunk0 + c] for c in range(n)], axis=1)


def _strided_rows(r, dil, ts):
    return pl.ds(r, ts // dil, stride=dil) if dil > 1 else pl.ds(0, ts)


def _deinterleave_one(buf, chunk0, out_ref, dil, ts):
    for half in range(CHUNKS_PER_GROUP):
        for r in range(dil):
            src = buf.at[chunk0 + half][_strided_rows(r, dil, ts), :]
            out_ref.at[r][:, pl.ds(half * LANES, LANES)] = src.astype(out_ref.dtype)


def _interleave_one(in_ref, buf, chunk0, dil, ts):
    for half in range(CHUNKS_PER_GROUP):
        for r in range(dil):
            src = in_ref.at[r][:, pl.ds(half * LANES, LANES)]
            buf.at[chunk0 + half][_strided_rows(r, dil, ts), :] = src.astype(F32)


def _deinterleave(buf, chunk0, out_refs, ts):
    for g, dil in enumerate(ATTN_DILATIONS):
        _deinterleave_one(buf, chunk0 + g * CHUNKS_PER_GROUP, out_refs[g], dil, ts)


def _interleave(in_refs, buf, chunk0, ts):
    for g, dil in enumerate(ATTN_DILATIONS):
        _interleave_one(in_refs[g], buf, chunk0 + g * CHUNKS_PER_GROUP, dil, ts)


def _rope_split(name, qkv, cos2, sin2):
    s, ts = qkv.shape[0], SPLIT_TS

    def body(qkv_ref, c_ref, s_ref, *rest):
        outs, buf = rest[:9], rest[9]
        c2, s2 = c_ref[...], s_ref[...]
        per = QKV_W // LANES
        _put(buf, 0, _rotate(qkv_ref[:, pl.ds(0, QKV_W)].astype(F32), c2, s2, 1.0))
        _put(buf, per, _rotate(qkv_ref[:, pl.ds(QKV_W, QKV_W)].astype(F32), c2, s2, 1.0))
        _put(buf, 2 * per, qkv_ref[:, pl.ds(2 * QKV_W, QKV_W)].astype(F32))
        for t in range(3):
            _deinterleave(buf, t * per, outs[3 * t:3 * t + 3], ts)

    tab = pl.BlockSpec((ts, LANES), lambda i: (i, 0))
    outs = pl.pallas_call(
        body, name=name, grid=(s // ts,), in_specs=[pl.BlockSpec((ts, 3 * QKV_W), lambda i: (i, 0)), tab, tab],
        out_specs=[_dilated_spec(d, ts) for _ in range(3) for d in ATTN_DILATIONS],
        out_shape=[_dilated_shape(s, d, BF16) for _ in range(3) for d in ATTN_DILATIONS],
        scratch_shapes=[pltpu.VMEM((3 * QKV_W // LANES, ts, LANES), F32)], compiler_params=_params(1),
    )(qkv, cos2, sin2)
    return outs[0:3], outs[3:6], outs[6:9]


def _rope_join(name, dq, dk, dv, cos2, sin2):
    s, ts = cos2.shape[0], SPLIT_TS

    def body(*refs):
        ins, c_ref, s_ref, o_ref, buf = refs[:9], refs[9], refs[10], refs[11], refs[12]
        per = QKV_W // LANES
        for t in range(3):
            _interleave(ins[3 * t:3 * t + 3], buf, t * per, ts)
        c2, s2 = c_ref[...], s_ref[...]
        o_ref[:, pl.ds(0, QKV_W)] = _rotate(_get(buf, 0, per), c2, s2, -1.0).astype(o_ref.dtype)
        o_ref[:, pl.ds(QKV_W, QKV_W)] = _rotate(_get(buf, per, per), c2, s2, -1.0).astype(o_ref.dtype)
        o_ref[:, pl.ds(2 * QKV_W, QKV_W)] = _get(buf, 2 * per, per).astype(o_ref.dtype)

    tab = pl.BlockSpec((ts, LANES), lambda i: (i, 0))
    return pl.pallas_call(
        body, name=name, grid=(s // ts,),
        in_specs=[_dilated_spec(d, ts) for _ in range(3) for d in ATTN_DILATIONS] + [tab, tab],
        out_specs=pl.BlockSpec((ts, 3 * QKV_W), lambda i: (i, 0)), out_shape=jax.ShapeDtypeStruct((s, 3 * QKV_W), BF16),
        scratch_shapes=[pltpu.VMEM((3 * QKV_W // LANES, ts, LANES), F32)], compiler_params=_params(1),
    )(*dq, *dk, *dv, cos2, sin2)


def _head_masks(shape):
    lane = lax.broadcasted_iota(jnp.int32, shape, 1)
    return [jnp.logical_and(lane >= h * HEAD_DIM, lane < (h + 1) * HEAD_DIM) for h in range(GROUP_W // HEAD_DIM)]


def _grad_split(name, do, o, lse):
    s, ts = do.shape[0], SPLIT_TS

    def body(do_ref, o_ref, l_ref, *rest):
        outs, buf = rest[:9], rest[9]
        dov = do_ref[...].astype(F32)
        prod = dov * o_ref[...].astype(F32)
        delta = jnp.zeros_like(prod)
        for hm in _head_masks(prod.shape):
            delta = jnp.where(hm, jnp.sum(jnp.where(hm, prod, 0.0), axis=1, keepdims=True), delta)
        _put(buf, 0, dov)
        _put(buf, CHUNKS_PER_GROUP, delta)
        _put(buf, 2 * CHUNKS_PER_GROUP, l_ref[...])
        for t in range(3):
            for g in range(3):
                _deinterleave_one(buf, t * CHUNKS_PER_GROUP, outs[3 * t + g], ATTN_DILATIONS[g], ts)

    nat = pl.BlockSpec((ts, GROUP_W), lambda i: (i, 0))
    dts = [BF16, F32, F32]
    outs = pl.pallas_call(
        body, name=name, grid=(s // ts,), in_specs=[nat, nat, nat],
        out_specs=[_dilated_spec(d, ts) for _ in range(3) for d in ATTN_DILATIONS],
        out_shape=[_dilated_shape(s, d, dt) for dt in dts for d in ATTN_DILATIONS],
        scratch_shapes=[pltpu.VMEM((3 * CHUNKS_PER_GROUP, ts, LANES), F32)], compiler_params=_params(1),
    )(do, o, lse)
    return outs[0:3], outs[3:6], outs[6:9]


def _band_masks(has_prev):
    qi = lax.broadcasted_iota(jnp.int32, (BLK, BLK), 0)
    kj = lax.broadcasted_iota(jnp.int32, (BLK, BLK), 1)
    return kj <= qi, jnp.logical_and(kj >= qi, has_prev)


def _dot_nt(a, b):
    return lax.dot_general(a, b, (((1,), (1,)), ((), ())), preferred_element_type=F32)


def _dot_tn(a, b):
    return lax.dot_general(a, b, (((0,), (0,)), ((), ())), preferred_element_type=F32)


def _dot(a, b):
    return jnp.dot(a, b, preferred_element_type=F32)


ATTN_BLK = (None, BLK, GROUP_W)


def _attn_specs(clamp):
    cur = pl.BlockSpec(ATTN_BLK, lambda r, n: (r, clamp(n), 0))
    prev = pl.BlockSpec(ATTN_BLK, lambda r, n: (r, jnp.maximum(clamp(n) - 1, 0), 0))
    return [cur, cur, prev, cur, prev]


def _attn_fwd(name, q, k, v):
    dil, rows, _ = q.shape
    nb = rows // BLK
    scale = HEAD_DIM ** -0.5

    def body(q_ref, kc_ref, kp_ref, vc_ref, vp_ref, o_ref, l_ref):
        mask_c, mask_p = _band_masks(pl.program_id(1) > 0)
        q, kc, kp, vc, vp = q_ref[...], kc_ref[...], kp_ref[...], vc_ref[...], vp_ref[...]
        o_acc = jnp.zeros((BLK, GROUP_W), F32)
        l_acc = jnp.zeros((BLK, GROUP_W), F32)
        for hm in _head_masks((BLK, GROUP_W)):
            qm = jnp.where(hm, q, jnp.zeros_like(q))
            sc = jnp.where(mask_c, _dot_nt(qm, kc) * scale, NEG_INF)
            sp = jnp.where(mask_p, _dot_nt(qm, kp) * scale, NEG_INF)
            m = jnp.maximum(jnp.max(sc, axis=1, keepdims=True), jnp.max(sp, axis=1, keepdims=True))
            pc, pp = jnp.exp(sc - m), jnp.exp(sp - m)
            den = jnp.sum(pc, axis=1, keepdims=True) + jnp.sum(pp, axis=1, keepdims=True)
            oh = (_dot(pc.astype(BF16), vc) + _dot(pp.astype(BF16), vp)) / den
            o_acc = jnp.where(hm, oh, o_acc)
            l_acc = jnp.where(hm, m + jnp.log(den), l_acc)
        o_ref[...] = o_acc
        l_ref[...] = l_acc

    out_spec = pl.BlockSpec(ATTN_BLK, lambda r, n: (r, n, 0))
    return pl.pallas_call(
        body, name=name, grid=(dil, nb), in_specs=_attn_specs(lambda n: n), out_specs=[out_spec, out_spec],
        out_shape=[jax.ShapeDtypeStruct(q.shape, F32)] * 2, compiler_params=_params(2),
    )(q, k, k, v, v)


def _attn_merge(name, os_, ls_):
    s, ts = os_[0].shape[0] * os_[0].shape[1], SPLIT_TS

    def body(*refs):
        o_refs, l_refs, o_ref, l_ref, buf = refs[0:3], refs[3:6], refs[6], refs[7], refs[8]
        _interleave(o_refs, buf, 0, ts)
        _interleave(l_refs, buf, 3 * CHUNKS_PER_GROUP, ts)
        o0, o1, o2 = [_get(buf, g * CHUNKS_PER_GROUP, CHUNKS_PER_GROUP) for g in range(3)]
        l0, l1, l2 = [_get(buf, (3 + g) * CHUNKS_PER_GROUP, CHUNKS_PER_GROUP) for g in range(3)]
        m = jnp.maximum(jnp.maximum(l0, l1), l2)
        e0, e1, e2 = jnp.exp(l0 - m), jnp.exp(l1 - m), jnp.exp(l2 - m)
        tot = e0 + e1 + e2
        o_ref[...] = ((e0 * o0 + e1 * o1 + e2 * o2) / tot).astype(o_ref.dtype)
        l_ref[...] = m + jnp.log(tot)

    nat = pl.BlockSpec((ts, GROUP_W), lambda i: (i, 0))
    return pl.pallas_call(
        body, name=name, grid=(s // ts,), in_specs=[_dilated_spec(d, ts) for _ in range(2) for d in ATTN_DILATIONS],
        out_specs=[nat, nat], out_shape=[jax.ShapeDtypeStruct((s, GROUP_W), BF16), jax.ShapeDtypeStruct((s, GROUP_W), F32)],
        scratch_shapes=[pltpu.VMEM((6 * CHUNKS_PER_GROUP, ts, LANES), F32)], compiler_params=_params(1),
    )(*os_, *ls_)


def _attn_bwd(name, q, k, v, do, delta, lse):
    dil, rows, _ = q.shape
    nb = rows // BLK
    scale = HEAD_DIM ** -0.5

    def body(q_ref, kc_ref, kp_ref, vc_ref, vp_ref, do_ref, dl_ref, l_ref, dq_ref, dk_ref, dv_ref, ck_ref, cv_ref):
        n = pl.program_id(1)

        @pl.when(n == 0)
        def _():
            ck_ref[...] = jnp.zeros_like(ck_ref)
            cv_ref[...] = jnp.zeros_like(cv_ref)

        @pl.when(n < nb)
        def _():
            mask_c, mask_p = _band_masks(n > 0)
            q, kc, kp, vc, vp, dov = q_ref[...], kc_ref[...], kp_ref[...], vc_ref[...], vp_ref[...], do_ref[...]
            lb, db = l_ref[...], dl_ref[...]
            zero = jnp.zeros((BLK, GROUP_W), F32)
            dq_acc, dkc, dkp, dvc, dvp = zero, zero, zero, zero, zero
            for hm in _head_masks((BLK, GROUP_W)):
                qm = jnp.where(hm, q, jnp.zeros_like(q))
                dom = jnp.where(hm, dov, jnp.zeros_like(dov))
                lh = jnp.max(jnp.where(hm, lb, NEG_INF), axis=1, keepdims=True)
                delta = jnp.max(jnp.where(hm, db, NEG_INF), axis=1, keepdims=True)
                pc = jnp.exp(jnp.where(mask_c, _dot_nt(qm, kc) * scale, NEG_INF) - lh)
                pp = jnp.exp(jnp.where(mask_p, _dot_nt(qm, kp) * scale, NEG_INF) - lh)
                dsc = (pc * (_dot_nt(dom, vc) - delta) * scale).astype(BF16)
                dsp = (pp * (_dot_nt(dom, vp) - delta) * scale).astype(BF16)
                dq_acc = jnp.where(hm, _dot(dsc, kc) + _dot(dsp, kp), dq_acc)
                dkc += _dot_tn(dsc, qm)
                dkp += _dot_tn(dsp, qm)
                dvc += _dot_tn(pc.astype(BF16), dom)
                dvp += _dot_tn(pp.astype(BF16), dom)
            dq_ref[...] = dq_acc.astype(dq_ref.dtype)
            dk_ref[...] = (ck_ref[...] + dkp).astype(dk_ref.dtype)
            dv_ref[...] = (cv_ref[...] + dvp).astype(dv_ref.dtype)
            ck_ref[...] = dkc
            cv_ref[...] = dvc

        @pl.when(n == nb)
        def _():
            dk_ref[...] = ck_ref[...].astype(dk_ref.dtype)
            dv_ref[...] = cv_ref[...].astype(dv_ref.dtype)

    clamp = lambda n: jnp.minimum(n, nb - 1)
    qspec = pl.BlockSpec(ATTN_BLK, lambda r, n: (r, clamp(n), 0))
    kspec = pl.BlockSpec(ATTN_BLK, lambda r, n: (r, jnp.maximum(n - 1, 0), 0))
    return pl.pallas_call(
        body, name=name, grid=(dil, nb + 1), in_specs=_attn_specs(clamp) + [qspec, qspec, qspec],
        out_specs=[qspec, kspec, kspec], out_shape=[jax.ShapeDtypeStruct(q.shape, BF16)] * 3,
        scratch_shapes=[pltpu.VMEM((BLK, GROUP_W), F32), pltpu.VMEM((BLK, GROUP_W), F32)], compiler_params=_params(2),
    )(q, k, k, v, v, do, delta, lse)


CONV_TS = 128
HALO = 32
SHIFT_ROWS = CONV_TS + HALO - 8


def _make_shifts(buf, sh):
    for s_ in range(1, 8):
        sh[s_ - 1] = buf[pl.ds(s_, SHIFT_ROWS), :]


def _window(buf, sh, off, ts, cols):
    q, s_ = divmod(off, 8)
    if s_ == 0:
        return buf[pl.ds(off, ts), cols]
    return sh[s_ - 1, pl.ds(8 * q, ts), cols]


def _conv_fwd(name, u, w, b, lg, lb):
    s = u.shape[0]
    ts, per = CONV_TS, CONV_TS // HALO

    def body(a_ref, g_ref, ap_ref, gp_ref, w_ref, b_ref, lg_ref, lb_ref, c_ref, act_ref, buf, cbuf, sh):
        i = pl.program_id(0)
        buf[pl.ds(HALO, ts), :] = a_ref[...].astype(F32) * _sigmoid(g_ref[...].astype(F32))
        prev = ap_ref[...].astype(F32) * _sigmoid(gp_ref[...].astype(F32))
        buf[pl.ds(0, HALO), :] = jnp.where(i > 0, prev, 0.0)
        _make_shifts(buf, sh)
        rb = ts // 2
        for lc in range(D // LANES):
            cols = pl.ds(lc * LANES, LANES)
            for r0 in range(0, ts, rb):
                acc = jnp.broadcast_to(b_ref[:, cols], (rb, LANES))
                for j in range(CONV_K):
                    acc = acc + w_ref[pl.ds(j, 1), cols] * _window(buf, sh, r0 + HALO - (CONV_K - 1) + j, rb, cols)
                cbuf[pl.ds(r0, rb), cols] = acc
        c = cbuf[...]
        mu = jnp.mean(c, axis=-1, keepdims=True)
        xc = c - mu
        ln = xc * lax.rsqrt(jnp.mean(xc * xc, axis=-1, keepdims=True) + EPS) * lg_ref[...] + lb_ref[...]
        c_ref[...] = c.astype(c_ref.dtype)
        act_ref[...] = (ln * _sigmoid(ln)).astype(act_ref.dtype)

    halo = lambda cb: pl.BlockSpec((HALO, D), functools.partial(lambda i, cb: (jnp.maximum(i * per - 1, 0), cb), cb=cb))
    vec = pl.BlockSpec((1, D), lambda i: (0, 0))
    return pl.pallas_call(
        body, name=name, grid=(s // ts,),
        in_specs=[pl.BlockSpec((ts, D), lambda i: (i, 0)), pl.BlockSpec((ts, D), lambda i: (i, 1)), halo(0), halo(1),
                  pl.BlockSpec((HALO, D), lambda i: (0, 0)), vec, vec, vec],
        out_specs=[pl.BlockSpec((ts, D), lambda i: (i, 0))] * 2,
        out_shape=[jax.ShapeDtypeStruct((s, D), BF16)] * 2,
        scratch_shapes=[pltpu.VMEM((ts + HALO, D), F32), pltpu.VMEM((ts, D), F32), pltpu.VMEM((7, SHIFT_ROWS, D), F32)],
        compiler_params=_params(1),
    )(u, u, u, u, w, b, lg, lb)


def _ln_bwd(name, dact, c, lg, lb):
    def body(rows, vecs):
        (dv, cv), (g, b) = rows, vecs
        cv = cv.astype(F32)
        mu = jnp.mean(cv, axis=-1, keepdims=True)
        xc = cv - mu
        rstd = lax.rsqrt(jnp.mean(xc * xc, axis=-1, keepdims=True) + EPS)
        xh = xc * rstd
        ln = xh * g + b
        sg = _sigmoid(ln)
        dln = dv * (sg * (1.0 + ln * (1.0 - sg)))
        dxh = dln * g
        dc = rstd * (dxh - jnp.mean(dxh, axis=-1, keepdims=True) - xh * jnp.mean(dxh * xh, axis=-1, keepdims=True))
        return [dc], [_colsum(dln * xh), _colsum(dln)]

    return _rowwise(name, body, [dact, c], [lg, lb], [(D, F32)], [(1, D), (1, D)])


def _conv_bwd(name, dc, u, w):
    s = u.shape[0]
    ts, per = CONV_TS, CONV_TS // HALO
    n_t = s // ts

    def body(dc_ref, dn_ref, a_ref, g_ref, ap_ref, gp_ref, w_ref, du_ref, dw_ref, db_ref, buf, dbuf, hbuf, sh, dsh):
        i = pl.program_id(0)
        a = a_ref[...].astype(F32)
        sg = _sigmoid(g_ref[...].astype(F32))
        buf[pl.ds(HALO, ts), :] = a * sg
        prev = ap_ref[...].astype(F32) * _sigmoid(gp_ref[...].astype(F32))
        buf[pl.ds(0, HALO), :] = jnp.where(i > 0, prev, 0.0)
        dcv = dc_ref[...]
        dbuf[pl.ds(0, ts), :] = dcv
        dbuf[pl.ds(ts, HALO), :] = jnp.where(i < n_t - 1, dn_ref[...], 0.0)

        @pl.when(i == 0)
        def _():
            dw_ref[...] = jnp.zeros_like(dw_ref)
            db_ref[...] = jnp.zeros_like(db_ref)

        db_ref[...] += _colsum(dcv)
        _make_shifts(buf, sh)
        _make_shifts(dbuf, dsh)
        rb = ts // 4
        for lc in range(D // LANES):
            cols = pl.ds(lc * LANES, LANES)
            for r0 in range(0, ts, rb):
                d0 = dbuf[pl.ds(r0, rb), cols]
                acc = jnp.zeros((rb, LANES), F32)
                for j in range(CONV_K):
                    acc = acc + w_ref[pl.ds(j, 1), cols] * _window(dbuf, dsh, r0 + CONV_K - 1 - j, rb, cols)
                    part = d0 * _window(buf, sh, r0 + HALO - (CONV_K - 1) + j, rb, cols)
                    dw_ref[pl.ds(8 * j, 8), cols] += jnp.sum(part.reshape(rb // 8, 8, LANES), axis=0)
                hbuf[pl.ds(r0, rb), cols] = acc
        dh = hbuf[...]
        du_ref[:, pl.ds(0, D)] = (dh * sg).astype(du_ref.dtype)
        du_ref[:, pl.ds(D, D)] = (dh * a * sg * (1.0 - sg)).astype(du_ref.dtype)

    halo = lambda cb: pl.BlockSpec((HALO, D), functools.partial(lambda i, cb: (jnp.maximum(i * per - 1, 0), cb), cb=cb))
    nxt = pl.BlockSpec((HALO, D), lambda i: (jnp.minimum((i + 1) * per, s // HALO - 1), 0))
    return pl.pallas_call(
        body, name=name, grid=(n_t,),
        in_specs=[pl.BlockSpec((ts, D), lambda i: (i, 0)), nxt, pl.BlockSpec((ts, D), lambda i: (i, 0)),
                  pl.BlockSpec((ts, D), lambda i: (i, 1)), halo(0), halo(1), pl.BlockSpec((HALO, D), lambda i: (0, 0))],
        out_specs=[pl.BlockSpec((ts, 2 * D), lambda i: (i, 0)), pl.BlockSpec((8 * CONV_K, D), lambda i: (0, 0)),
                   pl.BlockSpec((1, D), lambda i: (0, 0))],
        out_shape=[jax.ShapeDtypeStruct((s, 2 * D), BF16), jax.ShapeDtypeStruct((8 * CONV_K, D), F32),
                   jax.ShapeDtypeStruct((1, D), F32)],
        scratch_shapes=[pltpu.VMEM((ts + HALO, D), F32), pltpu.VMEM((ts + HALO, D), F32), pltpu.VMEM((ts, D), F32),
                        pltpu.VMEM((7, SHIFT_ROWS, D), F32), pltpu.VMEM((7, SHIFT_ROWS, D), F32)],
        compiler_params=_params(1),
    )(dc, dc, u, u, u, u, w)


def _mix_fwd(tag, x1, ng, sc, sh, gate, wts, cos2, sin2):
    h1 = _modulate(f"{tag}_mod", x1, ng, sc, sh)
    ident = lambda accs, _t, _v: accs
    qkv = _mm(f"{tag}_qkv", [(h1, wts["w_qkv"])], ident, [BF16], tn=QKV_W)[0]
    u = _mm(f"{tag}_u", [(h1, wts["w_u"])], ident, [BF16], tn=D)[0]
    ga, gc = _mm(f"{tag}_gates", [(h1, wts["w_ga"]), (h1, wts["w_gc"])], ident, [BF16] * 2)
    qd, kd, vd = _rope_split(f"{tag}_rope", qkv, cos2, sin2)
    per_group = [_attn_fwd(f"{tag}_attn{g}", qd[g], kd[g], vd[g]) for g in range(len(ATTN_DILATIONS))]
    o, lse = _attn_merge(f"{tag}_merge", [p[0] for p in per_group], [p[1] for p in per_group])
    cpre, act = _conv_fwd(f"{tag}_conv", u, wts["conv_w"], wts["conv_b"], wts["ln_g"], wts["ln_b"])

    def gate_epi(accs, tl, _v):
        ya, yc = accs
        return [_sigmoid(tl[0].astype(F32)) * ya + _sigmoid(tl[1].astype(F32)) * yc, ya, yc]

    y, ya, yc = _mm(f"{tag}_branch", [(o, wts["attn_wo"]), (act, wts["conv_wo"])], gate_epi, [BF16] * 3,
                    tiles=[(ga, 0), (gc, 0)])

    def res_epi(accs, tl, vs):
        return [tl[0] + vs[0] * accs[0], accs[0]]

    x2, f1 = _mm(f"{tag}_out", [(y, wts["w_out"])], res_epi, [F32, BF16], tiles=[(x1, 0)], vecs=[gate])
    return x2, (x1, h1, u, ga, gc, qd, kd, vd, o, lse, cpre, act, y, ya, yc, f1)


def _mix_bwd(tag, dx2, saved, ng, sc, gate, wts, cos2, sin2):
    x1, h1, u, ga, gc, qd, kd, vd, o, lse, cpre, act, y, ya, yc, f1 = saved
    dfm, dgate = _gate_scale(f"{tag}_gs", dx2, f1, gate, 1.0)

    def epi(accs, tl, _v):
        dy = accs[0]
        sa, sc_ = _sigmoid(tl[0].astype(F32)), _sigmoid(tl[1].astype(F32))
        return [dy * sa, dy * sc_, dy * tl[2].astype(F32) * sa * (1.0 - sa), dy * tl[3].astype(F32) * sc_ * (1.0 - sc_)]

    dya, dyc, dga, dgc = _mm(f"{tag}_bout", [(dfm, wts["w_out"])], epi, [BF16] * 4, nt=True,
                             tiles=[(ga, 0), (gc, 0), (ya, 0), (yc, 0)], chunk=2 * LANES)
    grads = {"w_out": _mm_tn(f"{tag}_dwout", y, dfm), "attn_wo": _mm_tn(f"{tag}_dwattn", o, dya),
             "conv_wo": _mm_tn(f"{tag}_dwconv", act, dyc)}
    ident = lambda accs, _t, _v: accs
    do = _mm(f"{tag}_battn", [(dya, wts["attn_wo"])], ident, [BF16], nt=True)[0]
    dact = _mm(f"{tag}_bconv", [(dyc, wts["conv_wo"])], ident, [F32], nt=True)[0]
    dc, dlg, dlb = _ln_bwd(f"{tag}_lnb", dact, cpre, wts["ln_g"], wts["ln_b"])
    du, dw8, dcb = _conv_bwd(f"{tag}_convb", dc, u, wts["conv_w"])
    dod, deltad, lsed = _grad_split(f"{tag}_gsplit", do, o, lse)
    dqs, dks, dvs = [], [], []
    for g in range(len(ATTN_DILATIONS)):
        dq, dk, dv = _attn_bwd(f"{tag}_attnb{g}", qd[g], kd[g], vd[g], dod[g], deltad[g], lsed[g])
        dqs.append(dq); dks.append(dk); dvs.append(dv)
    dqkv = _rope_join(f"{tag}_ropeb", dqs, dks, dvs, cos2, sin2)
    grads["w_in"] = jnp.concatenate([_mm_tn(f"{tag}_dwqkv", h1, dqkv), _mm_tn(f"{tag}_dwu", h1, du),
                                     _mm_tn(f"{tag}_dwga", h1, dga), _mm_tn(f"{tag}_dwgc", h1, dgc)], axis=1)
    dh1 = _mm(f"{tag}_bin", [(dqkv, wts["w_qkv"]), (du, wts["w_u"]), (dga, wts["w_ga"]), (dgc, wts["w_gc"])],
              lambda accs, _t, _v: [accs[0] + accs[1] + accs[2] + accs[3]], [F32], nt=True)[0]
    dx1, dsh, dsc, dng = _modulate_bwd(f"{tag}_modb", dh1, x1, dx2, ng, sc)
    small = {"conv_w": dw8.reshape(CONV_K, 8, D).sum(axis=1), "conv_b": dcb, "ln_g": dlg, "ln_b": dlb}
    return dx1, (dsh, dsc, dgate, dng), grads, small


def _loss_head(name, x, target, fg):
    def body(rows, vecs):
        (xv, tv), (g,) = rows, vecs
        xhat, r = _rms_parts(xv)
        err = xhat * g - tv
        dy = err * (1.0 / D)
        return [_rms_bwd(dy * g, xhat, r)], [_colsum(err * err), _colsum(dy * xhat)]

    return _rowwise(name, body, [x, target], [fg], [(D, F32)], [(1, D), (1, D)])


def _local_step(x, target, mod, norm_g, wts, final_g):
    s = x.shape[0]
    cos2, sin2 = _rope_tables(s)
    row = lambda a: a[None, :]
    saved = []
    for l in range(2):
        m = lambda i: row(mod[l, i])
        w = wts[l]
        x, s0 = _ffn_fwd(f"l{l}f0", x, row(norm_g[l, 0]), m(1), m(0), m(2), w["wg0"], w["wu0"], w["wd0"])
        x, s1 = _mix_fwd(f"l{l}mx", x, row(norm_g[l, 1]), m(4), m(3), m(5), w, cos2, sin2)
        x, s2 = _ffn_fwd(f"l{l}f1", x, row(norm_g[l, 2]), m(7), m(6), m(8), w["wg1"], w["wu1"], w["wd1"])
        saved.append((s0, s1, s2))
    dx, sq, dfg = _loss_head("loss_head", x, target, row(final_g))
    loss = (0.5 / D) * jnp.sum(sq)
    dmod, dng, big, small = [None, None], [None, None], [None, None], [None, None]
    for l in (1, 0):
        m = lambda i: row(mod[l, i])
        w = wts[l]
        s0, s1, s2 = saved[l]
        dx, v2, (dwg1, dwu1, dwd1) = _ffn_bwd(f"l{l}f1", dx, s2, row(norm_g[l, 2]), m(7), m(8), w["wg1"], w["wu1"], w["wd1"])
        dx, v1, gmix, small[l] = _mix_bwd(f"l{l}mx", dx, s1, row(norm_g[l, 1]), m(4), m(5), w, cos2, sin2)
        dx, v0, (dwg0, dwu0, dwd0) = _ffn_bwd(f"l{l}f0", dx, s0, row(norm_g[l, 0]), m(1), m(2), w["wg0"], w["wu0"], w["wd0"])
        dmod[l] = jnp.concatenate([jnp.concatenate(v[:3], axis=0) for v in (v0, v1, v2)], axis=0)
        dng[l] = jnp.concatenate([v0[3], v1[3], v2[3]], axis=0)
        big[l] = dict(gmix, ffn_wg=jnp.stack([dwg0, dwg1]), ffn_wu=jnp.stack([dwu0, dwu1]), ffn_wd=jnp.stack([dwd0, dwd1]))
    big = {k: jnp.stack([big[0][k], big[1][k]]) for k in big[0]}
    small = {k: jnp.stack([small[0][k].reshape(-1, D), small[1][k].reshape(-1, D)]) for k in small[0]}
    small = dict(small, norm_g=jnp.stack(dng), final_g=dfg.reshape(D))
    return loss, dx, jnp.stack(dmod), small, big


HBM_SPEC = pl.BlockSpec(memory_space=pl.ANY)


def _place():
    return lax.axis_index("x"), lax.axis_index("y"), lax.axis_index("c")


N_PEERS = N_DEV - 1


def _all_gather(name, arrays):
    n_a = len(arrays)

    def body(*refs):
        x_refs, out_refs = refs[:n_a], refs[n_a:2 * n_a]
        send_sems, recv_sems, local_sems = refs[2 * n_a:]
        x, y, c = _place()
        me, sibling = (x, y, c), (x, y, 1 - c)
        chips = [(1 - x, y), (x, 1 - y), (1 - x, 1 - y)]

        def copy(a, k, block, to, from_input=False):
            px, py, pc = block
            rows = out_refs[a].at[4 * px + 2 * py + pc]
            return pltpu.make_async_remote_copy(
                src_ref=x_refs[a] if from_input else rows, dst_ref=rows, send_sem=send_sems.at[a * N_PEERS + k],
                recv_sem=recv_sems.at[a * N_PEERS + k], device_id=to, device_id_type=pl.DeviceIdType.MESH)

        mine = [pltpu.make_async_copy(x_refs[a], out_refs[a].at[4 * x + 2 * y + c], local_sems.at[a]) for a in range(n_a)]
        for cp in mine:
            cp.start()
        first = []
        for j, chip in enumerate(chips):
            first += [copy(a, 1 + j, me, (*chip, c), from_input=True) for a in range(n_a)]
        first += [copy(a, 0, me, sibling, from_input=True) for a in range(n_a)]
        for cp in first:
            cp.start()
        passed = []
        for j, chip in enumerate(chips):
            for a in range(n_a):
                copy(a, 1 + j, (*chip, c), me).wait_recv()
                passed.append(copy(a, 4 + j, (*chip, c), sibling))
                passed[-1].start()
        for a in range(n_a):
            copy(a, 0, sibling, me).wait_recv()
        for j, chip in enumerate(chips):
            for a in range(n_a):
                copy(a, 4 + j, (*chip, 1 - c), me).wait_recv()
        for cp in first + passed:
            cp.wait_send()
        for cp in mine:
            cp.wait()

    return pl.pallas_call(
        body, name=name, out_shape=[jax.ShapeDtypeStruct((N_DEV, *a.shape), a.dtype) for a in arrays],
        in_specs=[HBM_SPEC] * n_a, out_specs=[HBM_SPEC] * n_a,
        scratch_shapes=[pltpu.SemaphoreType.DMA((n_a * N_PEERS,)), pltpu.SemaphoreType.DMA((n_a * N_PEERS,)),
                        pltpu.SemaphoreType.DMA((n_a,))],
    )(*arrays)


def _exchange_blocks(name, arrays):
    n_a = len(arrays)

    def body(*refs):
        g_refs, out_refs = refs[:n_a], refs[n_a:2 * n_a]
        send_sems, recv_sems, local_sems = refs[2 * n_a:]
        x, y, c = _place()
        my = 4 * x + 2 * y + c
        local = [pltpu.make_async_copy(g_refs[a].at[my], out_refs[a].at[my], local_sems.at[a]) for a in range(n_a)]
        for cp in local:
            cp.start()
        copies = []
        for k in (4, 2, 6, 1, 5, 3, 7):
            px = 1 - x if k & 4 else x
            py = 1 - y if k & 2 else y
            pc = 1 - c if k & 1 else c
            for a in range(n_a):
                copies.append(pltpu.make_async_remote_copy(
                    src_ref=g_refs[a].at[4 * px + 2 * py + pc], dst_ref=out_refs[a].at[my],
                    send_sem=send_sems.at[a * N_PEERS + k - 1], recv_sem=recv_sems.at[a * N_PEERS + k - 1],
                    device_id=(px, py, pc), device_id_type=pl.DeviceIdType.MESH))
        for cp in copies:
            cp.start()
        for cp in copies:
            cp.wait()
        for cp in local:
            cp.wait()

    return pl.pallas_call(
        body, name=name, out_shape=[jax.ShapeDtypeStruct(a.shape, a.dtype) for a in arrays],
        in_specs=[HBM_SPEC] * n_a, out_specs=[HBM_SPEC] * n_a,
        scratch_shapes=[pltpu.SemaphoreType.DMA((n_a * N_PEERS,)), pltpu.SemaphoreType.DMA((n_a * N_PEERS,)),
                        pltpu.SemaphoreType.DMA((n_a,))],
    )(*arrays)


SLAB_TS = 2048


def _sum_parts(name, parts):
    def body(rows, _v):
        tot = rows[0].astype(F32)
        for r in rows[1:]:
            tot = tot + r.astype(F32)
        return [tot], []

    return _rowwise(name, body, list(parts), [], [(parts[0].shape[1], F32)], ts=SLAB_TS)[0]


def _adamw(name, w, parts, m, v):
    def body(rows, _v):
        wv, mv, vv = rows[0], rows[1], rows[2]
        g = rows[3].astype(F32)
        for r in rows[4:]:
            g = g + r.astype(F32)
        m2 = ADAM_B1 * mv + (1.0 - ADAM_B1) * g
        v2 = ADAM_B2 * vv + (1.0 - ADAM_B2) * (g * g)
        m_hat = m2 / (1.0 - ADAM_B1 ** ADAM_STEP)
        v_hat = v2 / (1.0 - ADAM_B2 ** ADAM_STEP)
        delta = -ADAM_LR * (m_hat / (jnp.sqrt(v_hat) + ADAM_EPS) + ADAM_WD * wv)
        return [g, delta, m2, v2], []

    width = w.shape[1]
    return _rowwise(name, body, [w, m, v, *parts], [], [(width, F32)] * 4, ts=max(16, SLAB_TS * LANES // width))


def _to_slab(arrays, dtype):
    flat = jnp.concatenate([a.reshape(-1).astype(dtype) for a in arrays])
    rows = -(-flat.shape[0] // LANES)
    rows = -(-rows // 8) * 8 if rows <= SLAB_TS else -(-rows // SLAB_TS) * SLAB_TS
    return jnp.pad(flat, (0, rows * LANES - flat.shape[0])).reshape(rows, LANES)


def _from_slab(slab, shapes, lead=()):
    flat = slab.reshape(*lead, -1)
    out, at = [], 0
    for shp in shapes:
        size = 1
        for d in shp:
            size *= d
        out.append(flat[..., at:at + size].reshape(*lead, *shp))
        at += size
    return out


BIG = {"ffn_wg": 3, "ffn_wu": 3, "ffn_wd": 2, "w_in": 2, "attn_wo": 2, "conv_wo": 1, "w_out": 1}


def _join(blocks, axis):
    full = jnp.moveaxis(blocks, 0, axis)
    return full.reshape(*full.shape[:axis], -1, *full.shape[axis + 2:])


def _split(full, axis):
    shp = full.shape
    return jnp.moveaxis(full.reshape(*shp[:axis], N_DEV, shp[axis] // N_DEV, *shp[axis + 1:]), axis, 0)


def kernel(x, c, ada_w, ada_b, norm_g, ffn_wg, ffn_wu, ffn_wd, w_in, attn_wo, conv_w, conv_b, conv_ln_g, conv_ln_b, conv_wo, w_out, final_g, loss_target, m_ada_w, m_ada_b, m_norm_g, m_ffn_wg, m_ffn_wu, m_ffn_wd, m_w_in, m_attn_wo, m_conv_w, m_conv_b, m_conv_ln_g, m_conv_ln_b, m_conv_wo, m_w_out, m_final_g, v_ada_w, v_ada_b, v_norm_g, v_ffn_wg, v_ffn_wu, v_ffn_wd, v_w_in, v_attn_wo, v_conv_w, v_conv_b, v_conv_ln_g, v_conv_ln_b, v_conv_wo, v_w_out, v_final_g):
    px, py, pc = _place()
    me = 4 * px + 2 * py + pc
    n_mod = ada_w.shape[2]
    big_w = dict(ffn_wg=ffn_wg, ffn_wu=ffn_wu, ffn_wd=ffn_wd, w_in=w_in, attn_wo=attn_wo, conv_wo=conv_wo, w_out=w_out)
    big_m = dict(ffn_wg=m_ffn_wg, ffn_wu=m_ffn_wu, ffn_wd=m_ffn_wd, w_in=m_w_in, attn_wo=m_attn_wo, conv_wo=m_conv_wo, w_out=m_w_out)
    big_v = dict(ffn_wg=v_ffn_wg, ffn_wu=v_ffn_wu, ffn_wd=v_ffn_wd, w_in=v_w_in, attn_wo=v_attn_wo, conv_wo=v_conv_wo, w_out=v_w_out)
    big_shapes = [big_w[k].shape for k in BIG]

    small_in = [c, norm_g, conv_w]
    g1 = _all_gather("gather_small", [_to_slab(small_in, F32)])[0]
    c_all, ng_blocks, cw_blocks = _from_slab(g1, [a.shape for a in small_in], lead=(N_DEV,))
    c_all = c_all.reshape(N_DEV, D)
    norm_g_full = _join(ng_blocks, 2)
    conv_w_full = _join(cw_blocks, 2)
    as2d = lambda a: a.reshape(-1, a.shape[-1])
    gw = _all_gather("gather_weights", [as2d(big_w[k]).astype(BF16) for k in BIG])
    full = {k: _join(blk.reshape(N_DEV, *big_w[k].shape), BIG[k]) for k, blk in zip(BIG, gw)}

    c_act = _rowwise("cond_silu", lambda rows, _v: ([rows[0] * _sigmoid(rows[0])], []), [c_all], [], [(D, BF16)])[0]
    c_pad = jnp.pad(c_act, ((0, LANES - N_DEV), (0, 0)))
    mod_cols = []
    for l in range(2):
        bias = lax.dynamic_slice_in_dim(ada_b[l], me * n_mod, n_mod)[None, :]
        out = _mm(f"mod{l}", [(c_pad, ada_w[l].astype(BF16))], lambda accs, _t, vs: [accs[0] + vs[0]], [F32], vecs=[bias])[0]
        mod_cols.append(out[:N_DEV])
    g2 = _all_gather("gather_mod", [_to_slab([jnp.stack(mod_cols)], F32)])[0]
    mod_all = _from_slab(g2, [(2, N_DEV, n_mod)], lead=(N_DEV,))[0]
    mod = lax.dynamic_index_in_dim(mod_all, me, axis=2, keepdims=False)
    mod = jnp.moveaxis(mod, 0, 1).reshape(2, 9, D)

    wts = []
    for l in range(2):
        w_in_l = full["w_in"][l]
        cw = jnp.pad(conv_w_full[l], ((0, HALO - CONV_K), (0, 0)))
        wts.append(dict(
            wg0=full["ffn_wg"][l, 0], wu0=full["ffn_wu"][l, 0], wd0=full["ffn_wd"][l, 0],
            wg1=full["ffn_wg"][l, 1], wu1=full["ffn_wu"][l, 1], wd1=full["ffn_wd"][l, 1],
            w_qkv=w_in_l[:, :3 * QKV_W], w_u=w_in_l[:, 3 * QKV_W:3 * QKV_W + 2 * D],
            w_ga=w_in_l[:, 3 * QKV_W + 2 * D:3 * QKV_W + 3 * D], w_gc=w_in_l[:, 3 * QKV_W + 3 * D:],
            attn_wo=full["attn_wo"][l], conv_wo=full["conv_wo"][l], w_out=full["w_out"][l],
            conv_w=cw, conv_b=conv_b[l][None, :], ln_g=conv_ln_g[l][None, :], ln_b=conv_ln_b[l][None, :]))

    loss, dx, dmod, small, big = _local_step(x[0], loss_target[0], mod, norm_g_full, wts, final_g)
    loss = lax.psum(loss, MESH_AXES)

    small_names = ["norm_g", "conv_w", "conv_b", "ln_g", "ln_b", "final_g"]
    small_parts = [dmod] + [small[k] for k in small_names]
    g3 = _all_gather("gather_small_grads", [_to_slab(small_parts, F32)])[0]
    tot = _sum_parts("sum_small_grads", [g3[k] for k in range(N_DEV)])
    _, g_ng, g_cw, g_cb, g_lg, g_lb, g_fg = _from_slab(tot, [a.shape for a in small_parts])
    g_ab = _from_slab(tot, [(2, 9 * D)])[0]
    dmod_all = _from_slab(g3, [dmod.shape], lead=(N_DEV,))[0].reshape(N_DEV, 2, 9 * D)
    dm_mine = lax.dynamic_slice_in_dim(dmod_all, me * n_mod, n_mod, axis=2)
    g_aw = jnp.stack([
        _mm_tn(f"dada_w{l}", c_pad, jnp.pad(dm_mine[:, l], ((0, LANES - N_DEV), (0, 0))).astype(BF16), out_dtype=F32)
        for l in range(2)])
    cols = lambda a: lax.dynamic_slice_in_dim(a, me * (D // N_DEV), D // N_DEV, axis=2)
    small_w = [ada_b, norm_g, conv_w, conv_b, conv_ln_g, conv_ln_b, final_g]
    small_m = [m_ada_b, m_norm_g, m_conv_w, m_conv_b, m_conv_ln_g, m_conv_ln_b, m_final_g]
    small_v = [v_ada_b, v_norm_g, v_conv_w, v_conv_b, v_conv_ln_g, v_conv_ln_b, v_final_g]
    small_g = [g_ab, cols(g_ng), cols(g_cw), g_cb, g_lg, g_lb, g_fg]
    s_shapes = [a.shape for a in small_w]
    s_out = _adamw("adamw_small", _to_slab(small_w, F32), [_to_slab(small_g, F32)], _to_slab(small_m, F32), _to_slab(small_v, F32))
    aw_out = [o.reshape(ada_w.shape) for o in _adamw("adamw_ada_w", as2d(ada_w), [as2d(g_aw)], as2d(m_ada_w), as2d(v_ada_w))]

    send = [_split(big[k], BIG[k]).reshape(N_DEV, -1, big_w[k].shape[-1]) for k in BIG]
    got = _exchange_blocks("exchange_grads", send)
    b_out = {}
    for k, blocks in zip(BIG, got):
        parts = [(blocks, blocks.shape[2], 0, j) for j in range(N_DEV)]
        outs = _adamw(f"adamw_{k}", as2d(big_w[k]), parts, as2d(big_m[k]), as2d(big_v[k]))
        b_out[k] = [o.reshape(big_w[k].shape) for o in outs]

    def ordered(i):
        ab, ng, cw, cb, lg, lb, fg = _from_slab(s_out[i], s_shapes)
        bg = {k: b_out[k][i] for k in BIG}
        return [aw_out[i], ab, ng, bg["ffn_wg"], bg["ffn_wu"], bg["ffn_wd"], bg["w_in"], bg["attn_wo"], cw, cb, lg, lb,
                bg["conv_wo"], bg["w_out"], fg]

    return (loss, dx[None], *ordered(0), *ordered(1), *ordered(2), *ordered(3))
```

```python
import functools

import jax
import jax.numpy as jnp
from jax import lax
from jax.experimental import pallas as pl
from jax.experimental.pallas import tpu as pltpu

F32 = jnp.float32
BF16 = jnp.bfloat16

N_DEV = 8
D = 1024
D_FF = 2816
HEAD_DIM = 64
GROUP_W = 256
ATTN_DILATIONS = (1, 4, 16)
BLK = 128
QKV_W = 768
CONV_K = 31
ROPE_THETA = 10000.0
EPS = 1e-6
NEG_INF = -1e30
ADAM_LR, ADAM_B1, ADAM_B2, ADAM_EPS, ADAM_WD, ADAM_STEP = 0.001, 0.9, 0.999, 1e-08, 0.01, 10

V7X_VMEM_BYTES = 64 * 1024 * 1024
VMEM_LIMIT = V7X_VMEM_BYTES - 8 * 1024 * 1024
LANES = 128
MESH_AXES = ("x", "y", "c")


def _params(n_grid):
    return pltpu.CompilerParams(vmem_limit_bytes=VMEM_LIMIT, dimension_semantics=("arbitrary",) * n_grid)


def _sigmoid(v):
    return 1.0 / (1.0 + jnp.exp(-v))


def _mm(name, prods, epilogue, out_dtypes, *, nt=False, tiles=(), vecs=(), tm=512, tn=None, a_pre=None, chunk=None,
        job=None):
    s = prods[0][0].shape[0]
    n = prods[0][1].shape[0] if nt else prods[0][1].shape[1]
    tn = n if tn is None else tn
    tm = min(tm, s)
    assert s % tm == 0 and n % tn == 0
    n_p, n_t, n_v = len(prods), len(tiles), len(vecs)
    dn = (((1,), (1,)), ((), ())) if nt else (((1,), (0,)), ((), ()))

    chunk = tn if chunk is None else chunk
    bounds = [(c0, min(chunk, tn - c0)) for c0 in range(0, tn, chunk)]

    n_o = len(out_dtypes)
    n_j = len(job["arrays"]) if job else 0
    n_steps = (n // tn, s // tm)

    def body(*refs):
        p_refs, rest = refs[:2 * n_p], refs[2 * n_p:]
        t_refs, v_refs, rest = rest[:n_t], rest[n_t:n_t + n_v], rest[n_t + n_v:]
        j_in, o_refs, j_out, sems = rest[:n_j], rest[n_j:n_j + n_o], rest[n_j + n_o:2 * n_j + n_o], rest[2 * n_j + n_o:]
        if job:
            @pl.when(jnp.logical_and(pl.program_id(0) == 0, pl.program_id(1) == 0))
            def _():
                job["run"]("start", j_in, j_out, *sems)
        lhs = []
        for p in range(n_p):
            a = p_refs[2 * p][...]
            lhs.append(a if a_pre is None else a_pre(a))
        for c0, cw in bounds:
            cols = pl.ds(c0, cw)
            accs = []
            for p in range(n_p):
                b = p_refs[2 * p + 1][cols, :] if nt else p_refs[2 * p + 1][:, cols]
                accs.append(lax.dot_general(lhs[p], b, dn, preferred_element_type=F32))
            outs = epilogue(accs, [t[:, cols] for t in t_refs], [v[:, cols] for v in v_refs])
            for o_ref, o in zip(o_refs, outs, strict=True):
                o_ref[:, cols] = o.astype(o_ref.dtype)
        if job:
            @pl.when(jnp.logical_and(pl.program_id(0) == n_steps[0] - 1, pl.program_id(1) == n_steps[1] - 1))
            def _():
                job["run"]("finish", j_in, j_out, *sems)

    in_specs = []
    operands = []
    for a, b in prods:
        k = a.shape[1]
        in_specs.append(pl.BlockSpec((tm, k), lambda j, i: (i, 0)))
        in_specs.append(pl.BlockSpec((tn, k), lambda j, i: (j, 0)) if nt else pl.BlockSpec((k, tn), lambda j, i: (0, j)))
        operands += [a, b]
    for arr, off in tiles:
        in_specs.append(pl.BlockSpec((tm, tn), functools.partial(lambda j, i, off: (i, j + off), off=off)))
        operands.append(arr)
    for v in vecs:
        in_specs.append(pl.BlockSpec((1, tn), lambda j, i: (0, j)))
        operands.append(v)
    out_specs = [pl.BlockSpec((tm, tn), lambda j, i: (i, j)) for _ in out_dtypes]
    out_shape = [jax.ShapeDtypeStruct((s, n), dt) for dt in out_dtypes]
    scratch = []
    if job:
        in_specs += [HBM_SPEC] * n_j
        operands += job["arrays"]
        out_specs += [HBM_SPEC] * n_j
        out_shape += job["out_shape"]
        scratch = _job_scratch(job)
    out = pl.pallas_call(
        body, name=name, grid=n_steps, in_specs=in_specs, out_specs=out_specs, out_shape=out_shape,
        scratch_shapes=scratch, compiler_params=_params(2),
    )(*operands)
    return (out[:n_o], out[n_o:]) if job else out


def _mm_tn(name, a, b, *, tk=512, tn=None, out_dtype=BF16):
    s, m = a.shape
    n = b.shape[1]
    tn = n if tn is None else tn
    tk = min(tk, s)
    n_k = s // tk
    assert s % tk == 0 and n % tn == 0

    def body(a_ref, b_ref, o_ref, acc_ref):
        k = pl.program_id(1)

        @pl.when(k == 0)
        def _():
            acc_ref[...] = jnp.zeros_like(acc_ref)

        acc_ref[...] += lax.dot_general(a_ref[...], b_ref[...], (((0,), (0,)), ((), ())), preferred_element_type=F32)

        @pl.when(k == n_k - 1)
        def _():
            o_ref[...] = acc_ref[...].astype(o_ref.dtype)

    return pl.pallas_call(
        body, name=name, grid=(n // tn, n_k),
        in_specs=[pl.BlockSpec((tk, m), lambda j, k: (k, 0)), pl.BlockSpec((tk, tn), lambda j, k: (k, j))],
        out_specs=pl.BlockSpec((m, tn), lambda j, k: (0, j)),
        out_shape=jax.ShapeDtypeStruct((m, n), out_dtype),
        scratch_shapes=[pltpu.VMEM((m, tn), F32)],
        compiler_params=_params(2),
    )(a, b)


def _pick_rows(s, target):
    if s <= target:
        return s
    return max(t for t in range(16, target + 1, 16) if s % t == 0)


def _rowwise(name, body, rows, vecs, outs, accs=(), *, ts=512):
    rows = [(r if isinstance(r, tuple) else (r, r.shape[1], 0)) for r in rows]
    rows = [r if len(r) == 4 else (*r, None) for r in rows]
    s = rows[0][0].shape[0]
    ts = _pick_rows(s, ts)
    n_r, n_v, n_o, n_a = len(rows), len(vecs), len(outs), len(accs)

    def kbody(*refs):
        r_refs, v_refs = refs[:n_r], refs[n_r:n_r + n_v]
        o_refs, a_refs = refs[n_r + n_v:n_r + n_v + n_o], refs[n_r + n_v + n_o:]
        res_o, res_a = body([r[...] for r in r_refs], [v[...] for v in v_refs])
        for o_ref, o in zip(o_refs, res_o, strict=True):
            o_ref[...] = o.astype(o_ref.dtype)
        if n_a:
            first = pl.program_id(0) == 0

            @pl.when(first)
            def _():
                for a_ref, a in zip(a_refs, res_a, strict=True):
                    a_ref[...] = a

            @pl.when(jnp.logical_not(first))
            def _():
                for a_ref, a in zip(a_refs, res_a, strict=True):
                    a_ref[...] += a

    in_specs = [
        pl.BlockSpec((ts, w), functools.partial(lambda i, cb: (i, cb), cb=cb)) if lead is None else
        pl.BlockSpec((None, ts, w), functools.partial(lambda i, cb, lead: (lead, i, cb), cb=cb, lead=lead))
        for _, w, cb, lead in rows]
    in_specs += [pl.BlockSpec(v.shape, functools.partial(lambda i, nd: (0,) * nd, nd=v.ndim)) for v in vecs]
    out_specs = [pl.BlockSpec((ts, w), lambda i: (i, 0)) for w, _ in outs]
    out_specs += [pl.BlockSpec(shp, functools.partial(lambda i, nd: (0,) * nd, nd=len(shp))) for shp in accs]
    out_shape = [jax.ShapeDtypeStruct((s, w), dt) for w, dt in outs] + [jax.ShapeDtypeStruct(shp, F32) for shp in accs]
    return pl.pallas_call(
        kbody, name=name, grid=(s // ts,), in_specs=in_specs, out_specs=out_specs, out_shape=out_shape,
        compiler_params=_params(1),
    )(*[r[0] for r in rows], *vecs)


def _colsum(v):
    return jnp.sum(v, axis=0, keepdims=True)


def _rms_parts(x):
    r = lax.rsqrt(jnp.mean(x * x, axis=-1, keepdims=True) + EPS)
    return x * r, r


def _rms_bwd(dxhat, xhat, r):
    return r * (dxhat - xhat * jnp.mean(dxhat * xhat, axis=-1, keepdims=True))


def _modulate(name, x, ng, sc, sh):
    def body(rows, vecs):
        (xv,), (g, s_, b) = rows, vecs
        xhat, _ = _rms_parts(xv)
        return [xhat * g * (1.0 + s_) + b], []

    return _rowwise(name, body, [x], [ng, sc, sh], [(D, BF16)])[0]


def _modulate_bwd(name, dh, x, dxo, ng, sc):
    def body(rows, vecs):
        (dhv, xv, dxov), (g, s_) = rows, vecs
        xhat, r = _rms_parts(xv)
        dn = dhv * (1.0 + s_)
        dx = _rms_bwd(dn * g, xhat, r)
        return [dxov + dx], [_colsum(dhv), _colsum(dhv * xhat * g), _colsum(dn * xhat)]

    return _rowwise(name, body, [dh, x, dxo], [ng, sc], [(D, F32)], [(1, D)] * 3)


def _gate_scale(name, dx, f, gate, coef):
    def body(rows, vecs):
        (dxv, fv), (g,) = rows, vecs
        return [coef * g * dxv], [_colsum(coef * dxv * fv.astype(F32))]

    return _rowwise(name, body, [dx, f], [gate], [(D, BF16)], [(1, D)])


def _with_job(result, job):
    return result if job else (result, None)


def _ffn_fwd(tag, x, ng, sc, sh, gate, wg, wu, wd, job=None):
    h = _modulate(f"{tag}_mod", x, ng, sc, sh)

    def up_epi(accs, _t, _v):
        a, u = accs
        return [a, u, a * _sigmoid(a) * u]

    (a, u, t), got = _with_job(_mm(f"{tag}_up", [(h, wg), (h, wu)], up_epi, [BF16] * 3, tn=D_FF // 2, job=job), job)

    def down_epi(accs, tl, vs):
        return [tl[0] + 0.5 * vs[0] * accs[0], accs[0]]

    x_out, f = _mm(f"{tag}_down", [(t, wd)], down_epi, [F32, BF16], tiles=[(x, 0)], vecs=[gate])
    return x_out, (x, h, a, u, f), got


def _ffn_bwd(tag, dxo, saved, ng, sc, gate, wg, wu, wd, job=None):
    x, h, a, u, f = saved
    df, dgate = _gate_scale(f"{tag}_gs", dxo, f, gate, 0.5)

    def epi(accs, tl, _v):
        dt = accs[0]
        av, uv = tl[0].astype(F32), tl[1].astype(F32)
        sg = _sigmoid(av)
        sil = av * sg
        return [dt * uv * (sg * (1.0 + av * (1.0 - sg))), dt * sil, sil * uv]

    (da, du, t), got = _with_job(_mm(f"{tag}_bdown", [(df, wd)], epi, [BF16] * 3, nt=True, tiles=[(a, 0), (u, 0)],
                                     tn=D_FF // 2, chunk=3 * LANES, job=job), job)
    dwd = _mm_tn(f"{tag}_dwd", t, df)
    dwg = _mm_tn(f"{tag}_dwg", h, da)
    dwu = _mm_tn(f"{tag}_dwu", h, du)
    dh = _mm(f"{tag}_bup", [(da, wg), (du, wu)], lambda accs, _t, _v: [accs[0] + accs[1]], [F32], nt=True)[0]
    dx_in, dsh, dsc, dng = _modulate_bwd(f"{tag}_modb", dh, x, dxo, ng, sc)
    return dx_in, (dsh, dsc, dgate, dng), (dwg, dwu, dwd), got


def _rope_tables(s):
    half = HEAD_DIM // 2
    inv_freq = ROPE_THETA ** (-(jnp.arange(half, dtype=F32) * 2.0 / HEAD_DIM))
    ang = jnp.arange(s, dtype=F32)[:, None] * inv_freq[None, :]
    cos, sin = jnp.cos(ang), jnp.sin(ang)
    return jnp.tile(jnp.concatenate([cos, cos], axis=1), (1, 2)), jnp.tile(jnp.concatenate([-sin, sin], axis=1), (1, 2))


def _rotate(v, cos2, sin2, sign):
    w = v.shape[1]
    lane = lax.broadcasted_iota(jnp.int32, v.shape, 1)
    partner = jnp.where(lane % HEAD_DIM < HEAD_DIM // 2, pltpu.roll(v, w - HEAD_DIM // 2, 1), pltpu.roll(v, HEAD_DIM // 2, 1))
    reps = w // LANES
    return v * jnp.tile(cos2, (1, reps)) + partner * (sign * jnp.tile(sin2, (1, reps)))


SPLIT_TS = 512


def _dilated_spec(dil, ts):
    return pl.BlockSpec((dil, ts // dil, GROUP_W), lambda i: (0, i, 0))


def _dilated_shape(s, dil, dtype):
    return jax.ShapeDtypeStruct((dil, s // dil, GROUP_W), dtype)


CHUNKS_PER_GROUP = GROUP_W // LANES


def _put(buf, chunk0, val):
    for c in range(val.shape[1] // LANES):
        buf[chunk0 + c] = val[:, c * LANES:(c + 1) * LANES]


def _get(buf, chunk0, n):
    return jnp.concatenate([buf[chunk0 + c] for c in range(n)], axis=1)


def _strided_rows(r, dil, ts):
    return pl.ds(r, ts // dil, stride=dil) if dil > 1 else pl.ds(0, ts)


def _deinterleave_one(buf, chunk0, out_ref, dil, ts):
    for half in range(CHUNKS_PER_GROUP):
        for r in range(dil):
            src = buf.at[chunk0 + half][_strided_rows(r, dil, ts), :]
            out_ref.at[r][:, pl.ds(half * LANES, LANES)] = src.astype(out_ref.dtype)


def _interleave_one(in_ref, buf, chunk0, dil, ts):
    for half in range(CHUNKS_PER_GROUP):
        for r in range(dil):
            src = in_ref.at[r][:, pl.ds(half * LANES, LANES)]
            buf.at[chunk0 + half][_strided_rows(r, dil, ts), :] = src.astype(F32)


def _deinterleave(buf, chunk0, out_refs, ts):
    for g, dil in enumerate(ATTN_DILATIONS):
        _deinterleave_one(buf, chunk0 + g * CHUNKS_PER_GROUP, out_refs[g], dil, ts)


def _interleave(in_refs, buf, chunk0, ts):
    for g, dil in enumerate(ATTN_DILATIONS):
        _interleave_one(in_refs[g], buf, chunk0 + g * CHUNKS_PER_GROUP, dil, ts)


def _rope_split(name, qkv, cos2, sin2):
    s, ts = qkv.shape[0], SPLIT_TS

    def body(qkv_ref, c_ref, s_ref, *rest):
        outs, buf = rest[:9], rest[9]
        c2, s2 = c_ref[...], s_ref[...]
        per = QKV_W // LANES
        _put(buf, 0, _rotate(qkv_ref[:, pl.ds(0, QKV_W)].astype(F32), c2, s2, 1.0))
        _put(buf, per, _rotate(qkv_ref[:, pl.ds(QKV_W, QKV_W)].astype(F32), c2, s2, 1.0))
        _put(buf, 2 * per, qkv_ref[:, pl.ds(2 * QKV_W, QKV_W)].astype(F32))
        for t in range(3):
            _deinterleave(buf, t * per, outs[3 * t:3 * t + 3], ts)

    tab = pl.BlockSpec((ts, LANES), lambda i: (i, 0))
    outs = pl.pallas_call(
        body, name=name, grid=(s // ts,), in_specs=[pl.BlockSpec((ts, 3 * QKV_W), lambda i: (i, 0)), tab, tab],
        out_specs=[_dilated_spec(d, ts) for _ in range(3) for d in ATTN_DILATIONS],
        out_shape=[_dilated_shape(s, d, BF16) for _ in range(3) for d in ATTN_DILATIONS],
        scratch_shapes=[pltpu.VMEM((3 * QKV_W // LANES, ts, LANES), F32)], compiler_params=_params(1),
    )(qkv, cos2, sin2)
    return outs[0:3], outs[3:6], outs[6:9]


def _rope_join(name, dq, dk, dv, cos2, sin2):
    s, ts = cos2.shape[0], SPLIT_TS

    def body(*refs):
        ins, c_ref, s_ref, o_ref, buf = refs[:9], refs[9], refs[10], refs[11], refs[12]
        per = QKV_W // LANES
        for t in range(3):
            _interleave(ins[3 * t:3 * t + 3], buf, t * per, ts)
        c2, s2 = c_ref[...], s_ref[...]
        o_ref[:, pl.ds(0, QKV_W)] = _rotate(_get(buf, 0, per), c2, s2, -1.0).astype(o_ref.dtype)
        o_ref[:, pl.ds(QKV_W, QKV_W)] = _rotate(_get(buf, per, per), c2, s2, -1.0).astype(o_ref.dtype)
        o_ref[:, pl.ds(2 * QKV_W, QKV_W)] = _get(buf, 2 * per, per).astype(o_ref.dtype)

    tab = pl.BlockSpec((ts, LANES), lambda i: (i, 0))
    return pl.pallas_call(
        body, name=name, grid=(s // ts,),
        in_specs=[_dilated_spec(d, ts) for _ in range(3) for d in ATTN_DILATIONS] + [tab, tab],
        out_specs=pl.BlockSpec((ts, 3 * QKV_W), lambda i: (i, 0)), out_shape=jax.ShapeDtypeStruct((s, 3 * QKV_W), BF16),
        scratch_shapes=[pltpu.VMEM((3 * QKV_W // LANES, ts, LANES), F32)], compiler_params=_params(1),
    )(*dq, *dk, *dv, cos2, sin2)


def _head_masks(shape):
    lane = lax.broadcasted_iota(jnp.int32, shape, 1)
    return [jnp.logical_and(lane >= h * HEAD_DIM, lane < (h + 1) * HEAD_DIM) for h in range(GROUP_W // HEAD_DIM)]


def _grad_split(name, do, o, lse):
    s, ts = do.shape[0], SPLIT_TS

    def body(do_ref, o_ref, l_ref, *rest):
        outs, buf = rest[:9], rest[9]
        dov = do_ref[...].astype(F32)
        prod = dov * o_ref[...].astype(F32)
        delta = jnp.zeros_like(prod)
        for hm in _head_masks(prod.shape):
            delta = jnp.where(hm, jnp.sum(jnp.where(hm, prod, 0.0), axis=1, keepdims=True), delta)
        _put(buf, 0, dov)
        _put(buf, CHUNKS_PER_GROUP, delta)
        _put(buf, 2 * CHUNKS_PER_GROUP, l_ref[...])
        for t in range(3):
            for g in range(3):
                _deinterleave_one(buf, t * CHUNKS_PER_GROUP, outs[3 * t + g], ATTN_DILATIONS[g], ts)

    nat = pl.BlockSpec((ts, GROUP_W), lambda i: (i, 0))
    dts = [BF16, F32, F32]
    outs = pl.pallas_call(
        body, name=name, grid=(s // ts,), in_specs=[nat, nat, nat],
        out_specs=[_dilated_spec(d, ts) for _ in range(3) for d in ATTN_DILATIONS],
        out_shape=[_dilated_shape(s, d, dt) for dt in dts for d in ATTN_DILATIONS],
        scratch_shapes=[pltpu.VMEM((3 * CHUNKS_PER_GROUP, ts, LANES), F32)], compiler_params=_params(1),
    )(do, o, lse)
    return outs[0:3], outs[3:6], outs[6:9]


def _band_masks(has_prev):
    qi = lax.broadcasted_iota(jnp.int32, (BLK, BLK), 0)
    kj = lax.broadcasted_iota(jnp.int32, (BLK, BLK), 1)
    return kj <= qi, jnp.logical_and(kj >= qi, has_prev)


def _dot_nt(a, b):
    return lax.dot_general(a, b, (((1,), (1,)), ((), ())), preferred_element_type=F32)


def _dot_tn(a, b):
    return lax.dot_general(a, b, (((0,), (0,)), ((), ())), preferred_element_type=F32)


def _dot(a, b):
    return jnp.dot(a, b, preferred_element_type=F32)


ATTN_BLK = (None, BLK, GROUP_W)


def _attn_specs(clamp):
    cur = pl.BlockSpec(ATTN_BLK, lambda r, n: (r, clamp(n), 0))
    prev = pl.BlockSpec(ATTN_BLK, lambda r, n: (r, jnp.maximum(clamp(n) - 1, 0), 0))
    return [cur, cur, prev, cur, prev]


def _attn_fwd(name, q, k, v):
    dil, rows, _ = q.shape
    nb = rows // BLK
    scale = HEAD_DIM ** -0.5

    def body(q_ref, kc_ref, kp_ref, vc_ref, vp_ref, o_ref, l_ref):
        mask_c, mask_p = _band_masks(pl.program_id(1) > 0)
        q, kc, kp, vc, vp = q_ref[...], kc_ref[...], kp_ref[...], vc_ref[...], vp_ref[...]
        o_acc = jnp.zeros((BLK, GROUP_W), F32)
        l_acc = jnp.zeros((BLK, GROUP_W), F32)
        for hm in _head_masks((BLK, GROUP_W)):
            qm = jnp.where(hm, q, jnp.zeros_like(q))
            sc = jnp.where(mask_c, _dot_nt(qm, kc) * scale, NEG_INF)
            sp = jnp.where(mask_p, _dot_nt(qm, kp) * scale, NEG_INF)
            m = jnp.maximum(jnp.max(sc, axis=1, keepdims=True), jnp.max(sp, axis=1, keepdims=True))
            pc, pp = jnp.exp(sc - m), jnp.exp(sp - m)
            den = jnp.sum(pc, axis=1, keepdims=True) + jnp.sum(pp, axis=1, keepdims=True)
            oh = (_dot(pc.astype(BF16), vc) + _dot(pp.astype(BF16), vp)) / den
            o_acc = jnp.where(hm, oh, o_acc)
            l_acc = jnp.where(hm, m + jnp.log(den), l_acc)
        o_ref[...] = o_acc
        l_ref[...] = l_acc

    out_spec = pl.BlockSpec(ATTN_BLK, lambda r, n: (r, n, 0))
    return pl.pallas_call(
        body, name=name, grid=(dil, nb), in_specs=_attn_specs(lambda n: n), out_specs=[out_spec, out_spec],
        out_shape=[jax.ShapeDtypeStruct(q.shape, F32)] * 2, compiler_params=_params(2),
    )(q, k, k, v, v)


def _attn_merge(name, os_, ls_):
    s, ts = os_[0].shape[0] * os_[0].shape[1], SPLIT_TS

    def body(*refs):
        o_refs, l_refs, o_ref, l_ref, buf = refs[0:3], refs[3:6], refs[6], refs[7], refs[8]
        _interleave(o_refs, buf, 0, ts)
        _interleave(l_refs, buf, 3 * CHUNKS_PER_GROUP, ts)
        o0, o1, o2 = [_get(buf, g * CHUNKS_PER_GROUP, CHUNKS_PER_GROUP) for g in range(3)]
        l0, l1, l2 = [_get(buf, (3 + g) * CHUNKS_PER_GROUP, CHUNKS_PER_GROUP) for g in range(3)]
        m = jnp.maximum(jnp.maximum(l0, l1), l2)
        e0, e1, e2 = jnp.exp(l0 - m), jnp.exp(l1 - m), jnp.exp(l2 - m)
        tot = e0 + e1 + e2
        o_ref[...] = ((e0 * o0 + e1 * o1 + e2 * o2) / tot).astype(o_ref.dtype)
        l_ref[...] = m + jnp.log(tot)

    nat = pl.BlockSpec((ts, GROUP_W), lambda i: (i, 0))
    return pl.pallas_call(
        body, name=name, grid=(s // ts,), in_specs=[_dilated_spec(d, ts) for _ in range(2) for d in ATTN_DILATIONS],
        out_specs=[nat, nat], out_shape=[jax.ShapeDtypeStruct((s, GROUP_W), BF16), jax.ShapeDtypeStruct((s, GROUP_W), F32)],
        scratch_shapes=[pltpu.VMEM((6 * CHUNKS_PER_GROUP, ts, LANES), F32)], compiler_params=_params(1),
    )(*os_, *ls_)


def _attn_bwd(name, q, k, v, do, delta, lse):
    dil, rows, _ = q.shape
    nb = rows // BLK
    scale = HEAD_DIM ** -0.5

    def body(q_ref, kc_ref, kp_ref, vc_ref, vp_ref, do_ref, dl_ref, l_ref, dq_ref, dk_ref, dv_ref, ck_ref, cv_ref):
        n = pl.program_id(1)

        @pl.when(n == 0)
        def _():
            ck_ref[...] = jnp.zeros_like(ck_ref)
            cv_ref[...] = jnp.zeros_like(cv_ref)

        @pl.when(n < nb)
        def _():
            mask_c, mask_p = _band_masks(n > 0)
            q, kc, kp, vc, vp, dov = q_ref[...], kc_ref[...], kp_ref[...], vc_ref[...], vp_ref[...], do_ref[...]
            lb, db = l_ref[...], dl_ref[...]
            zero = jnp.zeros((BLK, GROUP_W), F32)
            dq_acc, dkc, dkp, dvc, dvp = zero, zero, zero, zero, zero
            for hm in _head_masks((BLK, GROUP_W)):
                qm = jnp.where(hm, q, jnp.zeros_like(q))
                dom = jnp.where(hm, dov, jnp.zeros_like(dov))
                lh = jnp.max(jnp.where(hm, lb, NEG_INF), axis=1, keepdims=True)
                delta = jnp.max(jnp.where(hm, db, NEG_INF), axis=1, keepdims=True)
                pc = jnp.exp(jnp.where(mask_c, _dot_nt(qm, kc) * scale, NEG_INF) - lh)
                pp = jnp.exp(jnp.where(mask_p, _dot_nt(qm, kp) * scale, NEG_INF) - lh)
                dsc = (pc * (_dot_nt(dom, vc) - delta) * scale).astype(BF16)
                dsp = (pp * (_dot_nt(dom, vp) - delta) * scale).astype(BF16)
                dq_acc = jnp.where(hm, _dot(dsc, kc) + _dot(dsp, kp), dq_acc)
                dkc += _dot_tn(dsc, qm)
                dkp += _dot_tn(dsp, qm)
                dvc += _dot_tn(pc.astype(BF16), dom)
                dvp += _dot_tn(pp.astype(BF16), dom)
            dq_ref[...] = dq_acc.astype(dq_ref.dtype)
            dk_ref[...] = (ck_ref[...] + dkp).astype(dk_ref.dtype)
            dv_ref[...] = (cv_ref[...] + dvp).astype(dv_ref.dtype)
            ck_ref[...] = dkc
            cv_ref[...] = dvc

        @pl.when(n == nb)
        def _():
            dk_ref[...] = ck_ref[...].astype(dk_ref.dtype)
            dv_ref[...] = cv_ref[...].astype(dv_ref.dtype)

    clamp = lambda n: jnp.minimum(n, nb - 1)
    qspec = pl.BlockSpec(ATTN_BLK, lambda r, n: (r, clamp(n), 0))
    kspec = pl.BlockSpec(ATTN_BLK, lambda r, n: (r, jnp.maximum(n - 1, 0), 0))
    return pl.pallas_call(
        body, name=name, grid=(dil, nb + 1), in_specs=_attn_specs(clamp) + [qspec, qspec, qspec],
        out_specs=[qspec, kspec, kspec], out_shape=[jax.ShapeDtypeStruct(q.shape, BF16)] * 3,
        scratch_shapes=[pltpu.VMEM((BLK, GROUP_W), F32), pltpu.VMEM((BLK, GROUP_W), F32)], compiler_params=_params(2),
    )(q, k, k, v, v, do, delta, lse)


CONV_TS = 128
HALO = 32
SHIFT_ROWS = CONV_TS + HALO - 8


def _make_shifts(buf, sh):
    for s_ in range(1, 8):
        sh[s_ - 1] = buf[pl.ds(s_, SHIFT_ROWS), :]


def _window(buf, sh, off, ts, cols):
    q, s_ = divmod(off, 8)
    if s_ == 0:
        return buf[pl.ds(off, ts), cols]
    return sh[s_ - 1, pl.ds(8 * q, ts), cols]


def _conv_fwd(name, u, w, b, lg, lb):
    s = u.shape[0]
    ts, per = CONV_TS, CONV_TS // HALO

    def body(a_ref, g_ref, ap_ref, gp_ref, w_ref, b_ref, lg_ref, lb_ref, c_ref, act_ref, buf, cbuf, sh):
        i = pl.program_id(0)
        buf[pl.ds(HALO, ts), :] = a_ref[...].astype(F32) * _sigmoid(g_ref[...].astype(F32))
        prev = ap_ref[...].astype(F32) * _sigmoid(gp_ref[...].astype(F32))
        buf[pl.ds(0, HALO), :] = jnp.where(i > 0, prev, 0.0)
        _make_shifts(buf, sh)
        rb = ts // 2
        for lc in range(D // LANES):
            cols = pl.ds(lc * LANES, LANES)
            for r0 in range(0, ts, rb):
                acc = jnp.broadcast_to(b_ref[:, cols], (rb, LANES))
                for j in range(CONV_K):
                    acc = acc + w_ref[pl.ds(j, 1), cols] * _window(buf, sh, r0 + HALO - (CONV_K - 1) + j, rb, cols)
                cbuf[pl.ds(r0, rb), cols] = acc
        c = cbuf[...]
        mu = jnp.mean(c, axis=-1, keepdims=True)
        xc = c - mu
        ln = xc * lax.rsqrt(jnp.mean(xc * xc, axis=-1, keepdims=True) + EPS) * lg_ref[...] + lb_ref[...]
        c_ref[...] = c.astype(c_ref.dtype)
        act_ref[...] = (ln * _sigmoid(ln)).astype(act_ref.dtype)

    halo = lambda cb: pl.BlockSpec((HALO, D), functools.partial(lambda i, cb: (jnp.maximum(i * per - 1, 0), cb), cb=cb))
    vec = pl.BlockSpec((1, D), lambda i: (0, 0))
    return pl.pallas_call(
        body, name=name, grid=(s // ts,),
        in_specs=[pl.BlockSpec((ts, D), lambda i: (i, 0)), pl.BlockSpec((ts, D), lambda i: (i, 1)), halo(0), halo(1),
                  pl.BlockSpec((HALO, D), lambda i: (0, 0)), vec, vec, vec],
        out_specs=[pl.BlockSpec((ts, D), lambda i: (i, 0))] * 2,
        out_shape=[jax.ShapeDtypeStruct((s, D), BF16)] * 2,
        scratch_shapes=[pltpu.VMEM((ts + HALO, D), F32), pltpu.VMEM((ts, D), F32), pltpu.VMEM((7, SHIFT_ROWS, D), F32)],
        compiler_params=_params(1),
    )(u, u, u, u, w, b, lg, lb)


def _ln_bwd(name, dact, c, lg, lb):
    def body(rows, vecs):
        (dv, cv), (g, b) = rows, vecs
        cv = cv.astype(F32)
        mu = jnp.mean(cv, axis=-1, keepdims=True)
        xc = cv - mu
        rstd = lax.rsqrt(jnp.mean(xc * xc, axis=-1, keepdims=True) + EPS)
        xh = xc * rstd
        ln = xh * g + b
        sg = _sigmoid(ln)
        dln = dv * (sg * (1.0 + ln * (1.0 - sg)))
        dxh = dln * g
        dc = rstd * (dxh - jnp.mean(dxh, axis=-1, keepdims=True) - xh * jnp.mean(dxh * xh, axis=-1, keepdims=True))
        return [dc], [_colsum(dln * xh), _colsum(dln)]

    return _rowwise(name, body, [dact, c], [lg, lb], [(D, F32)], [(1, D), (1, D)])


def _conv_bwd(name, dc, u, w):
    s = u.shape[0]
    ts, per = CONV_TS, CONV_TS // HALO
    n_t = s // ts

    def body(dc_ref, dn_ref, a_ref, g_ref, ap_ref, gp_ref, w_ref, du_ref, dw_ref, db_ref, buf, dbuf, hbuf, sh, dsh):
        i = pl.program_id(0)
        a = a_ref[...].astype(F32)
        sg = _sigmoid(g_ref[...].astype(F32))
        buf[pl.ds(HALO, ts), :] = a * sg
        prev = ap_ref[...].astype(F32) * _sigmoid(gp_ref[...].astype(F32))
        buf[pl.ds(0, HALO), :] = jnp.where(i > 0, prev, 0.0)
        dcv = dc_ref[...]
        dbuf[pl.ds(0, ts), :] = dcv
        dbuf[pl.ds(ts, HALO), :] = jnp.where(i < n_t - 1, dn_ref[...], 0.0)

        @pl.when(i == 0)
        def _():
            dw_ref[...] = jnp.zeros_like(dw_ref)
            db_ref[...] = jnp.zeros_like(db_ref)

        db_ref[...] += _colsum(dcv)
        _make_shifts(buf, sh)
        _make_shifts(dbuf, dsh)
        rb = ts // 4
        for lc in range(D // LANES):
            cols = pl.ds(lc * LANES, LANES)
            for r0 in range(0, ts, rb):
                d0 = dbuf[pl.ds(r0, rb), cols]
                acc = jnp.zeros((rb, LANES), F32)
                for j in range(CONV_K):
                    acc = acc + w_ref[pl.ds(j, 1), cols] * _window(dbuf, dsh, r0 + CONV_K - 1 - j, rb, cols)
                    part = d0 * _window(buf, sh, r0 + HALO - (CONV_K - 1) + j, rb, cols)
                    dw_ref[pl.ds(8 * j, 8), cols] += jnp.sum(part.reshape(rb // 8, 8, LANES), axis=0)
                hbuf[pl.ds(r0, rb), cols] = acc
        dh = hbuf[...]
        du_ref[:, pl.ds(0, D)] = (dh * sg).astype(du_ref.dtype)
        du_ref[:, pl.ds(D, D)] = (dh * a * sg * (1.0 - sg)).astype(du_ref.dtype)

    halo = lambda cb: pl.BlockSpec((HALO, D), functools.partial(lambda i, cb: (jnp.maximum(i * per - 1, 0), cb), cb=cb))
    nxt = pl.BlockSpec((HALO, D), lambda i: (jnp.minimum((i + 1) * per, s // HALO - 1), 0))
    return pl.pallas_call(
        body, name=name, grid=(n_t,),
        in_specs=[pl.BlockSpec((ts, D), lambda i: (i, 0)), nxt, pl.BlockSpec((ts, D), lambda i: (i, 0)),
                  pl.BlockSpec((ts, D), lambda i: (i, 1)), halo(0), halo(1), pl.BlockSpec((HALO, D), lambda i: (0, 0))],
        out_specs=[pl.BlockSpec((ts, 2 * D), lambda i: (i, 0)), pl.BlockSpec((8 * CONV_K, D), lambda i: (0, 0)),
                   pl.BlockSpec((1, D), lambda i: (0, 0))],
        out_shape=[jax.ShapeDtypeStruct((s, 2 * D), BF16), jax.ShapeDtypeStruct((8 * CONV_K, D), F32),
                   jax.ShapeDtypeStruct((1, D), F32)],
        scratch_shapes=[pltpu.VMEM((ts + HALO, D), F32), pltpu.VMEM((ts + HALO, D), F32), pltpu.VMEM((ts, D), F32),
                        pltpu.VMEM((7, SHIFT_ROWS, D), F32), pltpu.VMEM((7, SHIFT_ROWS, D), F32)],
        compiler_params=_params(1),
    )(dc, dc, u, u, u, u, w)


def _mix_fwd(tag, x1, ng, sc, sh, gate, wts, cos2, sin2, job=None):
    h1 = _modulate(f"{tag}_mod", x1, ng, sc, sh)
    ident = lambda accs, _t, _v: accs
    (qkv,), got = _with_job(_mm(f"{tag}_qkv", [(h1, wts["w_qkv"])], ident, [BF16], tn=QKV_W, job=job), job)
    u = _mm(f"{tag}_u", [(h1, wts["w_u"])], ident, [BF16], tn=D)[0]
    ga, gc = _mm(f"{tag}_gates", [(h1, wts["w_ga"]), (h1, wts["w_gc"])], ident, [BF16] * 2)
    qd, kd, vd = _rope_split(f"{tag}_rope", qkv, cos2, sin2)
    per_group = [_attn_fwd(f"{tag}_attn{g}", qd[g], kd[g], vd[g]) for g in range(len(ATTN_DILATIONS))]
    o, lse = _attn_merge(f"{tag}_merge", [p[0] for p in per_group], [p[1] for p in per_group])
    cpre, act = _conv_fwd(f"{tag}_conv", u, wts["conv_w"], wts["conv_b"], wts["ln_g"], wts["ln_b"])

    def gate_epi(accs, tl, _v):
        ya, yc = accs
        return [_sigmoid(tl[0].astype(F32)) * ya + _sigmoid(tl[1].astype(F32)) * yc, ya, yc]

    y, ya, yc = _mm(f"{tag}_branch", [(o, wts["attn_wo"]), (act, wts["conv_wo"])], gate_epi, [BF16] * 3,
                    tiles=[(ga, 0), (gc, 0)])

    def res_epi(accs, tl, vs):
        return [tl[0] + vs[0] * accs[0], accs[0]]

    x2, f1 = _mm(f"{tag}_out", [(y, wts["w_out"])], res_epi, [F32, BF16], tiles=[(x1, 0)], vecs=[gate])
    return x2, (x1, h1, u, ga, gc, qd, kd, vd, o, lse, cpre, act, y, ya, yc, f1), got


def _mix_bwd(tag, dx2, saved, ng, sc, gate, wts, cos2, sin2, job=None):
    x1, h1, u, ga, gc, qd, kd, vd, o, lse, cpre, act, y, ya, yc, f1 = saved
    dfm, dgate = _gate_scale(f"{tag}_gs", dx2, f1, gate, 1.0)

    def epi(accs, tl, _v):
        dy = accs[0]
        sa, sc_ = _sigmoid(tl[0].astype(F32)), _sigmoid(tl[1].astype(F32))
        return [dy * sa, dy * sc_, dy * tl[2].astype(F32) * sa * (1.0 - sa), dy * tl[3].astype(F32) * sc_ * (1.0 - sc_)]

    dya, dyc, dga, dgc = _mm(f"{tag}_bout", [(dfm, wts["w_out"])], epi, [BF16] * 4, nt=True,
                             tiles=[(ga, 0), (gc, 0), (ya, 0), (yc, 0)], chunk=2 * LANES)
    grads = {"w_out": _mm_tn(f"{tag}_dwout", y, dfm), "attn_wo": _mm_tn(f"{tag}_dwattn", o, dya),
             "conv_wo": _mm_tn(f"{tag}_dwconv", act, dyc)}
    ident = lambda accs, _t, _v: accs
    do = _mm(f"{tag}_battn", [(dya, wts["attn_wo"])], ident, [BF16], nt=True)[0]
    dact = _mm(f"{tag}_bconv", [(dyc, wts["conv_wo"])], ident, [F32], nt=True)[0]
    dc, dlg, dlb = _ln_bwd(f"{tag}_lnb", dact, cpre, wts["ln_g"], wts["ln_b"])
    du, dw8, dcb = _conv_bwd(f"{tag}_convb", dc, u, wts["conv_w"])
    dod, deltad, lsed = _grad_split(f"{tag}_gsplit", do, o, lse)
    dqs, dks, dvs = [], [], []
    for g in range(len(ATTN_DILATIONS)):
        dq, dk, dv = _attn_bwd(f"{tag}_attnb{g}", qd[g], kd[g], vd[g], dod[g], deltad[g], lsed[g])
        dqs.append(dq); dks.append(dk); dvs.append(dv)
    dqkv = _rope_join(f"{tag}_ropeb", dqs, dks, dvs, cos2, sin2)
    grads["w_in"] = jnp.concatenate([_mm_tn(f"{tag}_dwqkv", h1, dqkv), _mm_tn(f"{tag}_dwu", h1, du),
                                     _mm_tn(f"{tag}_dwga", h1, dga), _mm_tn(f"{tag}_dwgc", h1, dgc)], axis=1)
    (dh1,), got = _with_job(
        _mm(f"{tag}_bin", [(dqkv, wts["w_qkv"]), (du, wts["w_u"]), (dga, wts["w_ga"]), (dgc, wts["w_gc"])],
            lambda accs, _t, _v: [accs[0] + accs[1] + accs[2] + accs[3]], [F32], nt=True, job=job), job)
    dx1, dsh, dsc, dng = _modulate_bwd(f"{tag}_modb", dh1, x1, dx2, ng, sc)
    small = {"conv_w": dw8.reshape(CONV_K, 8, D).sum(axis=1), "conv_b": dcb, "ln_g": dlg, "ln_b": dlb}
    return dx1, (dsh, dsc, dgate, dng), grads, small, got


def _loss_head(name, x, target, fg):
    def body(rows, vecs):
        (xv, tv), (g,) = rows, vecs
        xhat, r = _rms_parts(xv)
        err = xhat * g - tv
        dy = err * (1.0 / D)
        return [_rms_bwd(dy * g, xhat, r)], [_colsum(err * err), _colsum(dy * xhat)]

    return _rowwise(name, body, [x, target], [fg], [(D, F32)], [(1, D), (1, D)])


FF_NAMES, FF_AXES = ("ffn_wg", "ffn_wu", "ffn_wd"), (1, 1, 0)
MX_NAMES, MX_AXES = ("w_in", "attn_wo", "conv_wo", "w_out"), (1, 1, 0, 0)
GROUPS = (("ff", 0, 0), ("mx", 0), ("ff", 0, 1), ("ff", 1, 0), ("mx", 1), ("ff", 1, 1))


def _group_name(grp):
    return "_".join(str(p) for p in grp)


def _mix_weights(blocks, small):
    w_in, attn_wo, conv_wo, w_out = [_join(b, a) for b, a in zip(blocks, MX_AXES)]
    return dict(small, w_qkv=w_in[:, :3 * QKV_W], w_u=w_in[:, 3 * QKV_W:3 * QKV_W + 2 * D],
                w_ga=w_in[:, 3 * QKV_W + 2 * D:3 * QKV_W + 3 * D], w_gc=w_in[:, 3 * QKV_W + 3 * D:],
                attn_wo=attn_wo, conv_wo=conv_wo, w_out=w_out)


def _local_step(x, target, mod, norm_g, shards, small_w, final_g, gather_job, exchange_job):
    cos2, sin2 = _rope_tables(x.shape[0])
    row = lambda a: a[None, :]
    ng = lambda l, i: row(norm_g[l, i])
    m = lambda l, i: row(mod[l, i])
    blocks = _run_job("gather_" + _group_name(GROUPS[0]), gather_job(shards[GROUPS[0]]))
    saved, wts = [], []
    for n, grp in enumerate(GROUPS):
        nxt = gather_job(shards[GROUPS[n + 1]]) if n + 1 < len(GROUPS) else None
        l = grp[1]
        if grp[0] == "ff":
            i = 2 * grp[2]
            wts.append([_join(b, a) for b, a in zip(blocks, FF_AXES)])
            x, sv, blocks = _ffn_fwd(f"l{l}f{grp[2]}", x, ng(l, i), m(l, 3 * i + 1), m(l, 3 * i), m(l, 3 * i + 2), *wts[-1], job=nxt)
        else:
            wts.append(_mix_weights(blocks, small_w[l]))
            x, sv, blocks = _mix_fwd(f"l{l}mx", x, ng(l, 1), m(l, 4), m(l, 3), m(l, 5), wts[-1], cos2, sin2, job=nxt)
        saved.append(sv)
    dx, sq, dfg = _loss_head("loss_head", x, target, row(final_g))
    loss = (0.5 / D) * jnp.sum(sq)
    dmod = [[None] * 3, [None] * 3]
    dng = [[None] * 3, [None] * 3]
    small, recv, payload, sent = [None, None], {}, None, None
    for n in reversed(range(len(GROUPS))):
        grp = GROUPS[n]
        l = grp[1]
        job = exchange_job(payload) if payload is not None else None
        if grp[0] == "ff":
            i = 2 * grp[2]
            dx, v, grads, got = _ffn_bwd(f"l{l}f{grp[2]}", dx, saved[n], ng(l, i), m(l, 3 * i + 1), m(l, 3 * i + 2), *wts[n], job=job)
            payload = [_split(g, a) for g, a in zip(grads, FF_AXES)]
        else:
            i = 1
            dx, v, grads, small[l], got = _mix_bwd(f"l{l}mx", dx, saved[n], ng(l, 1), m(l, 4), m(l, 5), wts[n], cos2, sin2, job=job)
            payload = [_split(grads[k], a) for k, a in zip(MX_NAMES, MX_AXES)]
        if job:
            recv[sent] = got
        sent = grp
        dmod[l][i] = jnp.concatenate(v[:3], axis=0)
        dng[l][i] = v[3]
    recv[sent] = _run_job("exchange_" + _group_name(sent), exchange_job(payload))
    dmod = jnp.stack([jnp.concatenate(d, axis=0) for d in dmod])
    small = {k: jnp.stack([small[0][k].reshape(-1, D), small[1][k].reshape(-1, D)]) for k in small[0]}
    small = dict(small, norm_g=jnp.stack([jnp.concatenate(d, axis=0) for d in dng]), final_g=dfg.reshape(D))
    return loss, dx, dmod, small, recv


HBM_SPEC = pl.BlockSpec(memory_space=pl.ANY)


def _place():
    return lax.axis_index("x"), lax.axis_index("y"), lax.axis_index("c")


N_PEERS = N_DEV - 1


def _gather_job(arrays):
    n_a = len(arrays)

    def run(phase, x_refs, out_refs, send_sems, recv_sems, local_sems):
        x, y, c = _place()
        me, sibling = (x, y, c), (x, y, 1 - c)
        chips = [(1 - x, y), (x, 1 - y), (1 - x, 1 - y)]

        def copy(a, k, block, to, from_input=False):
            px, py, pc = block
            rows = out_refs[a].at[4 * px + 2 * py + pc]
            return pltpu.make_async_remote_copy(
                src_ref=x_refs[a] if from_input else rows, dst_ref=rows, send_sem=send_sems.at[a * N_PEERS + k],
                recv_sem=recv_sems.at[a * N_PEERS + k], device_id=to, device_id_type=pl.DeviceIdType.MESH)

        mine = [pltpu.make_async_copy(x_refs[a], out_refs[a].at[4 * x + 2 * y + c], local_sems.at[a]) for a in range(n_a)]
        first = []
        for j, chip in enumerate(chips):
            first += [copy(a, 1 + j, me, (*chip, c), from_input=True) for a in range(n_a)]
        first += [copy(a, 0, me, sibling, from_input=True) for a in range(n_a)]
        if phase == "start":
            for cp in mine + first:
                cp.start()
            return
        passed = []
        for j, chip in enumerate(chips):
            for a in range(n_a):
                copy(a, 1 + j, (*chip, c), me).wait_recv()
                passed.append(copy(a, 4 + j, (*chip, c), sibling))
                passed[-1].start()
        for a in range(n_a):
            copy(a, 0, sibling, me).wait_recv()
        for j, chip in enumerate(chips):
            for a in range(n_a):
                copy(a, 4 + j, (*chip, 1 - c), me).wait_recv()
        for cp in first + passed:
            cp.wait_send()
        for cp in mine:
            cp.wait()

    return dict(arrays=list(arrays), run=run,
                out_shape=[jax.ShapeDtypeStruct((N_DEV, *a.shape), a.dtype) for a in arrays])


def _job_scratch(job):
    n_a = len(job["arrays"])
    return [pltpu.SemaphoreType.DMA((n_a * N_PEERS,)), pltpu.SemaphoreType.DMA((n_a * N_PEERS,)), pltpu.SemaphoreType.DMA((n_a,))]


def _run_job(name, job):
    n_a = len(job["arrays"])

    def body(*refs):
        job["run"]("start", refs[:n_a], refs[n_a:2 * n_a], *refs[2 * n_a:])
        job["run"]("finish", refs[:n_a], refs[n_a:2 * n_a], *refs[2 * n_a:])

    return pl.pallas_call(
        body, name=name, out_shape=job["out_shape"], in_specs=[HBM_SPEC] * n_a, out_specs=[HBM_SPEC] * n_a,
        scratch_shapes=_job_scratch(job),
    )(*job["arrays"])


def _all_gather(name, arrays):
    return _run_job(name, _gather_job(arrays))


def _exchange_job(arrays):
    n_a = len(arrays)

    def run(phase, g_refs, out_refs, send_sems, recv_sems, local_sems):
        x, y, c = _place()
        my = 4 * x + 2 * y + c
        copies = [pltpu.make_async_copy(g_refs[a].at[my], out_refs[a].at[my], local_sems.at[a]) for a in range(n_a)]
        for k in (4, 2, 6, 1, 5, 3, 7):
            px = 1 - x if k & 4 else x
            py = 1 - y if k & 2 else y
            pc = 1 - c if k & 1 else c
            for a in range(n_a):
                copies.append(pltpu.make_async_remote_copy(
                    src_ref=g_refs[a].at[4 * px + 2 * py + pc], dst_ref=out_refs[a].at[my],
                    send_sem=send_sems.at[a * N_PEERS + k - 1], recv_sem=recv_sems.at[a * N_PEERS + k - 1],
                    device_id=(px, py, pc), device_id_type=pl.DeviceIdType.MESH))
        for cp in copies:
            if phase == "start":
                cp.start()
            else:
                cp.wait()

    return dict(arrays=list(arrays), run=run, out_shape=[jax.ShapeDtypeStruct(a.shape, a.dtype) for a in arrays])


def _exchange_blocks(name, arrays):
    return _run_job(name, _exchange_job(arrays))


SLAB_TS = 2048


def _sum_parts(name, parts):
    def body(rows, _v):
        tot = rows[0].astype(F32)
        for r in rows[1:]:
            tot = tot + r.astype(F32)
        return [tot], []

    return _rowwise(name, body, list(parts), [], [(parts[0].shape[1], F32)], ts=SLAB_TS)[0]


def _adamw(name, w, parts, m, v):
    def body(rows, _v):
        wv, mv, vv = rows[0], rows[1], rows[2]
        g = rows[3].astype(F32)
        for r in rows[4:]:
            g = g + r.astype(F32)
        m2 = ADAM_B1 * mv + (1.0 - ADAM_B1) * g
        v2 = ADAM_B2 * vv + (1.0 - ADAM_B2) * (g * g)
        m_hat = m2 / (1.0 - ADAM_B1 ** ADAM_STEP)
        v_hat = v2 / (1.0 - ADAM_B2 ** ADAM_STEP)
        delta = -ADAM_LR * (m_hat / (jnp.sqrt(v_hat) + ADAM_EPS) + ADAM_WD * wv)
        return [g, delta, m2, v2], []

    width = w.shape[1]
    return _rowwise(name, body, [w, m, v, *parts], [], [(width, F32)] * 4, ts=max(16, SLAB_TS * LANES // width))


def _to_slab(arrays, dtype):
    flat = jnp.concatenate([a.reshape(-1).astype(dtype) for a in arrays])
    rows = -(-flat.shape[0] // LANES)
    rows = -(-rows // 8) * 8 if rows <= SLAB_TS else -(-rows // SLAB_TS) * SLAB_TS
    return jnp.pad(flat, (0, rows * LANES - flat.shape[0])).reshape(rows, LANES)


def _from_slab(slab, shapes, lead=()):
    flat = slab.reshape(*lead, -1)
    out, at = [], 0
    for shp in shapes:
        size = 1
        for d in shp:
            size *= d
        out.append(flat[..., at:at + size].reshape(*lead, *shp))
        at += size
    return out


BIG = {"ffn_wg": 3, "ffn_wu": 3, "ffn_wd": 2, "w_in": 2, "attn_wo": 2, "conv_wo": 1, "w_out": 1}


def _join(blocks, axis):
    full = jnp.moveaxis(blocks, 0, axis)
    return full.reshape(*full.shape[:axis], -1, *full.shape[axis + 2:])


def _split(full, axis):
    shp = full.shape
    return jnp.moveaxis(full.reshape(*shp[:axis], N_DEV, shp[axis] // N_DEV, *shp[axis + 1:]), axis, 0)


def kernel(x, c, ada_w, ada_b, norm_g, ffn_wg, ffn_wu, ffn_wd, w_in, attn_wo, conv_w, conv_b, conv_ln_g, conv_ln_b, conv_wo, w_out, final_g, loss_target, m_ada_w, m_ada_b, m_norm_g, m_ffn_wg, m_ffn_wu, m_ffn_wd, m_w_in, m_attn_wo, m_conv_w, m_conv_b, m_conv_ln_g, m_conv_ln_b, m_conv_wo, m_w_out, m_final_g, v_ada_w, v_ada_b, v_norm_g, v_ffn_wg, v_ffn_wu, v_ffn_wd, v_w_in, v_attn_wo, v_conv_w, v_conv_b, v_conv_ln_g, v_conv_ln_b, v_conv_wo, v_w_out, v_final_g):
    px, py, pc = _place()
    me = 4 * px + 2 * py + pc
    n_mod = ada_w.shape[2]
    big_w = dict(ffn_wg=ffn_wg, ffn_wu=ffn_wu, ffn_wd=ffn_wd, w_in=w_in, attn_wo=attn_wo, conv_wo=conv_wo, w_out=w_out)
    big_m = dict(ffn_wg=m_ffn_wg, ffn_wu=m_ffn_wu, ffn_wd=m_ffn_wd, w_in=m_w_in, attn_wo=m_attn_wo, conv_wo=m_conv_wo, w_out=m_w_out)
    big_v = dict(ffn_wg=v_ffn_wg, ffn_wu=v_ffn_wu, ffn_wd=v_ffn_wd, w_in=v_w_in, attn_wo=v_attn_wo, conv_wo=v_conv_wo, w_out=v_w_out)

    small_in = [c, norm_g, conv_w]
    g1 = _all_gather("gather_small", [_to_slab(small_in, F32)])[0]
    c_all, ng_blocks, cw_blocks = _from_slab(g1, [a.shape for a in small_in], lead=(N_DEV,))
    c_all = c_all.reshape(N_DEV, D)
    norm_g_full = _join(ng_blocks, 2)
    conv_w_full = _join(cw_blocks, 2)
    as2d = lambda a: a.reshape(-1, a.shape[-1])

    c_act = _rowwise("cond_silu", lambda rows, _v: ([rows[0] * _sigmoid(rows[0])], []), [c_all], [], [(D, BF16)])[0]
    c_pad = jnp.pad(c_act, ((0, LANES - N_DEV), (0, 0)))
    mod_cols = []
    for l in range(2):
        bias = lax.dynamic_slice_in_dim(ada_b[l], me * n_mod, n_mod)[None, :]
        out = _mm(f"mod{l}", [(c_pad, ada_w[l].astype(BF16))], lambda accs, _t, vs: [accs[0] + vs[0]], [F32], vecs=[bias])[0]
        mod_cols.append(out[:N_DEV])
    g2 = _all_gather("gather_mod", [_to_slab([jnp.stack(mod_cols)], F32)])[0]
    mod_all = _from_slab(g2, [(2, N_DEV, n_mod)], lead=(N_DEV,))[0]
    mod = lax.dynamic_index_in_dim(mod_all, me, axis=2, keepdims=False)
    mod = jnp.moveaxis(mod, 0, 1).reshape(2, 9, D)

    index = {grp: (grp[1], grp[2]) if grp[0] == "ff" else (grp[1],) for grp in GROUPS}
    names = {grp: FF_NAMES if grp[0] == "ff" else MX_NAMES for grp in GROUPS}
    shards = {grp: [big_w[k][index[grp]].astype(BF16) for k in names[grp]] for grp in GROUPS}
    small_l = [dict(conv_w=jnp.pad(conv_w_full[l], ((0, HALO - CONV_K), (0, 0))), conv_b=conv_b[l][None, :],
                    ln_g=conv_ln_g[l][None, :], ln_b=conv_ln_b[l][None, :]) for l in range(2)]

    loss, dx, dmod, small, recv = _local_step(x[0], loss_target[0], mod, norm_g_full, shards, small_l, final_g,
                                              _gather_job, _exchange_job)
    loss = lax.psum(loss, MESH_AXES)

    small_names = ["norm_g", "conv_w", "conv_b", "ln_g", "ln_b", "final_g"]
    small_parts = [dmod] + [small[k] for k in small_names]
    g3 = _all_gather("gather_small_grads", [_to_slab(small_parts, F32)])[0]
    tot = _sum_parts("sum_small_grads", [g3[k] for k in range(N_DEV)])
    _, g_ng, g_cw, g_cb, g_lg, g_lb, g_fg = _from_slab(tot, [a.shape for a in small_parts])
    g_ab = _from_slab(tot, [(2, 9 * D)])[0]
    dmod_all = _from_slab(g3, [dmod.shape], lead=(N_DEV,))[0].reshape(N_DEV, 2, 9 * D)
    dm_mine = lax.dynamic_slice_in_dim(dmod_all, me * n_mod, n_mod, axis=2)
    g_aw = jnp.stack([
        _mm_tn(f"dada_w{l}", c_pad, jnp.pad(dm_mine[:, l], ((0, LANES - N_DEV), (0, 0))).astype(BF16), out_dtype=F32)
        for l in range(2)])
    cols = lambda a: lax.dynamic_slice_in_dim(a, me * (D // N_DEV), D // N_DEV, axis=2)
    small_w = [ada_b, norm_g, conv_w, conv_b, conv_ln_g, conv_ln_b, final_g]
    small_m = [m_ada_b, m_norm_g, m_conv_w, m_conv_b, m_conv_ln_g, m_conv_ln_b, m_final_g]
    small_v = [v_ada_b, v_norm_g, v_conv_w, v_conv_b, v_conv_ln_g, v_conv_ln_b, v_final_g]
    small_g = [g_ab, cols(g_ng), cols(g_cw), g_cb, g_lg, g_lb, g_fg]
    s_shapes = [a.shape for a in small_w]
    s_out = _adamw("adamw_small", _to_slab(small_w, F32), [_to_slab(small_g, F32)], _to_slab(small_m, F32), _to_slab(small_v, F32))
    aw_out = [o.reshape(ada_w.shape) for o in _adamw("adamw_ada_w", as2d(ada_w), [as2d(g_aw)], as2d(m_ada_w), as2d(v_ada_w))]

    upd = {}
    for grp in GROUPS:
        for k, blocks in zip(names[grp], recv[grp]):
            parts = [(blocks, blocks.shape[2], 0, j) for j in range(N_DEV)]
            at = index[grp]
            upd[k, at] = _adamw(f"adamw_{k}_{_group_name(grp)}", big_w[k][at], parts, big_m[k][at], big_v[k][at])

    def stacked(k, i):
        if k in FF_NAMES:
            return jnp.stack([jnp.stack([upd[k, (l, j)][i] for j in range(2)]) for l in range(2)])
        return jnp.stack([upd[k, (l,)][i] for l in range(2)])

    def ordered(i):
        ab, ng, cw, cb, lg, lb, fg = _from_slab(s_out[i], s_shapes)
        bg = {k: stacked(k, i) for k in FF_NAMES + MX_NAMES}
        return [aw_out[i], ab, ng, bg["ffn_wg"], bg["ffn_wu"], bg["ffn_wd"], bg["w_in"], bg["attn_wo"], cw, cb, lg, lb,
                bg["conv_wo"], bg["w_out"], fg]

    return (loss, dx[None], *ordered(0), *ordered(1), *ordered(2), *ordered(3))
```

```python
import functools

import jax
import jax.numpy as jnp
from jax import lax
from jax.experimental import pallas as pl
from jax.experimental.pallas import tpu as pltpu

F32 = jnp.float32
BF16 = jnp.bfloat16

N_DEV = 8
D = 1024
D_FF = 2816
HEAD_DIM = 64
GROUP_W = 256
ATTN_DILATIONS = (1, 4, 16)
BLK = 128
QKV_W = 768
CONV_K = 31
ROPE_THETA = 10000.0
EPS = 1e-6
NEG_INF = -1e30
ADAM_LR, ADAM_B1, ADAM_B2, ADAM_EPS, ADAM_WD, ADAM_STEP = 0.001, 0.9, 0.999, 1e-08, 0.01, 10

V7X_VMEM_BYTES = 64 * 1024 * 1024
VMEM_LIMIT = V7X_VMEM_BYTES - 8 * 1024 * 1024
LANES = 128
MESH_AXES = ("x", "y", "c")


def _params(n_grid):
    return pltpu.CompilerParams(vmem_limit_bytes=VMEM_LIMIT, dimension_semantics=("arbitrary",) * n_grid)


def _sigmoid(v):
    return 1.0 / (1.0 + jnp.exp(-v))


def _mm(name, prods, epilogue, out_dtypes, *, nt=False, tiles=(), vecs=(), tm=512, tn=None, a_pre=None, chunk=None,
        job=None, n_sums=0):
    s = prods[0][0].shape[0]
    n = prods[0][1].shape[0] if nt else prods[0][1].shape[1]
    tn = n if tn is None else tn
    tm = min(tm, s)
    assert s % tm == 0 and n % tn == 0
    n_p, n_t, n_v = len(prods), len(tiles), len(vecs)
    dn = (((1,), (1,)), ((), ())) if nt else (((1,), (0,)), ((), ()))

    chunk = tn if chunk is None else chunk
    bounds = [(c0, min(chunk, tn - c0)) for c0 in range(0, tn, chunk)]

    n_o = len(out_dtypes) + n_sums
    n_j = len(job["arrays"]) if job else 0
    n_steps = (n // tn, s // tm)

    def body(*refs):
        p_refs, rest = refs[:2 * n_p], refs[2 * n_p:]
        t_refs, v_refs, rest = rest[:n_t], rest[n_t:n_t + n_v], rest[n_t + n_v:]
        j_in, o_refs, j_out, sems = rest[:n_j], rest[n_j:n_j + n_o], rest[n_j + n_o:2 * n_j + n_o], rest[2 * n_j + n_o:]
        o_refs, s_refs = o_refs[:n_o - n_sums], o_refs[n_o - n_sums:]
        if job:
            @pl.when(jnp.logical_and(pl.program_id(0) == 0, pl.program_id(1) == 0))
            def _():
                job["run"]("start", j_in, j_out, *sems)
        lhs = []
        for p in range(n_p):
            a = p_refs[2 * p][...]
            lhs.append(a if a_pre is None else a_pre(a))
        for c0, cw in bounds:
            cols = pl.ds(c0, cw)
            accs = []
            for p in range(n_p):
                b = p_refs[2 * p + 1][cols, :] if nt else p_refs[2 * p + 1][:, cols]
                accs.append(lax.dot_general(lhs[p], b, dn, preferred_element_type=F32))
            outs = epilogue(accs, [t[:, cols] for t in t_refs], [v[:, cols] for v in v_refs])
            for o_ref, o in zip(o_refs, outs[:len(o_refs)], strict=True):
                o_ref[:, cols] = o.astype(o_ref.dtype)
            if n_sums:
                first_row_tile = pl.program_id(1) == 0

                @pl.when(first_row_tile)
                def _():
                    for s_ref, part in zip(s_refs, outs[len(o_refs):], strict=True):
                        s_ref[:, cols] = part

                @pl.when(jnp.logical_not(first_row_tile))
                def _():
                    for s_ref, part in zip(s_refs, outs[len(o_refs):], strict=True):
                        s_ref[:, cols] += part
        if job:
            @pl.when(jnp.logical_and(pl.program_id(0) == n_steps[0] - 1, pl.program_id(1) == n_steps[1] - 1))
            def _():
                job["run"]("finish", j_in, j_out, *sems)

    in_specs = []
    operands = []
    for a, b in prods:
        k = a.shape[1]
        in_specs.append(pl.BlockSpec((tm, k), lambda j, i: (i, 0)))
        in_specs.append(pl.BlockSpec((tn, k), lambda j, i: (j, 0)) if nt else pl.BlockSpec((k, tn), lambda j, i: (0, j)))
        operands += [a, b]
    for arr, off in tiles:
        in_specs.append(pl.BlockSpec((tm, tn), functools.partial(lambda j, i, off: (i, j + off), off=off)))
        operands.append(arr)
    for v in vecs:
        in_specs.append(pl.BlockSpec((1, tn), lambda j, i: (0, j)))
        operands.append(v)
    out_specs = [pl.BlockSpec((tm, tn), lambda j, i: (i, j)) for _ in out_dtypes]
    out_specs += [pl.BlockSpec((1, tn), lambda j, i: (0, j)) for _ in range(n_sums)]
    out_shape = [jax.ShapeDtypeStruct((s, n), dt) for dt in out_dtypes] + [jax.ShapeDtypeStruct((1, n), F32)] * n_sums
    scratch = []
    if job:
        in_specs += [HBM_SPEC] * n_j
        operands += job["arrays"]
        out_specs += [HBM_SPEC] * n_j
        out_shape += job["out_shape"]
        scratch = _job_scratch(job)
    out = pl.pallas_call(
        body, name=name, grid=n_steps, in_specs=in_specs, out_specs=out_specs, out_shape=out_shape,
        scratch_shapes=scratch, compiler_params=_params(2),
    )(*operands)
    return (out[:n_o], out[n_o:]) if job else out


def _mm_tn(name, a, b, *, tk=512, tn=None, out_dtype=BF16):
    s, m = a.shape
    n = b.shape[1]
    tn = n if tn is None else tn
    tk = min(tk, s)
    n_k = s // tk
    assert s % tk == 0 and n % tn == 0

    def body(a_ref, b_ref, o_ref, acc_ref):
        k = pl.program_id(1)

        @pl.when(k == 0)
        def _():
            acc_ref[...] = jnp.zeros_like(acc_ref)

        acc_ref[...] += lax.dot_general(a_ref[...], b_ref[...], (((0,), (0,)), ((), ())), preferred_element_type=F32)

        @pl.when(k == n_k - 1)
        def _():
            o_ref[...] = acc_ref[...].astype(o_ref.dtype)

    return pl.pallas_call(
        body, name=name, grid=(n // tn, n_k),
        in_specs=[pl.BlockSpec((tk, m), lambda j, k: (k, 0)), pl.BlockSpec((tk, tn), lambda j, k: (k, j))],
        out_specs=pl.BlockSpec((m, tn), lambda j, k: (0, j)),
        out_shape=jax.ShapeDtypeStruct((m, n), out_dtype),
        scratch_shapes=[pltpu.VMEM((m, tn), F32)],
        compiler_params=_params(2),
    )(a, b)


def _pick_rows(s, target):
    if s <= target:
        return s
    return max(t for t in range(16, target + 1, 16) if s % t == 0)


def _rowwise(name, body, rows, vecs, outs, accs=(), *, ts=512):
    rows = [(r if isinstance(r, tuple) else (r, r.shape[1], 0)) for r in rows]
    rows = [r if len(r) == 4 else (*r, None) for r in rows]
    s = rows[0][0].shape[0]
    ts = _pick_rows(s, ts)
    n_r, n_v, n_o, n_a = len(rows), len(vecs), len(outs), len(accs)

    def kbody(*refs):
        r_refs, v_refs = refs[:n_r], refs[n_r:n_r + n_v]
        o_refs, a_refs = refs[n_r + n_v:n_r + n_v + n_o], refs[n_r + n_v + n_o:]
        res_o, res_a = body([r[...] for r in r_refs], [v[...] for v in v_refs])
        for o_ref, o in zip(o_refs, res_o, strict=True):
            o_ref[...] = o.astype(o_ref.dtype)
        if n_a:
            first = pl.program_id(0) == 0

            @pl.when(first)
            def _():
                for a_ref, a in zip(a_refs, res_a, strict=True):
                    a_ref[...] = a

            @pl.when(jnp.logical_not(first))
            def _():
                for a_ref, a in zip(a_refs, res_a, strict=True):
                    a_ref[...] += a

    in_specs = [
        pl.BlockSpec((ts, w), functools.partial(lambda i, cb: (i, cb), cb=cb)) if lead is None else
        pl.BlockSpec((None, ts, w), functools.partial(lambda i, cb, lead: (lead, i, cb), cb=cb, lead=lead))
        for _, w, cb, lead in rows]
    in_specs += [pl.BlockSpec(v.shape, functools.partial(lambda i, nd: (0,) * nd, nd=v.ndim)) for v in vecs]
    out_specs = [pl.BlockSpec((ts, w), lambda i: (i, 0)) for w, _ in outs]
    out_specs += [pl.BlockSpec(shp, functools.partial(lambda i, nd: (0,) * nd, nd=len(shp))) for shp in accs]
    out_shape = [jax.ShapeDtypeStruct((s, w), dt) for w, dt in outs] + [jax.ShapeDtypeStruct(shp, F32) for shp in accs]
    return pl.pallas_call(
        kbody, name=name, grid=(s // ts,), in_specs=in_specs, out_specs=out_specs, out_shape=out_shape,
        compiler_params=_params(1),
    )(*[r[0] for r in rows], *vecs)


def _colsum(v):
    return jnp.sum(v, axis=0, keepdims=True)


def _rms_parts(x):
    r = lax.rsqrt(jnp.mean(x * x, axis=-1, keepdims=True) + EPS)
    return x * r, r


def _rms_bwd(dxhat, xhat, r):
    return r * (dxhat - xhat * jnp.mean(dxhat * xhat, axis=-1, keepdims=True))


def _modulate(name, x, ng, sc, sh):
    def body(rows, vecs):
        (xv,), (g, s_, b) = rows, vecs
        xhat, _ = _rms_parts(xv)
        return [xhat * g * (1.0 + s_) + b], []

    return _rowwise(name, body, [x], [ng, sc, sh], [(D, BF16)])[0]


def _modulate_bwd_epi(accs, tl, vs):
    dh = accs[0]
    for acc in accs[1:]:
        dh = dh + acc
    (xv, dxo), (g, s_) = tl, vs
    xhat, r = _rms_parts(xv)
    dn = dh * (1.0 + s_)
    return [dxo + _rms_bwd(dn * g, xhat, r), _colsum(dh), _colsum(dh * xhat * g), _colsum(dn * xhat)]


BWD_LAST_TM = 256


def _gate_scale(name, dx, f, gate, coef):
    def body(rows, vecs):
        (dxv, fv), (g,) = rows, vecs
        return [coef * g * dxv], [_colsum(coef * dxv * fv.astype(F32))]

    return _rowwise(name, body, [dx, f], [gate], [(D, BF16)], [(1, D)])


def _with_job(result, job):
    return result if job else (result, None)


def _ffn_fwd(tag, x, ng, sc, sh, gate, wg, wu, wd, job=None):
    h = _modulate(f"{tag}_mod", x, ng, sc, sh)

    def up_epi(accs, _t, _v):
        a, u = accs
        return [a, u, a * _sigmoid(a) * u]

    (a, u, t), got = _with_job(_mm(f"{tag}_up", [(h, wg), (h, wu)], up_epi, [BF16] * 3, tn=D_FF // 2, job=job), job)

    def down_epi(accs, tl, vs):
        return [tl[0] + 0.5 * vs[0] * accs[0], accs[0]]

    x_out, f = _mm(f"{tag}_down", [(t, wd)], down_epi, [F32, BF16], tiles=[(x, 0)], vecs=[gate])
    return x_out, (x, h, a, u, t, f), got


def _ffn_bwd(tag, dxo, saved, ng, sc, gate, wg, wu, wd, exchange_job):
    x, h, a, u, t, f = saved
    df, dgate = _gate_scale(f"{tag}_gs", dxo, f, gate, 0.5)

    def epi(accs, tl, _v):
        dt = accs[0]
        av, uv = tl[0].astype(F32), tl[1].astype(F32)
        sg = _sigmoid(av)
        return [dt * uv * (sg * (1.0 + av * (1.0 - sg))), dt * (av * sg)]

    da, du = _mm(f"{tag}_bdown", [(df, wd)], epi, [BF16] * 2, nt=True, tiles=[(a, 0), (u, 0)], tn=D_FF // 2,
                 chunk=3 * LANES)
    grads = (_mm_tn(f"{tag}_dwg", h, da), _mm_tn(f"{tag}_dwu", h, du), _mm_tn(f"{tag}_dwd", t, df))
    job = exchange_job([_split(g, ax) for g, ax in zip(grads, FF_AXES)])
    (dx_in, dsh, dsc, dng), got = _mm(f"{tag}_bup", [(da, wg), (du, wu)], _modulate_bwd_epi, [F32], nt=True,
                                      tiles=[(x, 0), (dxo, 0)], vecs=[ng, sc], n_sums=3, tm=BWD_LAST_TM, job=job)
    return dx_in, (dsh, dsc, dgate, dng), got


def _rope_tables(s):
    half = HEAD_DIM // 2
    inv_freq = ROPE_THETA ** (-(jnp.arange(half, dtype=F32) * 2.0 / HEAD_DIM))
    ang = jnp.arange(s, dtype=F32)[:, None] * inv_freq[None, :]
    cos, sin = jnp.cos(ang), jnp.sin(ang)
    return jnp.tile(jnp.concatenate([cos, cos], axis=1), (1, 2)), jnp.tile(jnp.concatenate([-sin, sin], axis=1), (1, 2))


def _rotate(v, cos2, sin2, sign):
    w = v.shape[1]
    lane = lax.broadcasted_iota(jnp.int32, v.shape, 1)
    partner = jnp.where(lane % HEAD_DIM < HEAD_DIM // 2, pltpu.roll(v, w - HEAD_DIM // 2, 1), pltpu.roll(v, HEAD_DIM // 2, 1))
    reps = w // LANES
    return v * jnp.tile(cos2, (1, reps)) + partner * (sign * jnp.tile(sin2, (1, reps)))


SPLIT_TS = 512


def _dilated_spec(dil, ts):
    return pl.BlockSpec((dil, ts // dil, GROUP_W), lambda i: (0, i, 0))


def _dilated_shape(s, dil, dtype):
    return jax.ShapeDtypeStruct((dil, s // dil, GROUP_W), dtype)


CHUNKS_PER_GROUP = GROUP_W // LANES


def _put(buf, chunk0, val):
    for c in range(val.shape[1] // LANES):
        buf[chunk0 + c] = val[:, c * LANES:(c + 1) * LANES]


def _get(buf, chunk0, n):
    return jnp.concatenate([buf[chunk0 + c] for c in range(n)], axis=1)


def _strided_rows(r, dil, ts):
    return pl.ds(r, ts // dil, stride=dil) if dil > 1 else pl.ds(0, ts)


def _deinterleave_one(buf, chunk0, out_ref, dil, ts):
    for half in range(CHUNKS_PER_GROUP):
        for r in range(dil):
            src = buf.at[chunk0 + half][_strided_rows(r, dil, ts), :]
            out_ref.at[r][:, pl.ds(half * LANES, LANES)] = src.astype(out_ref.dtype)


def _interleave_one(in_ref, buf, chunk0, dil, ts):
    for half in range(CHUNKS_PER_GROUP):
        for r in range(dil):
            src = in_ref.at[r][:, pl.ds(half * LANES, LANES)]
            buf.at[chunk0 + half][_strided_rows(r, dil, ts), :] = src.astype(F32)


def _deinterleave(buf, chunk0, out_refs, ts):
    for g, dil in enumerate(ATTN_DILATIONS):
        _deinterleave_one(buf, chunk0 + g * CHUNKS_PER_GROUP, out_refs[g], dil, ts)


def _interleave(in_refs, buf, chunk0, ts):
    for g, dil in enumerate(ATTN_DILATIONS):
        _interleave_one(in_refs[g], buf, chunk0 + g * CHUNKS_PER_GROUP, dil, ts)


def _rope_split(name, qkv, cos2, sin2):
    s, ts = qkv.shape[0], SPLIT_TS

    def body(qkv_ref, c_ref, s_ref, *rest):
        outs, buf = rest[:9], rest[9]
        c2, s2 = c_ref[...], s_ref[...]
        per = QKV_W // LANES
        _put(buf, 0, _rotate(qkv_ref[:, pl.ds(0, QKV_W)].astype(F32), c2, s2, 1.0))
        _put(buf, per, _rotate(qkv_ref[:, pl.ds(QKV_W, QKV_W)].astype(F32), c2, s2, 1.0))
        _put(buf, 2 * per, qkv_ref[:, pl.ds(2 * QKV_W, QKV_W)].astype(F32))
        for t in range(3):
            _deinterleave(buf, t * per, outs[3 * t:3 * t + 3], ts)

    tab = pl.BlockSpec((ts, LANES), lambda i: (i, 0))
    outs = pl.pallas_call(
        body, name=name, grid=(s // ts,), in_specs=[pl.BlockSpec((ts, 3 * QKV_W), lambda i: (i, 0)), tab, tab],
        out_specs=[_dilated_spec(d, ts) for _ in range(3) for d in ATTN_DILATIONS],
        out_shape=[_dilated_shape(s, d, BF16) for _ in range(3) for d in ATTN_DILATIONS],
        scratch_shapes=[pltpu.VMEM((3 * QKV_W // LANES, ts, LANES), F32)], compiler_params=_params(1),
    )(qkv, cos2, sin2)
    return outs[0:3], outs[3:6], outs[6:9]


def _rope_join(name, dq, dk, dv, cos2, sin2):
    s, ts = cos2.shape[0], SPLIT_TS

    def body(*refs):
        ins, c_ref, s_ref, o_ref, buf = refs[:9], refs[9], refs[10], refs[11], refs[12]
        per = QKV_W // LANES
        for t in range(3):
            _interleave(ins[3 * t:3 * t + 3], buf, t * per, ts)
        c2, s2 = c_ref[...], s_ref[...]
        o_ref[:, pl.ds(0, QKV_W)] = _rotate(_get(buf, 0, per), c2, s2, -1.0).astype(o_ref.dtype)
        o_ref[:, pl.ds(QKV_W, QKV_W)] = _rotate(_get(buf, per, per), c2, s2, -1.0).astype(o_ref.dtype)
        o_ref[:, pl.ds(2 * QKV_W, QKV_W)] = _get(buf, 2 * per, per).astype(o_ref.dtype)

    tab = pl.BlockSpec((ts, LANES), lambda i: (i, 0))
    return pl.pallas_call(
        body, name=name, grid=(s // ts,),
        in_specs=[_dilated_spec(d, ts) for _ in range(3) for d in ATTN_DILATIONS] + [tab, tab],
        out_specs=pl.BlockSpec((ts, 3 * QKV_W), lambda i: (i, 0)), out_shape=jax.ShapeDtypeStruct((s, 3 * QKV_W), BF16),
        scratch_shapes=[pltpu.VMEM((3 * QKV_W // LANES, ts, LANES), F32)], compiler_params=_params(1),
    )(*dq, *dk, *dv, cos2, sin2)


def _head_masks(shape):
    lane = lax.broadcasted_iota(jnp.int32, shape, 1)
    return [jnp.logical_and(lane >= h * HEAD_DIM, lane < (h + 1) * HEAD_DIM) for h in range(GROUP_W // HEAD_DIM)]


def _grad_split(name, do, o, lse):
    s, ts = do.shape[0], SPLIT_TS

    def body(do_ref, o_ref, l_ref, *rest):
        outs, buf = rest[:9], rest[9]
        dov = do_ref[...].astype(F32)
        prod = dov * o_ref[...].astype(F32)
        delta = jnp.zeros_like(prod)
        for hm in _head_masks(prod.shape):
            delta = jnp.where(hm, jnp.sum(jnp.where(hm, prod, 0.0), axis=1, keepdims=True), delta)
        _put(buf, 0, dov)
        _put(buf, CHUNKS_PER_GROUP, delta)
        _put(buf, 2 * CHUNKS_PER_GROUP, l_ref[...])
        for t in range(3):
            for g in range(3):
                _deinterleave_one(buf, t * CHUNKS_PER_GROUP, outs[3 * t + g], ATTN_DILATIONS[g], ts)

    nat = pl.BlockSpec((ts, GROUP_W), lambda i: (i, 0))
    dts = [BF16, F32, F32]
    outs = pl.pallas_call(
        body, name=name, grid=(s // ts,), in_specs=[nat, nat, nat],
        out_specs=[_dilated_spec(d, ts) for _ in range(3) for d in ATTN_DILATIONS],
        out_shape=[_dilated_shape(s, d, dt) for dt in dts for d in ATTN_DILATIONS],
        scratch_shapes=[pltpu.VMEM((3 * CHUNKS_PER_GROUP, ts, LANES), F32)], compiler_params=_params(1),
    )(do, o, lse)
    return outs[0:3], outs[3:6], outs[6:9]


def _band_masks(has_prev):
    qi = lax.broadcasted_iota(jnp.int32, (BLK, BLK), 0)
    kj = lax.broadcasted_iota(jnp.int32, (BLK, BLK), 1)
    return kj <= qi, jnp.logical_and(kj >= qi, has_prev)


def _dot_nt(a, b):
    return lax.dot_general(a, b, (((1,), (1,)), ((), ())), preferred_element_type=F32)


def _dot_tn(a, b):
    return lax.dot_general(a, b, (((0,), (0,)), ((), ())), preferred_element_type=F32)


def _dot(a, b):
    return jnp.dot(a, b, preferred_element_type=F32)


ATTN_BLK = (None, BLK, GROUP_W)


def _attn_specs(clamp):
    cur = pl.BlockSpec(ATTN_BLK, lambda r, n: (r, clamp(n), 0))
    prev = pl.BlockSpec(ATTN_BLK, lambda r, n: (r, jnp.maximum(clamp(n) - 1, 0), 0))
    return [cur, cur, prev, cur, prev]


def _attn_fwd(name, q, k, v):
    dil, rows, _ = q.shape
    nb = rows // BLK
    scale = HEAD_DIM ** -0.5

    def body(q_ref, kc_ref, kp_ref, vc_ref, vp_ref, o_ref, l_ref):
        mask_c, mask_p = _band_masks(pl.program_id(1) > 0)
        q, kc, kp, vc, vp = q_ref[...], kc_ref[...], kp_ref[...], vc_ref[...], vp_ref[...]
        o_acc = jnp.zeros((BLK, GROUP_W), F32)
        l_acc = jnp.zeros((BLK, GROUP_W), F32)
        for hm in _head_masks((BLK, GROUP_W)):
            qm = jnp.where(hm, q, jnp.zeros_like(q))
            sc = jnp.where(mask_c, _dot_nt(qm, kc) * scale, NEG_INF)
            sp = jnp.where(mask_p, _dot_nt(qm, kp) * scale, NEG_INF)
            m = jnp.maximum(jnp.max(sc, axis=1, keepdims=True), jnp.max(sp, axis=1, keepdims=True))
            pc, pp = jnp.exp(sc - m), jnp.exp(sp - m)
            den = jnp.sum(pc, axis=1, keepdims=True) + jnp.sum(pp, axis=1, keepdims=True)
            oh = (_dot(pc.astype(BF16), vc) + _dot(pp.astype(BF16), vp)) / den
            o_acc = jnp.where(hm, oh, o_acc)
            l_acc = jnp.where(hm, m + jnp.log(den), l_acc)
        o_ref[...] = o_acc
        l_ref[...] = l_acc

    out_spec = pl.BlockSpec(ATTN_BLK, lambda r, n: (r, n, 0))
    return pl.pallas_call(
        body, name=name, grid=(dil, nb), in_specs=_attn_specs(lambda n: n), out_specs=[out_spec, out_spec],
        out_shape=[jax.ShapeDtypeStruct(q.shape, F32)] * 2, compiler_params=_params(2),
    )(q, k, k, v, v)


def _attn_merge(name, os_, ls_):
    s, ts = os_[0].shape[0] * os_[0].shape[1], SPLIT_TS

    def body(*refs):
        o_refs, l_refs, o_ref, l_ref, buf = refs[0:3], refs[3:6], refs[6], refs[7], refs[8]
        _interleave(o_refs, buf, 0, ts)
        _interleave(l_refs, buf, 3 * CHUNKS_PER_GROUP, ts)
        o0, o1, o2 = [_get(buf, g * CHUNKS_PER_GROUP, CHUNKS_PER_GROUP) for g in range(3)]
        l0, l1, l2 = [_get(buf, (3 + g) * CHUNKS_PER_GROUP, CHUNKS_PER_GROUP) for g in range(3)]
        m = jnp.maximum(jnp.maximum(l0, l1), l2)
        e0, e1, e2 = jnp.exp(l0 - m), jnp.exp(l1 - m), jnp.exp(l2 - m)
        tot = e0 + e1 + e2
        o_ref[...] = ((e0 * o0 + e1 * o1 + e2 * o2) / tot).astype(o_ref.dtype)
        l_ref[...] = m + jnp.log(tot)

    nat = pl.BlockSpec((ts, GROUP_W), lambda i: (i, 0))
    return pl.pallas_call(
        body, name=name, grid=(s // ts,), in_specs=[_dilated_spec(d, ts) for _ in range(2) for d in ATTN_DILATIONS],
        out_specs=[nat, nat], out_shape=[jax.ShapeDtypeStruct((s, GROUP_W), BF16), jax.ShapeDtypeStruct((s, GROUP_W), F32)],
        scratch_shapes=[pltpu.VMEM((6 * CHUNKS_PER_GROUP, ts, LANES), F32)], compiler_params=_params(1),
    )(*os_, *ls_)


def _attn_bwd(name, q, k, v, do, delta, lse):
    dil, rows, _ = q.shape
    nb = rows // BLK
    scale = HEAD_DIM ** -0.5

    def body(q_ref, kc_ref, kp_ref, vc_ref, vp_ref, do_ref, dl_ref, l_ref, dq_ref, dk_ref, dv_ref, ck_ref, cv_ref):
        n = pl.program_id(1)

        @pl.when(n == 0)
        def _():
            ck_ref[...] = jnp.zeros_like(ck_ref)
            cv_ref[...] = jnp.zeros_like(cv_ref)

        @pl.when(n < nb)
        def _():
            mask_c, mask_p = _band_masks(n > 0)
            q, kc, kp, vc, vp, dov = q_ref[...], kc_ref[...], kp_ref[...], vc_ref[...], vp_ref[...], do_ref[...]
            lb, db = l_ref[...], dl_ref[...]
            zero = jnp.zeros((BLK, GROUP_W), F32)
            dq_acc, dkc, dkp, dvc, dvp = zero, zero, zero, zero, zero
            for hm in _head_masks((BLK, GROUP_W)):
                qm = jnp.where(hm, q, jnp.zeros_like(q))
                dom = jnp.where(hm, dov, jnp.zeros_like(dov))
                lh = jnp.max(jnp.where(hm, lb, NEG_INF), axis=1, keepdims=True)
                delta = jnp.max(jnp.where(hm, db, NEG_INF), axis=1, keepdims=True)
                pc = jnp.exp(jnp.where(mask_c, _dot_nt(qm, kc) * scale, NEG_INF) - lh)
                pp = jnp.exp(jnp.where(mask_p, _dot_nt(qm, kp) * scale, NEG_INF) - lh)
                dsc = (pc * (_dot_nt(dom, vc) - delta) * scale).astype(BF16)
                dsp = (pp * (_dot_nt(dom, vp) - delta) * scale).astype(BF16)
                dq_acc = jnp.where(hm, _dot(dsc, kc) + _dot(dsp, kp), dq_acc)
                dkc += _dot_tn(dsc, qm)
                dkp += _dot_tn(dsp, qm)
                dvc += _dot_tn(pc.astype(BF16), dom)
                dvp += _dot_tn(pp.astype(BF16), dom)
            dq_ref[...] = dq_acc.astype(dq_ref.dtype)
            dk_ref[...] = (ck_ref[...] + dkp).astype(dk_ref.dtype)
            dv_ref[...] = (cv_ref[...] + dvp).astype(dv_ref.dtype)
            ck_ref[...] = dkc
            cv_ref[...] = dvc

        @pl.when(n == nb)
        def _():
            dk_ref[...] = ck_ref[...].astype(dk_ref.dtype)
            dv_ref[...] = cv_ref[...].astype(dv_ref.dtype)

    clamp = lambda n: jnp.minimum(n, nb - 1)
    qspec = pl.BlockSpec(ATTN_BLK, lambda r, n: (r, clamp(n), 0))
    kspec = pl.BlockSpec(ATTN_BLK, lambda r, n: (r, jnp.maximum(n - 1, 0), 0))
    return pl.pallas_call(
        body, name=name, grid=(dil, nb + 1), in_specs=_attn_specs(clamp) + [qspec, qspec, qspec],
        out_specs=[qspec, kspec, kspec], out_shape=[jax.ShapeDtypeStruct(q.shape, BF16)] * 3,
        scratch_shapes=[pltpu.VMEM((BLK, GROUP_W), F32), pltpu.VMEM((BLK, GROUP_W), F32)], compiler_params=_params(2),
    )(q, k, k, v, v, do, delta, lse)


CONV_TS = 128
HALO = 32
SHIFT_ROWS = CONV_TS + HALO - 8


def _make_shifts(buf, sh):
    for s_ in range(1, 8):
        sh[s_ - 1] = buf[pl.ds(s_, SHIFT_ROWS), :]


def _window(buf, sh, off, ts, cols):
    q, s_ = divmod(off, 8)
    if s_ == 0:
        return buf[pl.ds(off, ts), cols]
    return sh[s_ - 1, pl.ds(8 * q, ts), cols]


def _conv_fwd(name, u, w, b, lg, lb):
    s = u.shape[0]
    ts, per = CONV_TS, CONV_TS // HALO

    def body(a_ref, g_ref, ap_ref, gp_ref, w_ref, b_ref, lg_ref, lb_ref, c_ref, act_ref, buf, cbuf, sh):
        i = pl.program_id(0)
        buf[pl.ds(HALO, ts), :] = a_ref[...].astype(F32) * _sigmoid(g_ref[...].astype(F32))
        prev = ap_ref[...].astype(F32) * _sigmoid(gp_ref[...].astype(F32))
        buf[pl.ds(0, HALO), :] = jnp.where(i > 0, prev, 0.0)
        _make_shifts(buf, sh)
        rb = ts // 2
        for lc in range(D // LANES):
            cols = pl.ds(lc * LANES, LANES)
            for r0 in range(0, ts, rb):
                acc = jnp.broadcast_to(b_ref[:, cols], (rb, LANES))
                for j in range(CONV_K):
                    acc = acc + w_ref[pl.ds(j, 1), cols] * _window(buf, sh, r0 + HALO - (CONV_K - 1) + j, rb, cols)
                cbuf[pl.ds(r0, rb), cols] = acc
        c = cbuf[...]
        mu = jnp.mean(c, axis=-1, keepdims=True)
        xc = c - mu
        ln = xc * lax.rsqrt(jnp.mean(xc * xc, axis=-1, keepdims=True) + EPS) * lg_ref[...] + lb_ref[...]
        c_ref[...] = c.astype(c_ref.dtype)
        act_ref[...] = (ln * _sigmoid(ln)).astype(act_ref.dtype)

    halo = lambda cb: pl.BlockSpec((HALO, D), functools.partial(lambda i, cb: (jnp.maximum(i * per - 1, 0), cb), cb=cb))
    vec = pl.BlockSpec((1, D), lambda i: (0, 0))
    return pl.pallas_call(
        body, name=name, grid=(s // ts,),
        in_specs=[pl.BlockSpec((ts, D), lambda i: (i, 0)), pl.BlockSpec((ts, D), lambda i: (i, 1)), halo(0), halo(1),
                  pl.BlockSpec((HALO, D), lambda i: (0, 0)), vec, vec, vec],
        out_specs=[pl.BlockSpec((ts, D), lambda i: (i, 0))] * 2,
        out_shape=[jax.ShapeDtypeStruct((s, D), BF16)] * 2,
        scratch_shapes=[pltpu.VMEM((ts + HALO, D), F32), pltpu.VMEM((ts, D), F32), pltpu.VMEM((7, SHIFT_ROWS, D), F32)],
        compiler_params=_params(1),
    )(u, u, u, u, w, b, lg, lb)


def _ln_bwd(name, dact, c, lg, lb):
    def body(rows, vecs):
        (dv, cv), (g, b) = rows, vecs
        cv = cv.astype(F32)
        mu = jnp.mean(cv, axis=-1, keepdims=True)
        xc = cv - mu
        rstd = lax.rsqrt(jnp.mean(xc * xc, axis=-1, keepdims=True) + EPS)
        xh = xc * rstd
        ln = xh * g + b
        sg = _sigmoid(ln)
        dln = dv * (sg * (1.0 + ln * (1.0 - sg)))
        dxh = dln * g
        dc = rstd * (dxh - jnp.mean(dxh, axis=-1, keepdims=True) - xh * jnp.mean(dxh * xh, axis=-1, keepdims=True))
        return [dc], [_colsum(dln * xh), _colsum(dln)]

    return _rowwise(name, body, [dact, c], [lg, lb], [(D, F32)], [(1, D), (1, D)])


def _conv_bwd(name, dc, u, w):
    s = u.shape[0]
    ts, per = CONV_TS, CONV_TS // HALO
    n_t = s // ts

    def body(dc_ref, dn_ref, a_ref, g_ref, ap_ref, gp_ref, w_ref, du_ref, dw_ref, db_ref, buf, dbuf, hbuf, sh, dsh):
        i = pl.program_id(0)
        a = a_ref[...].astype(F32)
        sg = _sigmoid(g_ref[...].astype(F32))
        buf[pl.ds(HALO, ts), :] = a * sg
        prev = ap_ref[...].astype(F32) * _sigmoid(gp_ref[...].astype(F32))
        buf[pl.ds(0, HALO), :] = jnp.where(i > 0, prev, 0.0)
        dcv = dc_ref[...]
        dbuf[pl.ds(0, ts), :] = dcv
        dbuf[pl.ds(ts, HALO), :] = jnp.where(i < n_t - 1, dn_ref[...], 0.0)

        @pl.when(i == 0)
        def _():
            dw_ref[...] = jnp.zeros_like(dw_ref)
            db_ref[...] = jnp.zeros_like(db_ref)

        db_ref[...] += _colsum(dcv)
        _make_shifts(buf, sh)
        _make_shifts(dbuf, dsh)
        rb = ts // 4
        for lc in range(D // LANES):
            cols = pl.ds(lc * LANES, LANES)
            for r0 in range(0, ts, rb):
                d0 = dbuf[pl.ds(r0, rb), cols]
                acc = jnp.zeros((rb, LANES), F32)
                for j in range(CONV_K):
                    acc = acc + w_ref[pl.ds(j, 1), cols] * _window(dbuf, dsh, r0 + CONV_K - 1 - j, rb, cols)
                    part = d0 * _window(buf, sh, r0 + HALO - (CONV_K - 1) + j, rb, cols)
                    dw_ref[pl.ds(8 * j, 8), cols] += jnp.sum(part.reshape(rb // 8, 8, LANES), axis=0)
                hbuf[pl.ds(r0, rb), cols] = acc
        dh = hbuf[...]
        du_ref[:, pl.ds(0, D)] = (dh * sg).astype(du_ref.dtype)
        du_ref[:, pl.ds(D, D)] = (dh * a * sg * (1.0 - sg)).astype(du_ref.dtype)

    halo = lambda cb: pl.BlockSpec((HALO, D), functools.partial(lambda i, cb: (jnp.maximum(i * per - 1, 0), cb), cb=cb))
    nxt = pl.BlockSpec((HALO, D), lambda i: (jnp.minimum((i + 1) * per, s // HALO - 1), 0))
    return pl.pallas_call(
        body, name=name, grid=(n_t,),
        in_specs=[pl.BlockSpec((ts, D), lambda i: (i, 0)), nxt, pl.BlockSpec((ts, D), lambda i: (i, 0)),
                  pl.BlockSpec((ts, D), lambda i: (i, 1)), halo(0), halo(1), pl.BlockSpec((HALO, D), lambda i: (0, 0))],
        out_specs=[pl.BlockSpec((ts, 2 * D), lambda i: (i, 0)), pl.BlockSpec((8 * CONV_K, D), lambda i: (0, 0)),
                   pl.BlockSpec((1, D), lambda i: (0, 0))],
        out_shape=[jax.ShapeDtypeStruct((s, 2 * D), BF16), jax.ShapeDtypeStruct((8 * CONV_K, D), F32),
                   jax.ShapeDtypeStruct((1, D), F32)],
        scratch_shapes=[pltpu.VMEM((ts + HALO, D), F32), pltpu.VMEM((ts + HALO, D), F32), pltpu.VMEM((ts, D), F32),
                        pltpu.VMEM((7, SHIFT_ROWS, D), F32), pltpu.VMEM((7, SHIFT_ROWS, D), F32)],
        compiler_params=_params(1),
    )(dc, dc, u, u, u, u, w)


def _mix_fwd(tag, x1, ng, sc, sh, gate, wts, cos2, sin2, job=None):
    h1 = _modulate(f"{tag}_mod", x1, ng, sc, sh)
    ident = lambda accs, _t, _v: accs
    (qkv,), got = _with_job(_mm(f"{tag}_qkv", [(h1, wts["w_qkv"])], ident, [BF16], job=job), job)
    u = _mm(f"{tag}_u", [(h1, wts["w_u"])], ident, [BF16], tn=D)[0]
    ga, gc = _mm(f"{tag}_gates", [(h1, wts["w_ga"]), (h1, wts["w_gc"])], ident, [BF16] * 2)
    qd, kd, vd = _rope_split(f"{tag}_rope", qkv, cos2, sin2)
    per_group = [_attn_fwd(f"{tag}_attn{g}", qd[g], kd[g], vd[g]) for g in range(len(ATTN_DILATIONS))]
    o, lse = _attn_merge(f"{tag}_merge", [p[0] for p in per_group], [p[1] for p in per_group])
    cpre, act = _conv_fwd(f"{tag}_conv", u, wts["conv_w"], wts["conv_b"], wts["ln_g"], wts["ln_b"])

    def gate_epi(accs, tl, _v):
        ya, yc = accs
        return [_sigmoid(tl[0].astype(F32)) * ya + _sigmoid(tl[1].astype(F32)) * yc, ya, yc]

    y, ya, yc = _mm(f"{tag}_branch", [(o, wts["attn_wo"]), (act, wts["conv_wo"])], gate_epi, [BF16] * 3,
                    tiles=[(ga, 0), (gc, 0)])

    def res_epi(accs, tl, vs):
        return [tl[0] + vs[0] * accs[0], accs[0]]

    x2, f1 = _mm(f"{tag}_out", [(y, wts["w_out"])], res_epi, [F32, BF16], tiles=[(x1, 0)], vecs=[gate])
    return x2, (x1, h1, u, ga, gc, qd, kd, vd, o, lse, cpre, act, y, ya, yc, f1), got


def _mix_bwd(tag, dx2, saved, ng, sc, gate, wts, cos2, sin2, exchange_job):
    x1, h1, u, ga, gc, qd, kd, vd, o, lse, cpre, act, y, ya, yc, f1 = saved
    dfm, dgate = _gate_scale(f"{tag}_gs", dx2, f1, gate, 1.0)

    def epi(accs, tl, _v):
        dy = accs[0]
        sa, sc_ = _sigmoid(tl[0].astype(F32)), _sigmoid(tl[1].astype(F32))
        return [dy * sa, dy * sc_, dy * tl[2].astype(F32) * sa * (1.0 - sa), dy * tl[3].astype(F32) * sc_ * (1.0 - sc_)]

    dya, dyc, dga, dgc = _mm(f"{tag}_bout", [(dfm, wts["w_out"])], epi, [BF16] * 4, nt=True,
                             tiles=[(ga, 0), (gc, 0), (ya, 0), (yc, 0)], chunk=2 * LANES)
    grads = {"w_out": _mm_tn(f"{tag}_dwout", y, dfm), "attn_wo": _mm_tn(f"{tag}_dwattn", o, dya),
             "conv_wo": _mm_tn(f"{tag}_dwconv", act, dyc)}
    ident = lambda accs, _t, _v: accs
    do = _mm(f"{tag}_battn", [(dya, wts["attn_wo"])], ident, [BF16], nt=True)[0]
    dact = _mm(f"{tag}_bconv", [(dyc, wts["conv_wo"])], ident, [F32], nt=True)[0]
    dc, dlg, dlb = _ln_bwd(f"{tag}_lnb", dact, cpre, wts["ln_g"], wts["ln_b"])
    du, dw8, dcb = _conv_bwd(f"{tag}_convb", dc, u, wts["conv_w"])
    dod, deltad, lsed = _grad_split(f"{tag}_gsplit", do, o, lse)
    dqs, dks, dvs = [], [], []
    for g in range(len(ATTN_DILATIONS)):
        dq, dk, dv = _attn_bwd(f"{tag}_attnb{g}", qd[g], kd[g], vd[g], dod[g], deltad[g], lsed[g])
        dqs.append(dq); dks.append(dk); dvs.append(dv)
    dqkv = _rope_join(f"{tag}_ropeb", dqs, dks, dvs, cos2, sin2)
    grads["w_in"] = jnp.concatenate([_mm_tn(f"{tag}_dwqkv", h1, dqkv), _mm_tn(f"{tag}_dwu", h1, du),
                                     _mm_tn(f"{tag}_dwga", h1, dga), _mm_tn(f"{tag}_dwgc", h1, dgc)], axis=1)
    job = exchange_job([_split(grads[k], ax) for k, ax in zip(MX_NAMES, MX_AXES)])
    (dx1, dsh, dsc, dng), got = _mm(
        f"{tag}_bin", [(dqkv, wts["w_qkv"]), (du, wts["w_u"]), (dga, wts["w_ga"]), (dgc, wts["w_gc"])],
        _modulate_bwd_epi, [F32], nt=True, tiles=[(x1, 0), (dx2, 0)], vecs=[ng, sc], n_sums=3, tm=BWD_LAST_TM, job=job)
    small = {"conv_w": dw8.reshape(CONV_K, 8, D).sum(axis=1), "conv_b": dcb, "ln_g": dlg, "ln_b": dlb}
    return dx1, (dsh, dsc, dgate, dng), small, got


def _loss_head(name, x, target, fg):
    def body(rows, vecs):
        (xv, tv), (g,) = rows, vecs
        xhat, r = _rms_parts(xv)
        err = xhat * g - tv
        dy = err * (1.0 / D)
        return [_rms_bwd(dy * g, xhat, r)], [_colsum(err * err), _colsum(dy * xhat)]

    return _rowwise(name, body, [x, target], [fg], [(D, F32)], [(1, D), (1, D)])


FF_NAMES, FF_AXES = ("ffn_wg", "ffn_wu", "ffn_wd"), (1, 1, 0)
MX_NAMES, MX_AXES = ("w_in", "attn_wo", "conv_wo", "w_out"), (1, 1, 0, 0)
GROUPS = (("ff", 0, 0), ("mx", 0), ("ff", 0, 1), ("ff", 1, 0), ("mx", 1), ("ff", 1, 1))


def _group_name(grp):
    return "_".join(str(p) for p in grp)


def _mix_weights(blocks, small):
    w_in, attn_wo, conv_wo, w_out = [_join(b, a) for b, a in zip(blocks, MX_AXES)]
    return dict(small, w_qkv=w_in[:, :3 * QKV_W], w_u=w_in[:, 3 * QKV_W:3 * QKV_W + 2 * D],
                w_ga=w_in[:, 3 * QKV_W + 2 * D:3 * QKV_W + 3 * D], w_gc=w_in[:, 3 * QKV_W + 3 * D:],
                attn_wo=attn_wo, conv_wo=conv_wo, w_out=w_out)


def _local_step(x, target, mod, norm_g, shards, small_w, final_g, gather_job, exchange_job):
    cos2, sin2 = _rope_tables(x.shape[0])
    row = lambda a: a[None, :]
    ng = lambda l, i: row(norm_g[l, i])
    m = lambda l, i: row(mod[l, i])
    blocks = _run_job("gather_" + _group_name(GROUPS[0]), gather_job(shards[GROUPS[0]]))
    saved, wts = [], []
    for n, grp in enumerate(GROUPS):
        nxt = gather_job(shards[GROUPS[n + 1]]) if n + 1 < len(GROUPS) else None
        l = grp[1]
        if grp[0] == "ff":
            i = 2 * grp[2]
            wts.append([_join(b, a) for b, a in zip(blocks, FF_AXES)])
            x, sv, blocks = _ffn_fwd(f"l{l}f{grp[2]}", x, ng(l, i), m(l, 3 * i + 1), m(l, 3 * i), m(l, 3 * i + 2), *wts[-1], job=nxt)
        else:
            wts.append(_mix_weights(blocks, small_w[l]))
            x, sv, blocks = _mix_fwd(f"l{l}mx", x, ng(l, 1), m(l, 4), m(l, 3), m(l, 5), wts[-1], cos2, sin2, job=nxt)
        saved.append(sv)
    dx, sq, dfg = _loss_head("loss_head", x, target, row(final_g))
    loss = (0.5 / D) * jnp.sum(sq)
    dmod = [[None] * 3, [None] * 3]
    dng = [[None] * 3, [None] * 3]
    small, recv = [None, None], {}
    for n in reversed(range(len(GROUPS))):
        grp = GROUPS[n]
        l = grp[1]
        if grp[0] == "ff":
            i = 2 * grp[2]
            dx, v, recv[grp] = _ffn_bwd(f"l{l}f{grp[2]}", dx, saved[n], ng(l, i), m(l, 3 * i + 1), m(l, 3 * i + 2), *wts[n], exchange_job)
        else:
            i = 1
            dx, v, small[l], recv[grp] = _mix_bwd(f"l{l}mx", dx, saved[n], ng(l, 1), m(l, 4), m(l, 5), wts[n], cos2, sin2, exchange_job)
        dmod[l][i] = jnp.concatenate(v[:3], axis=0)
        dng[l][i] = v[3]
    dmod = jnp.stack([jnp.concatenate(d, axis=0) for d in dmod])
    small = {k: jnp.stack([small[0][k].reshape(-1, D), small[1][k].reshape(-1, D)]) for k in small[0]}
    small = dict(small, norm_g=jnp.stack([jnp.concatenate(d, axis=0) for d in dng]), final_g=dfg.reshape(D))
    return loss, dx, dmod, small, recv


HBM_SPEC = pl.BlockSpec(memory_space=pl.ANY)


def _place():
    return lax.axis_index("x"), lax.axis_index("y"), lax.axis_index("c")


N_PEERS = N_DEV - 1


def _gather_job(arrays):
    n_a = len(arrays)

    def run(phase, x_refs, out_refs, send_sems, recv_sems, local_sems):
        x, y, c = _place()
        me, sibling = (x, y, c), (x, y, 1 - c)
        chips = [(1 - x, y), (x, 1 - y), (1 - x, 1 - y)]

        def copy(a, k, block, to, from_input=False):
            px, py, pc = block
            rows = out_refs[a].at[4 * px + 2 * py + pc]
            return pltpu.make_async_remote_copy(
                src_ref=x_refs[a] if from_input else rows, dst_ref=rows, send_sem=send_sems.at[a * N_PEERS + k],
                recv_sem=recv_sems.at[a * N_PEERS + k], device_id=to, device_id_type=pl.DeviceIdType.MESH)

        mine = [pltpu.make_async_copy(x_refs[a], out_refs[a].at[4 * x + 2 * y + c], local_sems.at[a]) for a in range(n_a)]
        first = []
        for j, chip in enumerate(chips):
            first += [copy(a, 1 + j, me, (*chip, c), from_input=True) for a in range(n_a)]
        first += [copy(a, 0, me, sibling, from_input=True) for a in range(n_a)]
        if phase == "start":
            for cp in mine + first:
                cp.start()
            return
        passed = []
        for j, chip in enumerate(chips):
            for a in range(n_a):
                copy(a, 1 + j, (*chip, c), me).wait_recv()
                passed.append(copy(a, 4 + j, (*chip, c), sibling))
                passed[-1].start()
        for a in range(n_a):
            copy(a, 0, sibling, me).wait_recv()
        for j, chip in enumerate(chips):
            for a in range(n_a):
                copy(a, 4 + j, (*chip, 1 - c), me).wait_recv()
        for cp in first + passed:
            cp.wait_send()
        for cp in mine:
            cp.wait()

    return dict(arrays=list(arrays), run=run,
                out_shape=[jax.ShapeDtypeStruct((N_DEV, *a.shape), a.dtype) for a in arrays])


def _job_scratch(job):
    n_a = len(job["arrays"])
    return [pltpu.SemaphoreType.DMA((n_a * N_PEERS,)), pltpu.SemaphoreType.DMA((n_a * N_PEERS,)), pltpu.SemaphoreType.DMA((n_a,))]


def _run_job(name, job):
    n_a = len(job["arrays"])

    def body(*refs):
        job["run"]("start", refs[:n_a], refs[n_a:2 * n_a], *refs[2 * n_a:])
        job["run"]("finish", refs[:n_a], refs[n_a:2 * n_a], *refs[2 * n_a:])

    return pl.pallas_call(
        body, name=name, out_shape=job["out_shape"], in_specs=[HBM_SPEC] * n_a, out_specs=[HBM_SPEC] * n_a,
        scratch_shapes=_job_scratch(job),
    )(*job["arrays"])


def _all_gather(name, arrays):
    return _run_job(name, _gather_job(arrays))


def _exchange_job(arrays):
    n_a = len(arrays)

    def run(phase, g_refs, out_refs, send_sems, recv_sems, local_sems):
        x, y, c = _place()
        my = 4 * x + 2 * y + c
        copies = [pltpu.make_async_copy(g_refs[a].at[my], out_refs[a].at[my], local_sems.at[a]) for a in range(n_a)]
        for k in (4, 2, 6, 1, 5, 3, 7):
            px = 1 - x if k & 4 else x
            py = 1 - y if k & 2 else y
            pc = 1 - c if k & 1 else c
            for a in range(n_a):
                copies.append(pltpu.make_async_remote_copy(
                    src_ref=g_refs[a].at[4 * px + 2 * py + pc], dst_ref=out_refs[a].at[my],
                    send_sem=send_sems.at[a * N_PEERS + k - 1], recv_sem=recv_sems.at[a * N_PEERS + k - 1],
                    device_id=(px, py, pc), device_id_type=pl.DeviceIdType.MESH))
        for cp in copies:
            if phase == "start":
                cp.start()
            else:
                cp.wait()

    return dict(arrays=list(arrays), run=run, out_shape=[jax.ShapeDtypeStruct(a.shape, a.dtype) for a in arrays])


SLAB_TS = 2048


def _sum_parts(name, parts):
    def body(rows, _v):
        tot = rows[0].astype(F32)
        for r in rows[1:]:
            tot = tot + r.astype(F32)
        return [tot], []

    return _rowwise(name, body, list(parts), [], [(parts[0].shape[1], F32)], ts=SLAB_TS)[0]


def _adamw(name, w, parts, m, v):
    def body(rows, _v):
        wv, mv, vv = rows[0], rows[1], rows[2]
        g = rows[3].astype(F32)
        for r in rows[4:]:
            g = g + r.astype(F32)
        m2 = ADAM_B1 * mv + (1.0 - ADAM_B1) * g
        v2 = ADAM_B2 * vv + (1.0 - ADAM_B2) * (g * g)
        m_hat = m2 / (1.0 - ADAM_B1 ** ADAM_STEP)
        v_hat = v2 / (1.0 - ADAM_B2 ** ADAM_STEP)
        delta = -ADAM_LR * (m_hat / (jnp.sqrt(v_hat) + ADAM_EPS) + ADAM_WD * wv)
        return [g, delta, m2, v2], []

    width = w.shape[1]
    return _rowwise(name, body, [w, m, v, *parts], [], [(width, F32)] * 4, ts=max(16, SLAB_TS * LANES // width))


def _to_slab(arrays, dtype):
    flat = jnp.concatenate([a.reshape(-1).astype(dtype) for a in arrays])
    rows = -(-flat.shape[0] // LANES)
    rows = -(-rows // 8) * 8 if rows <= SLAB_TS else -(-rows // SLAB_TS) * SLAB_TS
    return jnp.pad(flat, (0, rows * LANES - flat.shape[0])).reshape(rows, LANES)


def _from_slab(slab, shapes, lead=()):
    flat = slab.reshape(*lead, -1)
    out, at = [], 0
    for shp in shapes:
        size = 1
        for d in shp:
            size *= d
        out.append(flat[..., at:at + size].reshape(*lead, *shp))
        at += size
    return out


def _join(blocks, axis):
    full = jnp.moveaxis(blocks, 0, axis)
    return full.reshape(*full.shape[:axis], -1, *full.shape[axis + 2:])


def _split(full, axis):
    shp = full.shape
    return jnp.moveaxis(full.reshape(*shp[:axis], N_DEV, shp[axis] // N_DEV, *shp[axis + 1:]), axis, 0)


def kernel(x, c, ada_w, ada_b, norm_g, ffn_wg, ffn_wu, ffn_wd, w_in, attn_wo, conv_w, conv_b, conv_ln_g, conv_ln_b, conv_wo, w_out, final_g, loss_target, m_ada_w, m_ada_b, m_norm_g, m_ffn_wg, m_ffn_wu, m_ffn_wd, m_w_in, m_attn_wo, m_conv_w, m_conv_b, m_conv_ln_g, m_conv_ln_b, m_conv_wo, m_w_out, m_final_g, v_ada_w, v_ada_b, v_norm_g, v_ffn_wg, v_ffn_wu, v_ffn_wd, v_w_in, v_attn_wo, v_conv_w, v_conv_b, v_conv_ln_g, v_conv_ln_b, v_conv_wo, v_w_out, v_final_g):
    px, py, pc = _place()
    me = 4 * px + 2 * py + pc
    n_mod = ada_w.shape[2]
    big_w = dict(ffn_wg=ffn_wg, ffn_wu=ffn_wu, ffn_wd=ffn_wd, w_in=w_in, attn_wo=attn_wo, conv_wo=conv_wo, w_out=w_out)
    big_m = dict(ffn_wg=m_ffn_wg, ffn_wu=m_ffn_wu, ffn_wd=m_ffn_wd, w_in=m_w_in, attn_wo=m_attn_wo, conv_wo=m_conv_wo, w_out=m_w_out)
    big_v = dict(ffn_wg=v_ffn_wg, ffn_wu=v_ffn_wu, ffn_wd=v_ffn_wd, w_in=v_w_in, attn_wo=v_attn_wo, conv_wo=v_conv_wo, w_out=v_w_out)

    small_in = [c, norm_g, conv_w]
    g1 = _all_gather("gather_small", [_to_slab(small_in, F32)])[0]
    c_all, ng_blocks, cw_blocks = _from_slab(g1, [a.shape for a in small_in], lead=(N_DEV,))
    c_all = c_all.reshape(N_DEV, D)
    norm_g_full = _join(ng_blocks, 2)
    conv_w_full = _join(cw_blocks, 2)
    as2d = lambda a: a.reshape(-1, a.shape[-1])

    c_act = _rowwise("cond_silu", lambda rows, _v: ([rows[0] * _sigmoid(rows[0])], []), [c_all], [], [(D, BF16)])[0]
    c_pad = jnp.pad(c_act, ((0, LANES - N_DEV), (0, 0)))
    mod_cols = []
    for l in range(2):
        bias = lax.dynamic_slice_in_dim(ada_b[l], me * n_mod, n_mod)[None, :]
        out = _mm(f"mod{l}", [(c_pad, ada_w[l].astype(BF16))], lambda accs, _t, vs: [accs[0] + vs[0]], [F32], vecs=[bias])[0]
        mod_cols.append(out[:N_DEV])
    g2 = _all_gather("gather_mod", [_to_slab([jnp.stack(mod_cols)], F32)])[0]
    mod_all = _from_slab(g2, [(2, N_DEV, n_mod)], lead=(N_DEV,))[0]
    mod = lax.dynamic_index_in_dim(mod_all, me, axis=2, keepdims=False)
    mod = jnp.moveaxis(mod, 0, 1).reshape(2, 9, D)

    index = {grp: (grp[1], grp[2]) if grp[0] == "ff" else (grp[1],) for grp in GROUPS}
    names = {grp: FF_NAMES if grp[0] == "ff" else MX_NAMES for grp in GROUPS}
    shards = {grp: [big_w[k][index[grp]].astype(BF16) for k in names[grp]] for grp in GROUPS}
    small_l = [dict(conv_w=jnp.pad(conv_w_full[l], ((0, HALO - CONV_K), (0, 0))), conv_b=conv_b[l][None, :],
                    ln_g=conv_ln_g[l][None, :], ln_b=conv_ln_b[l][None, :]) for l in range(2)]

    loss, dx, dmod, small, recv = _local_step(x[0], loss_target[0], mod, norm_g_full, shards, small_l, final_g,
                                              _gather_job, _exchange_job)
    loss = lax.psum(loss, MESH_AXES)

    small_names = ["norm_g", "conv_w", "conv_b", "ln_g", "ln_b", "final_g"]
    small_parts = [dmod] + [small[k] for k in small_names]
    g3 = _all_gather("gather_small_grads", [_to_slab(small_parts, F32)])[0]
    tot = _sum_parts("sum_small_grads", [g3[k] for k in range(N_DEV)])
    _, g_ng, g_cw, g_cb, g_lg, g_lb, g_fg = _from_slab(tot, [a.shape for a in small_parts])
    g_ab = _from_slab(tot, [(2, 9 * D)])[0]
    dmod_all = _from_slab(g3, [dmod.shape], lead=(N_DEV,))[0].reshape(N_DEV, 2, 9 * D)
    dm_mine = lax.dynamic_slice_in_dim(dmod_all, me * n_mod, n_mod, axis=2)
    g_aw = jnp.stack([
        _mm_tn(f"dada_w{l}", c_pad, jnp.pad(dm_mine[:, l], ((0, LANES - N_DEV), (0, 0))).astype(BF16), out_dtype=F32)
        for l in range(2)])
    cols = lambda a: lax.dynamic_slice_in_dim(a, me * (D // N_DEV), D // N_DEV, axis=2)
    small_w = [ada_b, norm_g, conv_w, conv_b, conv_ln_g, conv_ln_b, final_g]
    small_m = [m_ada_b, m_norm_g, m_conv_w, m_conv_b, m_conv_ln_g, m_conv_ln_b, m_final_g]
    small_v = [v_ada_b, v_norm_g, v_conv_w, v_conv_b, v_conv_ln_g, v_conv_ln_b, v_final_g]
    small_g = [g_ab, cols(g_ng), cols(g_cw), g_cb, g_lg, g_lb, g_fg]
    s_shapes = [a.shape for a in small_w]
    s_out = _adamw("adamw_small", _to_slab(small_w, F32), [_to_slab(small_g, F32)], _to_slab(small_m, F32), _to_slab(small_v, F32))
    aw_out = [o.reshape(ada_w.shape) for o in _adamw("adamw_ada_w", as2d(ada_w), [as2d(g_aw)], as2d(m_ada_w), as2d(v_ada_w))]

    upd = {}
    for grp in GROUPS:
        for k, blocks in zip(names[grp], recv[grp]):
            parts = [(blocks, blocks.shape[2], 0, j) for j in range(N_DEV)]
            at = index[grp]
            upd[k, at] = _adamw(f"adamw_{k}_{_group_name(grp)}", big_w[k][at], parts, big_m[k][at], big_v[k][at])

    def stacked(k, i):
        if k in FF_NAMES:
            return jnp.stack([jnp.stack([upd[k, (l, j)][i] for j in range(2)]) for l in range(2)])
        return jnp.stack([upd[k, (l,)][i] for l in range(2)])

    def ordered(i):
        ab, ng, cw, cb, lg, lb, fg = _from_slab(s_out[i], s_shapes)
        bg = {k: stacked(k, i) for k in FF_NAMES + MX_NAMES}
        return [aw_out[i], ab, ng, bg["ffn_wg"], bg["ffn_wu"], bg["ffn_wd"], bg["w_in"], bg["attn_wo"], cw, cb, lg, lb,
                bg["conv_wo"], bg["w_out"], fg]

    return (loss, dx[None], *ordered(0), *ordered(1), *ordered(2), *ordered(3))
```

```python
import functools

import jax
import jax.numpy as jnp
from jax import lax
from jax.experimental import pallas as pl
from jax.experimental.pallas import tpu as pltpu

F32 = jnp.float32
BF16 = jnp.bfloat16

N_DEV = 8
D = 1024
D_FF = 2816
HEAD_DIM = 64
GROUP_W = 256
ATTN_DILATIONS = (1, 4, 16)
BLK = 128
QKV_W = 768
CONV_K = 31
ROPE_THETA = 10000.0
EPS = 1e-6
NEG_INF = -1e30
ADAM_LR, ADAM_B1, ADAM_B2, ADAM_EPS, ADAM_WD, ADAM_STEP = 0.001, 0.9, 0.999, 1e-08, 0.01, 10

V7X_VMEM_BYTES = 64 * 1024 * 1024
VMEM_LIMIT = V7X_VMEM_BYTES - 8 * 1024 * 1024
LANES = 128
MESH_AXES = ("x", "y", "c")


def _params(n_grid):
    return pltpu.CompilerParams(vmem_limit_bytes=VMEM_LIMIT, dimension_semantics=("arbitrary",) * n_grid)


def _sigmoid(v):
    return 1.0 / (1.0 + jnp.exp(-v))


def _mm(name, prods, epilogue, out_dtypes, *, nt=False, tiles=(), vecs=(), tm=512, tn=None, a_pre=None, chunk=None,
        job=None, n_sums=0):
    s = prods[0][0].shape[0]
    n = prods[0][1].shape[0] if nt else prods[0][1].shape[1]
    tn = n if tn is None else tn
    tm = min(tm, s)
    assert s % tm == 0 and n % tn == 0
    n_p, n_t, n_v = len(prods), len(tiles), len(vecs)
    dn = (((1,), (1,)), ((), ())) if nt else (((1,), (0,)), ((), ()))

    chunk = tn if chunk is None else chunk
    bounds = [(c0, min(chunk, tn - c0)) for c0 in range(0, tn, chunk)]

    n_o = len(out_dtypes) + n_sums
    n_j = len(job["arrays"]) if job else 0
    n_steps = (n // tn, s // tm)

    def body(*refs):
        p_refs, rest = refs[:2 * n_p], refs[2 * n_p:]
        t_refs, v_refs, rest = rest[:n_t], rest[n_t:n_t + n_v], rest[n_t + n_v:]
        j_in, o_refs, j_out, sems = rest[:n_j], rest[n_j:n_j + n_o], rest[n_j + n_o:2 * n_j + n_o], rest[2 * n_j + n_o:]
        o_refs, s_refs = o_refs[:n_o - n_sums], o_refs[n_o - n_sums:]
        if job:
            @pl.when(jnp.logical_and(pl.program_id(0) == 0, pl.program_id(1) == 0))
            def _():
                job["run"]("start", j_in, j_out, *sems)
        lhs = []
        for p in range(n_p):
            a = p_refs[2 * p][...]
            lhs.append(a if a_pre is None else a_pre(a))
        for c0, cw in bounds:
            cols = pl.ds(c0, cw)
            accs = []
            for p in range(n_p):
                b = p_refs[2 * p + 1][cols, :] if nt else p_refs[2 * p + 1][:, cols]
                accs.append(lax.dot_general(lhs[p], b, dn, preferred_element_type=F32))
            outs = epilogue(accs, [t[:, cols] for t in t_refs], [v[:, cols] for v in v_refs])
            for o_ref, o in zip(o_refs, outs[:len(o_refs)], strict=True):
                o_ref[:, cols] = o.astype(o_ref.dtype)
            if n_sums:
                first_row_tile = pl.program_id(1) == 0

                @pl.when(first_row_tile)
                def _():
                    for s_ref, part in zip(s_refs, outs[len(o_refs):], strict=True):
                        s_ref[:, cols] = part

                @pl.when(jnp.logical_not(first_row_tile))
                def _():
                    for s_ref, part in zip(s_refs, outs[len(o_refs):], strict=True):
                        s_ref[:, cols] += part
        if job:
            @pl.when(jnp.logical_and(pl.program_id(0) == n_steps[0] - 1, pl.program_id(1) == n_steps[1] - 1))
            def _():
                job["run"]("finish", j_in, j_out, *sems)

    in_specs = []
    operands = []
    for a, b in prods:
        k = a.shape[1]
        in_specs.append(pl.BlockSpec((tm, k), lambda j, i: (i, 0)))
        in_specs.append(pl.BlockSpec((tn, k), lambda j, i: (j, 0)) if nt else pl.BlockSpec((k, tn), lambda j, i: (0, j)))
        operands += [a, b]
    for arr, off in tiles:
        in_specs.append(pl.BlockSpec((tm, tn), functools.partial(lambda j, i, off: (i, j + off), off=off)))
        operands.append(arr)
    for v in vecs:
        in_specs.append(pl.BlockSpec((1, tn), lambda j, i: (0, j)))
        operands.append(v)
    out_specs = [pl.BlockSpec((tm, tn), lambda j, i: (i, j)) for _ in out_dtypes]
    out_specs += [pl.BlockSpec((1, tn), lambda j, i: (0, j)) for _ in range(n_sums)]
    out_shape = [jax.ShapeDtypeStruct((s, n), dt) for dt in out_dtypes] + [jax.ShapeDtypeStruct((1, n), F32)] * n_sums
    scratch = []
    if job:
        in_specs += [HBM_SPEC] * n_j
        operands += job["arrays"]
        out_specs += [HBM_SPEC] * n_j
        out_shape += job["out_shape"]
        scratch = _job_scratch(job)
    out = pl.pallas_call(
        body, name=name, grid=n_steps, in_specs=in_specs, out_specs=out_specs, out_shape=out_shape,
        scratch_shapes=scratch, compiler_params=_params(2),
    )(*operands)
    return (out[:n_o], out[n_o:]) if job else out


def _mm_tn(name, a, b, *, tk=512, tn=None, out_dtype=BF16):
    s, m = a.shape
    n = b.shape[1]
    tn = n if tn is None else tn
    tk = min(tk, s)
    n_k = s // tk
    assert s % tk == 0 and n % tn == 0

    def body(a_ref, b_ref, o_ref, acc_ref):
        k = pl.program_id(1)

        @pl.when(k == 0)
        def _():
            acc_ref[...] = jnp.zeros_like(acc_ref)

        acc_ref[...] += lax.dot_general(a_ref[...], b_ref[...], (((0,), (0,)), ((), ())), preferred_element_type=F32)

        @pl.when(k == n_k - 1)
        def _():
            o_ref[...] = acc_ref[...].astype(o_ref.dtype)

    return pl.pallas_call(
        body, name=name, grid=(n // tn, n_k),
        in_specs=[pl.BlockSpec((tk, m), lambda j, k: (k, 0)), pl.BlockSpec((tk, tn), lambda j, k: (k, j))],
        out_specs=pl.BlockSpec((m, tn), lambda j, k: (0, j)),
        out_shape=jax.ShapeDtypeStruct((m, n), out_dtype),
        scratch_shapes=[pltpu.VMEM((m, tn), F32)],
        compiler_params=_params(2),
    )(a, b)


def _pick_rows(s, target):
    if s <= target:
        return s
    return max(t for t in range(16, target + 1, 16) if s % t == 0)


def _rowwise(name, body, rows, vecs, outs, accs=(), *, ts=512):
    rows = [(r if isinstance(r, tuple) else (r, r.shape[1], 0)) for r in rows]
    rows = [r if len(r) == 4 else (*r, None) for r in rows]
    s = rows[0][0].shape[0]
    ts = _pick_rows(s, ts)
    n_r, n_v, n_o, n_a = len(rows), len(vecs), len(outs), len(accs)

    def kbody(*refs):
        r_refs, v_refs = refs[:n_r], refs[n_r:n_r + n_v]
        o_refs, a_refs = refs[n_r + n_v:n_r + n_v + n_o], refs[n_r + n_v + n_o:]
        res_o, res_a = body([r[...] for r in r_refs], [v[...] for v in v_refs])
        for o_ref, o in zip(o_refs, res_o, strict=True):
            o_ref[...] = o.astype(o_ref.dtype)
        if n_a:
            first = pl.program_id(0) == 0

            @pl.when(first)
            def _():
                for a_ref, a in zip(a_refs, res_a, strict=True):
                    a_ref[...] = a

            @pl.when(jnp.logical_not(first))
            def _():
                for a_ref, a in zip(a_refs, res_a, strict=True):
                    a_ref[...] += a

    in_specs = [
        pl.BlockSpec((ts, w), functools.partial(lambda i, cb: (i, cb), cb=cb)) if lead is None else
        pl.BlockSpec((None, ts, w), functools.partial(lambda i, cb, lead: (lead, i, cb), cb=cb, lead=lead))
        for _, w, cb, lead in rows]
    in_specs += [pl.BlockSpec(v.shape, functools.partial(lambda i, nd: (0,) * nd, nd=v.ndim)) for v in vecs]
    out_specs = [pl.BlockSpec((ts, w), lambda i: (i, 0)) for w, _ in outs]
    out_specs += [pl.BlockSpec(shp, functools.partial(lambda i, nd: (0,) * nd, nd=len(shp))) for shp in accs]
    out_shape = [jax.ShapeDtypeStruct((s, w), dt) for w, dt in outs] + [jax.ShapeDtypeStruct(shp, F32) for shp in accs]
    return pl.pallas_call(
        kbody, name=name, grid=(s // ts,), in_specs=in_specs, out_specs=out_specs, out_shape=out_shape,
        compiler_params=_params(1),
    )(*[r[0] for r in rows], *vecs)


def _colsum(v):
    return jnp.sum(v, axis=0, keepdims=True)


def _rms_parts(x):
    r = lax.rsqrt(jnp.mean(x * x, axis=-1, keepdims=True) + EPS)
    return x * r, r


def _rms_bwd(dxhat, xhat, r):
    return r * (dxhat - xhat * jnp.mean(dxhat * xhat, axis=-1, keepdims=True))


def _modulate(name, x, ng, sc, sh):
    def body(rows, vecs):
        (xv,), (g, s_, b) = rows, vecs
        xhat, _ = _rms_parts(xv)
        return [xhat * g * (1.0 + s_) + b], []

    return _rowwise(name, body, [x], [ng, sc, sh], [(D, BF16)])[0]


def _residual_epi(coef, with_next):
    def epi(accs, tl, vs):
        x_out = tl[0] + coef * vs[0] * accs[0]
        outs = [x_out, accs[0]]
        if with_next:
            xhat, _ = _rms_parts(x_out)
            outs.append(xhat * vs[1] * (1.0 + vs[2]) + vs[3])
        return outs

    return epi


def _bwd_last_epi(coef_prev):
    def epi(accs, tl, vs):
        dh = accs[0]
        for acc in accs[1:]:
            dh = dh + acc
        xhat, r = _rms_parts(tl[0])
        dn = dh * (1.0 + vs[1])
        dx_in = tl[1] + _rms_bwd(dn * vs[0], xhat, r)
        outs, sums = [dx_in], [_colsum(dh), _colsum(dh * xhat * vs[0]), _colsum(dn * xhat)]
        if coef_prev is not None:
            outs.append(coef_prev * vs[2] * dx_in)
            sums.append(_colsum(coef_prev * dx_in * tl[2].astype(F32)))
        return outs + sums

    return epi


BWD_LAST_TM = 256


def _bwd_last(name, prods, x, dxo, ng, sc, prev, job):
    tiles, vecs, dts = [(x, 0), (dxo, 0)], [ng, sc], [F32]
    if prev is not None:
        tiles, vecs, dts = tiles + [(prev[0], 0)], vecs + [prev[1]], dts + [BF16]
    outs, got = _mm(name, prods, _bwd_last_epi(None if prev is None else prev[2]), dts, nt=True, tiles=tiles, vecs=vecs,
                    n_sums=3 + (prev is not None), tm=BWD_LAST_TM, job=job)
    if prev is None:
        return outs[0], None, tuple(outs[1:4]), None, got
    return outs[0], outs[1], tuple(outs[2:5]), outs[5], got


def _with_job(result, job):
    return result if job else (result, None)


FF_COEF, MX_COEF = 0.5, 1.0


def _project_out(name, y, w, coef, x, gate, nxt):
    outs = _mm(name, [(y, w)], _residual_epi(coef, nxt is not None), [F32, BF16] + [BF16] * (nxt is not None),
               tiles=[(x, 0)], vecs=[gate, *(nxt or ())])
    return outs[0], outs[1], (outs[2] if nxt is not None else None)


def _ffn_fwd(tag, x, h, gate, wg, wu, wd, nxt, job=None):
    def up_epi(accs, _t, _v):
        a, u = accs
        return [a, u, a * _sigmoid(a) * u]

    (a, u, t), got = _with_job(_mm(f"{tag}_up", [(h, wg), (h, wu)], up_epi, [BF16] * 3, tn=D_FF // 2, job=job), job)
    x_out, f, h_next = _project_out(f"{tag}_down", t, wd, FF_COEF, x, gate, nxt)
    return x_out, h_next, (x, h, a, u, t, f), got


def _ffn_bwd(tag, dxo, df, saved, ng, sc, wg, wu, wd, prev, exchange_job):
    x, h, a, u, t, _f = saved

    def epi(accs, tl, _v):
        dt = accs[0]
        av, uv = tl[0].astype(F32), tl[1].astype(F32)
        sg = _sigmoid(av)
        return [dt * uv * (sg * (1.0 + av * (1.0 - sg))), dt * (av * sg)]

    da, du = _mm(f"{tag}_bdown", [(df, wd)], epi, [BF16] * 2, nt=True, tiles=[(a, 0), (u, 0)], tn=D_FF // 2,
                 chunk=3 * LANES)
    grads = (_mm_tn(f"{tag}_dwg", h, da), _mm_tn(f"{tag}_dwu", h, du), _mm_tn(f"{tag}_dwd", t, df))
    job = exchange_job([_split(g, ax) for g, ax in zip(grads, FF_AXES)])
    return _bwd_last(f"{tag}_bup", [(da, wg), (du, wu)], x, dxo, ng, sc, prev, job)


def _rope_tables(s):
    half = HEAD_DIM // 2
    inv_freq = ROPE_THETA ** (-(jnp.arange(half, dtype=F32) * 2.0 / HEAD_DIM))
    ang = jnp.arange(s, dtype=F32)[:, None] * inv_freq[None, :]
    cos, sin = jnp.cos(ang), jnp.sin(ang)
    return jnp.tile(jnp.concatenate([cos, cos], axis=1), (1, 2)), jnp.tile(jnp.concatenate([-sin, sin], axis=1), (1, 2))


def _rotate(v, cos2, sin2, sign):
    w = v.shape[1]
    lane = lax.broadcasted_iota(jnp.int32, v.shape, 1)
    partner = jnp.where(lane % HEAD_DIM < HEAD_DIM // 2, pltpu.roll(v, w - HEAD_DIM // 2, 1), pltpu.roll(v, HEAD_DIM // 2, 1))
    reps = w // LANES
    return v * jnp.tile(cos2, (1, reps)) + partner * (sign * jnp.tile(sin2, (1, reps)))


SPLIT_TS = 512


def _dilated_spec(dil, ts):
    return pl.BlockSpec((dil, ts // dil, GROUP_W), lambda i: (0, i, 0))


def _dilated_shape(s, dil, dtype):
    return jax.ShapeDtypeStruct((dil, s // dil, GROUP_W), dtype)


CHUNKS_PER_GROUP = GROUP_W // LANES


def _put(buf, chunk0, val):
    for c in range(val.shape[1] // LANES):
        buf[chunk0 + c] = val[:, c * LANES:(c + 1) * LANES]


def _get(buf, chunk0, n):
    return jnp.concatenate([buf[chunk0 + c] for c in range(n)], axis=1)


def _strided_rows(r, dil, ts):
    return pl.ds(r, ts // dil, stride=dil) if dil > 1 else pl.ds(0, ts)


def _deinterleave_one(buf, chunk0, out_ref, dil, ts):
    for half in range(CHUNKS_PER_GROUP):
        for r in range(dil):
            src = buf.at[chunk0 + half][_strided_rows(r, dil, ts), :]
            out_ref.at[r][:, pl.ds(half * LANES, LANES)] = src.astype(out_ref.dtype)


def _interleave_one(in_ref, buf, chunk0, dil, ts):
    for half in range(CHUNKS_PER_GROUP):
        for r in range(dil):
            src = in_ref.at[r][:, pl.ds(half * LANES, LANES)]
            buf.at[chunk0 + half][_strided_rows(r, dil, ts), :] = src.astype(F32)


def _deinterleave(buf, chunk0, out_refs, ts):
    for g, dil in enumerate(ATTN_DILATIONS):
        _deinterleave_one(buf, chunk0 + g * CHUNKS_PER_GROUP, out_refs[g], dil, ts)


def _interleave(in_refs, buf, chunk0, ts):
    for g, dil in enumerate(ATTN_DILATIONS):
        _interleave_one(in_refs[g], buf, chunk0 + g * CHUNKS_PER_GROUP, dil, ts)


def _rope_split(name, qkv, cos2, sin2):
    s, ts = qkv.shape[0], SPLIT_TS

    def body(qkv_ref, c_ref, s_ref, *rest):
        outs, buf = rest[:9], rest[9]
        c2, s2 = c_ref[...], s_ref[...]
        per = QKV_W // LANES
        _put(buf, 0, _rotate(qkv_ref[:, pl.ds(0, QKV_W)].astype(F32), c2, s2, 1.0))
        _put(buf, per, _rotate(qkv_ref[:, pl.ds(QKV_W, QKV_W)].astype(F32), c2, s2, 1.0))
        _put(buf, 2 * per, qkv_ref[:, pl.ds(2 * QKV_W, QKV_W)].astype(F32))
        for t in range(3):
            _deinterleave(buf, t * per, outs[3 * t:3 * t + 3], ts)

    tab = pl.BlockSpec((ts, LANES), lambda i: (i, 0))
    outs = pl.pallas_call(
        body, name=name, grid=(s // ts,), in_specs=[pl.BlockSpec((ts, 3 * QKV_W), lambda i: (i, 0)), tab, tab],
        out_specs=[_dilated_spec(d, ts) for _ in range(3) for d in ATTN_DILATIONS],
        out_shape=[_dilated_shape(s, d, BF16) for _ in range(3) for d in ATTN_DILATIONS],
        scratch_shapes=[pltpu.VMEM((3 * QKV_W // LANES, ts, LANES), F32)], compiler_params=_params(1),
    )(qkv, cos2, sin2)
    return outs[0:3], outs[3:6], outs[6:9]


def _rope_join(name, dq, dk, dv, cos2, sin2):
    s, ts = cos2.shape[0], SPLIT_TS

    def body(*refs):
        ins, c_ref, s_ref, o_ref, buf = refs[:9], refs[9], refs[10], refs[11], refs[12]
        per = QKV_W // LANES
        for t in range(3):
            _interleave(ins[3 * t:3 * t + 3], buf, t * per, ts)
        c2, s2 = c_ref[...], s_ref[...]
        o_ref[:, pl.ds(0, QKV_W)] = _rotate(_get(buf, 0, per), c2, s2, -1.0).astype(o_ref.dtype)
        o_ref[:, pl.ds(QKV_W, QKV_W)] = _rotate(_get(buf, per, per), c2, s2, -1.0).astype(o_ref.dtype)
        o_ref[:, pl.ds(2 * QKV_W, QKV_W)] = _get(buf, 2 * per, per).astype(o_ref.dtype)

    tab = pl.BlockSpec((ts, LANES), lambda i: (i, 0))
    return pl.pallas_call(
        body, name=name, grid=(s // ts,),
        in_specs=[_dilated_spec(d, ts) for _ in range(3) for d in ATTN_DILATIONS] + [tab, tab],
        out_specs=pl.BlockSpec((ts, 3 * QKV_W), lambda i: (i, 0)), out_shape=jax.ShapeDtypeStruct((s, 3 * QKV_W), BF16),
        scratch_shapes=[pltpu.VMEM((3 * QKV_W // LANES, ts, LANES), F32)], compiler_params=_params(1),
    )(*dq, *dk, *dv, cos2, sin2)


def _head_masks(shape):
    lane = lax.broadcasted_iota(jnp.int32, shape, 1)
    return [jnp.logical_and(lane >= h * HEAD_DIM, lane < (h + 1) * HEAD_DIM) for h in range(GROUP_W // HEAD_DIM)]


def _grad_split(name, do, o, lse):
    s, ts = do.shape[0], SPLIT_TS

    def body(do_ref, o_ref, l_ref, *rest):
        outs, buf = rest[:9], rest[9]
        dov = do_ref[...].astype(F32)
        prod = dov * o_ref[...].astype(F32)
        delta = jnp.zeros_like(prod)
        for hm in _head_masks(prod.shape):
            delta = jnp.where(hm, jnp.sum(jnp.where(hm, prod, 0.0), axis=1, keepdims=True), delta)
        _put(buf, 0, dov)
        _put(buf, CHUNKS_PER_GROUP, delta)
        _put(buf, 2 * CHUNKS_PER_GROUP, l_ref[...])
        for t in range(3):
            for g in range(3):
                _deinterleave_one(buf, t * CHUNKS_PER_GROUP, outs[3 * t + g], ATTN_DILATIONS[g], ts)

    nat = pl.BlockSpec((ts, GROUP_W), lambda i: (i, 0))
    dts = [BF16, F32, F32]
    outs = pl.pallas_call(
        body, name=name, grid=(s // ts,), in_specs=[nat, nat, nat],
        out_specs=[_dilated_spec(d, ts) for _ in range(3) for d in ATTN_DILATIONS],
        out_shape=[_dilated_shape(s, d, dt) for dt in dts for d in ATTN_DILATIONS],
        scratch_shapes=[pltpu.VMEM((3 * CHUNKS_PER_GROUP, ts, LANES), F32)], compiler_params=_params(1),
    )(do, o, lse)
    return outs[0:3], outs[3:6], outs[6:9]


def _band_masks(has_prev):
    qi = lax.broadcasted_iota(jnp.int32, (BLK, BLK), 0)
    kj = lax.broadcasted_iota(jnp.int32, (BLK, BLK), 1)
    return kj <= qi, jnp.logical_and(kj >= qi, has_prev)


def _dot_nt(a, b):
    return lax.dot_general(a, b, (((1,), (1,)), ((), ())), preferred_element_type=F32)


def _dot_tn(a, b):
    return lax.dot_general(a, b, (((0,), (0,)), ((), ())), preferred_element_type=F32)


def _dot(a, b):
    return jnp.dot(a, b, preferred_element_type=F32)


ATTN_BLK = (None, BLK, GROUP_W)


def _attn_specs(clamp):
    cur = pl.BlockSpec(ATTN_BLK, lambda r, n: (r, clamp(n), 0))
    prev = pl.BlockSpec(ATTN_BLK, lambda r, n: (r, jnp.maximum(clamp(n) - 1, 0), 0))
    return [cur, cur, prev, cur, prev]


def _attn_fwd(name, q, k, v):
    dil, rows, _ = q.shape
    nb = rows // BLK
    scale = HEAD_DIM ** -0.5

    def body(q_ref, kc_ref, kp_ref, vc_ref, vp_ref, o_ref, l_ref):
        mask_c, mask_p = _band_masks(pl.program_id(1) > 0)
        q, kc, kp, vc, vp = q_ref[...], kc_ref[...], kp_ref[...], vc_ref[...], vp_ref[...]
        o_acc = jnp.zeros((BLK, GROUP_W), F32)
        l_acc = jnp.zeros((BLK, GROUP_W), F32)
        for hm in _head_masks((BLK, GROUP_W)):
            qm = jnp.where(hm, q, jnp.zeros_like(q))
            sc = jnp.where(mask_c, _dot_nt(qm, kc) * scale, NEG_INF)
            sp = jnp.where(mask_p, _dot_nt(qm, kp) * scale, NEG_INF)
            m = jnp.maximum(jnp.max(sc, axis=1, keepdims=True), jnp.max(sp, axis=1, keepdims=True))
            pc, pp = jnp.exp(sc - m), jnp.exp(sp - m)
            den = jnp.sum(pc, axis=1, keepdims=True) + jnp.sum(pp, axis=1, keepdims=True)
            oh = (_dot(pc.astype(BF16), vc) + _dot(pp.astype(BF16), vp)) / den
            o_acc = jnp.where(hm, oh, o_acc)
            l_acc = jnp.where(hm, m + jnp.log(den), l_acc)
        o_ref[...] = o_acc
        l_ref[...] = l_acc

    out_spec = pl.BlockSpec(ATTN_BLK, lambda r, n: (r, n, 0))
    return pl.pallas_call(
        body, name=name, grid=(dil, nb), in_specs=_attn_specs(lambda n: n), out_specs=[out_spec, out_spec],
        out_shape=[jax.ShapeDtypeStruct(q.shape, F32)] * 2, compiler_params=_params(2),
    )(q, k, k, v, v)


def _attn_merge(name, os_, ls_):
    s, ts = os_[0].shape[0] * os_[0].shape[1], SPLIT_TS

    def body(*refs):
        o_refs, l_refs, o_ref, l_ref, buf = refs[0:3], refs[3:6], refs[6], refs[7], refs[8]
        _interleave(o_refs, buf, 0, ts)
        _interleave(l_refs, buf, 3 * CHUNKS_PER_GROUP, ts)
        o0, o1, o2 = [_get(buf, g * CHUNKS_PER_GROUP, CHUNKS_PER_GROUP) for g in range(3)]
        l0, l1, l2 = [_get(buf, (3 + g) * CHUNKS_PER_GROUP, CHUNKS_PER_GROUP) for g in range(3)]
        m = jnp.maximum(jnp.maximum(l0, l1), l2)
        e0, e1, e2 = jnp.exp(l0 - m), jnp.exp(l1 - m), jnp.exp(l2 - m)
        tot = e0 + e1 + e2
        o_ref[...] = ((e0 * o0 + e1 * o1 + e2 * o2) / tot).astype(o_ref.dtype)
        l_ref[...] = m + jnp.log(tot)

    nat = pl.BlockSpec((ts, GROUP_W), lambda i: (i, 0))
    return pl.pallas_call(
        body, name=name, grid=(s // ts,), in_specs=[_dilated_spec(d, ts) for _ in range(2) for d in ATTN_DILATIONS],
        out_specs=[nat, nat], out_shape=[jax.ShapeDtypeStruct((s, GROUP_W), BF16), jax.ShapeDtypeStruct((s, GROUP_W), F32)],
        scratch_shapes=[pltpu.VMEM((6 * CHUNKS_PER_GROUP, ts, LANES), F32)], compiler_params=_params(1),
    )(*os_, *ls_)


def _attn_bwd(name, q, k, v, do, delta, lse):
    dil, rows, _ = q.shape
    nb = rows // BLK
    scale = HEAD_DIM ** -0.5

    def body(q_ref, kc_ref, kp_ref, vc_ref, vp_ref, do_ref, dl_ref, l_ref, dq_ref, dk_ref, dv_ref, ck_ref, cv_ref):
        n = pl.program_id(1)

        @pl.when(n == 0)
        def _():
            ck_ref[...] = jnp.zeros_like(ck_ref)
            cv_ref[...] = jnp.zeros_like(cv_ref)

        @pl.when(n < nb)
        def _():
            mask_c, mask_p = _band_masks(n > 0)
            q, kc, kp, vc, vp, dov = q_ref[...], kc_ref[...], kp_ref[...], vc_ref[...], vp_ref[...], do_ref[...]
            lb, db = l_ref[...], dl_ref[...]
            zero = jnp.zeros((BLK, GROUP_W), F32)
            dq_acc, dkc, dkp, dvc, dvp = zero, zero, zero, zero, zero
            for hm in _head_masks((BLK, GROUP_W)):
                qm = jnp.where(hm, q, jnp.zeros_like(q))
                dom = jnp.where(hm, dov, jnp.zeros_like(dov))
                lh = jnp.max(jnp.where(hm, lb, NEG_INF), axis=1, keepdims=True)
                delta = jnp.max(jnp.where(hm, db, NEG_INF), axis=1, keepdims=True)
                pc = jnp.exp(jnp.where(mask_c, _dot_nt(qm, kc) * scale, NEG_INF) - lh)
                pp = jnp.exp(jnp.where(mask_p, _dot_nt(qm, kp) * scale, NEG_INF) - lh)
                dsc = (pc * (_dot_nt(dom, vc) - delta) * scale).astype(BF16)
                dsp = (pp * (_dot_nt(dom, vp) - delta) * scale).astype(BF16)
                dq_acc = jnp.where(hm, _dot(dsc, kc) + _dot(dsp, kp), dq_acc)
                dkc += _dot_tn(dsc, qm)
                dkp += _dot_tn(dsp, qm)
                dvc += _dot_tn(pc.astype(BF16), dom)
                dvp += _dot_tn(pp.astype(BF16), dom)
            dq_ref[...] = dq_acc.astype(dq_ref.dtype)
            dk_ref[...] = (ck_ref[...] + dkp).astype(dk_ref.dtype)
            dv_ref[...] = (cv_ref[...] + dvp).astype(dv_ref.dtype)
            ck_ref[...] = dkc
            cv_ref[...] = dvc

        @pl.when(n == nb)
        def _():
            dk_ref[...] = ck_ref[...].astype(dk_ref.dtype)
            dv_ref[...] = cv_ref[...].astype(dv_ref.dtype)

    clamp = lambda n: jnp.minimum(n, nb - 1)
    qspec = pl.BlockSpec(ATTN_BLK, lambda r, n: (r, clamp(n), 0))
    kspec = pl.BlockSpec(ATTN_BLK, lambda r, n: (r, jnp.maximum(n - 1, 0), 0))
    return pl.pallas_call(
        body, name=name, grid=(dil, nb + 1), in_specs=_attn_specs(clamp) + [qspec, qspec, qspec],
        out_specs=[qspec, kspec, kspec], out_shape=[jax.ShapeDtypeStruct(q.shape, BF16)] * 3,
        scratch_shapes=[pltpu.VMEM((BLK, GROUP_W), F32), pltpu.VMEM((BLK, GROUP_W), F32)], compiler_params=_params(2),
    )(q, k, k, v, v, do, delta, lse)


CONV_TS = 128
HALO = 32
SHIFT_ROWS = CONV_TS + HALO - 8


def _make_shifts(buf, sh):
    for s_ in range(1, 8):
        sh[s_ - 1] = buf[pl.ds(s_, SHIFT_ROWS), :]


def _window(buf, sh, off, ts, cols):
    q, s_ = divmod(off, 8)
    if s_ == 0:
        return buf[pl.ds(off, ts), cols]
    return sh[s_ - 1, pl.ds(8 * q, ts), cols]


def _conv_fwd(name, u, w, b, lg, lb):
    s = u.shape[0]
    ts, per = CONV_TS, CONV_TS // HALO

    def body(a_ref, g_ref, ap_ref, gp_ref, w_ref, b_ref, lg_ref, lb_ref, c_ref, act_ref, buf, cbuf, sh):
        i = pl.program_id(0)
        buf[pl.ds(HALO, ts), :] = a_ref[...].astype(F32) * _sigmoid(g_ref[...].astype(F32))
        prev = ap_ref[...].astype(F32) * _sigmoid(gp_ref[...].astype(F32))
        buf[pl.ds(0, HALO), :] = jnp.where(i > 0, prev, 0.0)
        _make_shifts(buf, sh)
        rb = ts // 2
        for lc in range(D // LANES):
            cols = pl.ds(lc * LANES, LANES)
            for r0 in range(0, ts, rb):
                acc = jnp.broadcast_to(b_ref[:, cols], (rb, LANES))
                for j in range(CONV_K):
                    acc = acc + w_ref[pl.ds(j, 1), cols] * _window(buf, sh, r0 + HALO - (CONV_K - 1) + j, rb, cols)
                cbuf[pl.ds(r0, rb), cols] = acc
        c = cbuf[...]
        mu = jnp.mean(c, axis=-1, keepdims=True)
        xc = c - mu
        ln = xc * lax.rsqrt(jnp.mean(xc * xc, axis=-1, keepdims=True) + EPS) * lg_ref[...] + lb_ref[...]
        c_ref[...] = c.astype(c_ref.dtype)
        act_ref[...] = (ln * _sigmoid(ln)).astype(act_ref.dtype)

    halo = lambda cb: pl.BlockSpec((HALO, D), functools.partial(lambda i, cb: (jnp.maximum(i * per - 1, 0), cb), cb=cb))
    vec = pl.BlockSpec((1, D), lambda i: (0, 0))
    return pl.pallas_call(
        body, name=name, grid=(s // ts,),
        in_specs=[pl.BlockSpec((ts, D), lambda i: (i, 0)), pl.BlockSpec((ts, D), lambda i: (i, 1)), halo(0), halo(1),
                  pl.BlockSpec((HALO, D), lambda i: (0, 0)), vec, vec, vec],
        out_specs=[pl.BlockSpec((ts, D), lambda i: (i, 0))] * 2,
        out_shape=[jax.ShapeDtypeStruct((s, D), BF16)] * 2,
        scratch_shapes=[pltpu.VMEM((ts + HALO, D), F32), pltpu.VMEM((ts, D), F32), pltpu.VMEM((7, SHIFT_ROWS, D), F32)],
        compiler_params=_params(1),
    )(u, u, u, u, w, b, lg, lb)


def _conv_bwd(name, dc, u, w):
    s = u.shape[0]
    ts, per = CONV_TS, CONV_TS // HALO
    n_t = s // ts

    def body(dc_ref, dn_ref, a_ref, g_ref, ap_ref, gp_ref, w_ref, du_ref, dw_ref, db_ref, buf, dbuf, hbuf, sh, dsh):
        i = pl.program_id(0)
        a = a_ref[...].astype(F32)
        sg = _sigmoid(g_ref[...].astype(F32))
        buf[pl.ds(HALO, ts), :] = a * sg
        prev = ap_ref[...].astype(F32) * _sigmoid(gp_ref[...].astype(F32))
        buf[pl.ds(0, HALO), :] = jnp.where(i > 0, prev, 0.0)
        dcv = dc_ref[...]
        dbuf[pl.ds(0, ts), :] = dcv
        dbuf[pl.ds(ts, HALO), :] = jnp.where(i < n_t - 1, dn_ref[...], 0.0)

        @pl.when(i == 0)
        def _():
            dw_ref[...] = jnp.zeros_like(dw_ref)
            db_ref[...] = jnp.zeros_like(db_ref)

        db_ref[...] += _colsum(dcv)
        _make_shifts(buf, sh)
        _make_shifts(dbuf, dsh)
        rb = ts // 4
        for lc in range(D // LANES):
            cols = pl.ds(lc * LANES, LANES)
            for r0 in range(0, ts, rb):
                d0 = dbuf[pl.ds(r0, rb), cols]
                acc = jnp.zeros((rb, LANES), F32)
                for j in range(CONV_K):
                    acc = acc + w_ref[pl.ds(j, 1), cols] * _window(dbuf, dsh, r0 + CONV_K - 1 - j, rb, cols)
                    part = d0 * _window(buf, sh, r0 + HALO - (CONV_K - 1) + j, rb, cols)
                    dw_ref[pl.ds(8 * j, 8), cols] += jnp.sum(part.reshape(rb // 8, 8, LANES), axis=0)
                hbuf[pl.ds(r0, rb), cols] = acc
        dh = hbuf[...]
        du_ref[:, pl.ds(0, D)] = (dh * sg).astype(du_ref.dtype)
        du_ref[:, pl.ds(D, D)] = (dh * a * sg * (1.0 - sg)).astype(du_ref.dtype)

    halo = lambda cb: pl.BlockSpec((HALO, D), functools.partial(lambda i, cb: (jnp.maximum(i * per - 1, 0), cb), cb=cb))
    nxt = pl.BlockSpec((HALO, D), lambda i: (jnp.minimum((i + 1) * per, s // HALO - 1), 0))
    return pl.pallas_call(
        body, name=name, grid=(n_t,),
        in_specs=[pl.BlockSpec((ts, D), lambda i: (i, 0)), nxt, pl.BlockSpec((ts, D), lambda i: (i, 0)),
                  pl.BlockSpec((ts, D), lambda i: (i, 1)), halo(0), halo(1), pl.BlockSpec((HALO, D), lambda i: (0, 0))],
        out_specs=[pl.BlockSpec((ts, 2 * D), lambda i: (i, 0)), pl.BlockSpec((8 * CONV_K, D), lambda i: (0, 0)),
                   pl.BlockSpec((1, D), lambda i: (0, 0))],
        out_shape=[jax.ShapeDtypeStruct((s, 2 * D), BF16), jax.ShapeDtypeStruct((8 * CONV_K, D), F32),
                   jax.ShapeDtypeStruct((1, D), F32)],
        scratch_shapes=[pltpu.VMEM((ts + HALO, D), F32), pltpu.VMEM((ts + HALO, D), F32), pltpu.VMEM((ts, D), F32),
                        pltpu.VMEM((7, SHIFT_ROWS, D), F32), pltpu.VMEM((7, SHIFT_ROWS, D), F32)],
        compiler_params=_params(1),
    )(dc, dc, u, u, u, u, w)


def _mix_fwd(tag, x1, h1, gate, wts, cos2, sin2, nxt, job=None):
    ident = lambda accs, _t, _v: accs
    (qkv,), got = _with_job(_mm(f"{tag}_qkv", [(h1, wts["w_qkv"])], ident, [BF16], job=job), job)
    u = _mm(f"{tag}_u", [(h1, wts["w_u"])], ident, [BF16], tn=D)[0]
    ga, gc = _mm(f"{tag}_gates", [(h1, wts["w_ga"]), (h1, wts["w_gc"])], ident, [BF16] * 2)
    qd, kd, vd = _rope_split(f"{tag}_rope", qkv, cos2, sin2)
    per_group = [_attn_fwd(f"{tag}_attn{g}", qd[g], kd[g], vd[g]) for g in range(len(ATTN_DILATIONS))]
    o, lse = _attn_merge(f"{tag}_merge", [p[0] for p in per_group], [p[1] for p in per_group])
    cpre, act = _conv_fwd(f"{tag}_conv", u, wts["conv_w"], wts["conv_b"], wts["ln_g"], wts["ln_b"])

    def gate_epi(accs, tl, _v):
        ya, yc = accs
        return [_sigmoid(tl[0].astype(F32)) * ya + _sigmoid(tl[1].astype(F32)) * yc, ya, yc]

    y, ya, yc = _mm(f"{tag}_branch", [(o, wts["attn_wo"]), (act, wts["conv_wo"])], gate_epi, [BF16] * 3,
                    tiles=[(ga, 0), (gc, 0)])

    x2, f1, h_next = _project_out(f"{tag}_out", y, wts["w_out"], MX_COEF, x1, gate, nxt)
    return x2, h_next, (x1, h1, u, ga, gc, qd, kd, vd, o, lse, cpre, act, y, ya, yc, f1), got


def _ln_bwd_epi(accs, tl, vs):
    cv = tl[0].astype(F32)
    g, b = vs
    xc = cv - jnp.mean(cv, axis=-1, keepdims=True)
    rstd = lax.rsqrt(jnp.mean(xc * xc, axis=-1, keepdims=True) + EPS)
    xh = xc * rstd
    ln = xh * g + b
    sg = _sigmoid(ln)
    dln = accs[0] * (sg * (1.0 + ln * (1.0 - sg)))
    dxh = dln * g
    dc = rstd * (dxh - jnp.mean(dxh, axis=-1, keepdims=True) - xh * jnp.mean(dxh * xh, axis=-1, keepdims=True))
    return [dc, _colsum(dln * xh), _colsum(dln)]


def _mix_bwd(tag, dx2, dfm, saved, ng, sc, wts, cos2, sin2, prev, exchange_job):
    x1, h1, u, ga, gc, qd, kd, vd, o, lse, cpre, act, y, ya, yc, _f1 = saved

    def epi(accs, tl, _v):
        dy = accs[0]
        sa, sc_ = _sigmoid(tl[0].astype(F32)), _sigmoid(tl[1].astype(F32))
        return [dy * sa, dy * sc_, dy * tl[2].astype(F32) * sa * (1.0 - sa), dy * tl[3].astype(F32) * sc_ * (1.0 - sc_)]

    dya, dyc, dga, dgc = _mm(f"{tag}_bout", [(dfm, wts["w_out"])], epi, [BF16] * 4, nt=True,
                             tiles=[(ga, 0), (gc, 0), (ya, 0), (yc, 0)], chunk=2 * LANES)
    grads = {"w_out": _mm_tn(f"{tag}_dwout", y, dfm), "attn_wo": _mm_tn(f"{tag}_dwattn", o, dya),
             "conv_wo": _mm_tn(f"{tag}_dwconv", act, dyc)}
    ident = lambda accs, _t, _v: accs
    do = _mm(f"{tag}_battn", [(dya, wts["attn_wo"])], ident, [BF16], nt=True)[0]
    dc, dlg, dlb = _mm(f"{tag}_bconv", [(dyc, wts["conv_wo"])], _ln_bwd_epi, [F32], nt=True, tiles=[(cpre, 0)],
                       vecs=[wts["ln_g"], wts["ln_b"]], n_sums=2, tm=BWD_LAST_TM)
    du, dw8, dcb = _conv_bwd(f"{tag}_convb", dc, u, wts["conv_w"])
    dod, deltad, lsed = _grad_split(f"{tag}_gsplit", do, o, lse)
    dqs, dks, dvs = [], [], []
    for g in range(len(ATTN_DILATIONS)):
        dq, dk, dv = _attn_bwd(f"{tag}_attnb{g}", qd[g], kd[g], vd[g], dod[g], deltad[g], lsed[g])
        dqs.append(dq); dks.append(dk); dvs.append(dv)
    dqkv = _rope_join(f"{tag}_ropeb", dqs, dks, dvs, cos2, sin2)
    grads["w_in"] = jnp.concatenate([_mm_tn(f"{tag}_dwqkv", h1, dqkv), _mm_tn(f"{tag}_dwu", h1, du),
                                     _mm_tn(f"{tag}_dwga", h1, dga), _mm_tn(f"{tag}_dwgc", h1, dgc)], axis=1)
    job = exchange_job([_split(grads[k], ax) for k, ax in zip(MX_NAMES, MX_AXES)])
    prods = [(dqkv, wts["w_qkv"]), (du, wts["w_u"]), (dga, wts["w_ga"]), (dgc, wts["w_gc"])]
    dx1, df_prev, sums, dgate_prev, got = _bwd_last(f"{tag}_bin", prods, x1, dx2, ng, sc, prev, job)
    small = {"conv_w": dw8.reshape(CONV_K, 8, D).sum(axis=1), "conv_b": dcb, "ln_g": dlg, "ln_b": dlb}
    return dx1, df_prev, sums, dgate_prev, small, got


def _loss_head(name, x, target, fg, f_last, gate_last, coef_last):
    def body(rows, vecs):
        (xv, tv, fv), (g, gl) = rows, vecs
        xhat, r = _rms_parts(xv)
        err = xhat * g - tv
        dy = err * (1.0 / D)
        dx = _rms_bwd(dy * g, xhat, r)
        return [dx, coef_last * gl * dx], [_colsum(err * err), _colsum(dy * xhat), _colsum(coef_last * dx * fv.astype(F32))]

    return _rowwise(name, body, [x, target, f_last], [fg, gate_last], [(D, F32), (D, BF16)], [(1, D)] * 3)


FF_NAMES, FF_AXES = ("ffn_wg", "ffn_wu", "ffn_wd"), (1, 1, 0)
MX_NAMES, MX_AXES = ("w_in", "attn_wo", "conv_wo", "w_out"), (1, 1, 0, 0)
GROUPS = (("ff", 0, 0), ("mx", 0), ("ff", 0, 1), ("ff", 1, 0), ("mx", 1), ("ff", 1, 1))


def _group_name(grp):
    return "_".join(str(p) for p in grp)


def _mix_weights(blocks, small):
    w_in, attn_wo, conv_wo, w_out = [_join(b, a) for b, a in zip(blocks, MX_AXES)]
    return dict(small, w_qkv=w_in[:, :3 * QKV_W], w_u=w_in[:, 3 * QKV_W:3 * QKV_W + 2 * D],
                w_ga=w_in[:, 3 * QKV_W + 2 * D:3 * QKV_W + 3 * D], w_gc=w_in[:, 3 * QKV_W + 3 * D:],
                attn_wo=attn_wo, conv_wo=conv_wo, w_out=w_out)


def _local_step(x, target, mod, norm_g, shards, small_w, final_g, gather_job, exchange_job):
    cos2, sin2 = _rope_tables(x.shape[0])
    row = lambda a: a[None, :]
    sub = lambda grp: (grp[1], 2 * grp[2] if grp[0] == "ff" else 1)
    coef = lambda grp: FF_COEF if grp[0] == "ff" else MX_COEF
    ng = lambda grp: row(norm_g[sub(grp)])
    shift = lambda grp: row(mod[sub(grp)[0], 3 * sub(grp)[1]])
    scale = lambda grp: row(mod[sub(grp)[0], 3 * sub(grp)[1] + 1])
    gate = lambda grp: row(mod[sub(grp)[0], 3 * sub(grp)[1] + 2])
    blocks = _run_job("gather_" + _group_name(GROUPS[0]), gather_job(shards[GROUPS[0]]))
    h = _modulate("first_mod", x, ng(GROUPS[0]), scale(GROUPS[0]), shift(GROUPS[0]))
    saved, wts = [], []
    for n, grp in enumerate(GROUPS):
        last = n + 1 == len(GROUPS)
        job = None if last else gather_job(shards[GROUPS[n + 1]])
        nxt = None if last else (ng(GROUPS[n + 1]), scale(GROUPS[n + 1]), shift(GROUPS[n + 1]))
        if grp[0] == "ff":
            wts.append([_join(b, a) for b, a in zip(blocks, FF_AXES)])
            x, h, sv, blocks = _ffn_fwd(f"l{grp[1]}f{grp[2]}", x, h, gate(grp), *wts[-1], nxt, job=job)
        else:
            wts.append(_mix_weights(blocks, small_w[grp[1]]))
            x, h, sv, blocks = _mix_fwd(f"l{grp[1]}mx", x, h, gate(grp), wts[-1], cos2, sin2, nxt, job=job)
        saved.append(sv)
    dx, df, sq, dfg, dgate = _loss_head("loss_head", x, target, row(final_g), saved[-1][-1], gate(GROUPS[-1]), coef(GROUPS[-1]))
    loss = (0.5 / D) * jnp.sum(sq)
    dmod = [[None] * 3, [None] * 3]
    dng = [[None] * 3, [None] * 3]
    small, recv = [None, None], {}
    for n in reversed(range(len(GROUPS))):
        grp = GROUPS[n]
        l, i = sub(grp)
        prev = None if n == 0 else (saved[n - 1][-1], gate(GROUPS[n - 1]), coef(GROUPS[n - 1]))
        if grp[0] == "ff":
            dx, df_prev, sums, dgate_prev, recv[grp] = _ffn_bwd(
                f"l{l}f{grp[2]}", dx, df, saved[n], ng(grp), scale(grp), *wts[n], prev, exchange_job)
        else:
            dx, df_prev, sums, dgate_prev, small[l], recv[grp] = _mix_bwd(
                f"l{l}mx", dx, df, saved[n], ng(grp), scale(grp), wts[n], cos2, sin2, prev, exchange_job)
        dmod[l][i] = jnp.concatenate([sums[0], sums[1], dgate], axis=0)
        dng[l][i] = sums[2]
        df, dgate = df_prev, dgate_prev
    dmod = jnp.stack([jnp.concatenate(d, axis=0) for d in dmod])
    small = {k: jnp.stack([small[0][k].reshape(-1, D), small[1][k].reshape(-1, D)]) for k in small[0]}
    small = dict(small, norm_g=jnp.stack([jnp.concatenate(d, axis=0) for d in dng]), final_g=dfg.reshape(D))
    return loss, dx, dmod, small, recv


HBM_SPEC = pl.BlockSpec(memory_space=pl.ANY)


def _place():
    return lax.axis_index("x"), lax.axis_index("y"), lax.axis_index("c")


N_PEERS = N_DEV - 1


def _gather_job(arrays):
    n_a = len(arrays)

    def run(phase, x_refs, out_refs, send_sems, recv_sems, local_sems):
        x, y, c = _place()
        me, sibling = (x, y, c), (x, y, 1 - c)
        chips = [(1 - x, y), (x, 1 - y), (1 - x, 1 - y)]

        def copy(a, k, block, to, from_input=False):
            px, py, pc = block
            rows = out_refs[a].at[4 * px + 2 * py + pc]
            return pltpu.make_async_remote_copy(
                src_ref=x_refs[a] if from_input else rows, dst_ref=rows, send_sem=send_sems.at[a * N_PEERS + k],
                recv_sem=recv_sems.at[a * N_PEERS + k], device_id=to, device_id_type=pl.DeviceIdType.MESH)

        mine = [pltpu.make_async_copy(x_refs[a], out_refs[a].at[4 * x + 2 * y + c], local_sems.at[a]) for a in range(n_a)]
        first = []
        for j, chip in enumerate(chips):
            first += [copy(a, 1 + j, me, (*chip, c), from_input=True) for a in range(n_a)]
        first += [copy(a, 0, me, sibling, from_input=True) for a in range(n_a)]
        if phase == "start":
            for cp in mine + first:
                cp.start()
            return
        passed = []
        for j, chip in enumerate(chips):
            for a in range(n_a):
                copy(a, 1 + j, (*chip, c), me).wait_recv()
                passed.append(copy(a, 4 + j, (*chip, c), sibling))
                passed[-1].start()
        for a in range(n_a):
            copy(a, 0, sibling, me).wait_recv()
        for j, chip in enumerate(chips):
            for a in range(n_a):
                copy(a, 4 + j, (*chip, 1 - c), me).wait_recv()
        for cp in first + passed:
            cp.wait_send()
        for cp in mine:
            cp.wait()

    return dict(arrays=list(arrays), run=run,
                out_shape=[jax.ShapeDtypeStruct((N_DEV, *a.shape), a.dtype) for a in arrays])


def _job_scratch(job):
    n_a = len(job["arrays"])
    return [pltpu.SemaphoreType.DMA((n_a * N_PEERS,)), pltpu.SemaphoreType.DMA((n_a * N_PEERS,)), pltpu.SemaphoreType.DMA((n_a,))]


def _run_job(name, job):
    n_a = len(job["arrays"])

    def body(*refs):
        job["run"]("start", refs[:n_a], refs[n_a:2 * n_a], *refs[2 * n_a:])
        job["run"]("finish", refs[:n_a], refs[n_a:2 * n_a], *refs[2 * n_a:])

    return pl.pallas_call(
        body, name=name, out_shape=job["out_shape"], in_specs=[HBM_SPEC] * n_a, out_specs=[HBM_SPEC] * n_a,
        scratch_shapes=_job_scratch(job),
    )(*job["arrays"])


def _all_gather(name, arrays):
    return _run_job(name, _gather_job(arrays))


def _exchange_job(arrays):
    n_a = len(arrays)

    def run(phase, g_refs, out_refs, send_sems, recv_sems, local_sems):
        x, y, c = _place()
        my = 4 * x + 2 * y + c
        copies = [pltpu.make_async_copy(g_refs[a].at[my], out_refs[a].at[my], local_sems.at[a]) for a in range(n_a)]
        for k in (4, 2, 6, 1, 5, 3, 7):
            px = 1 - x if k & 4 else x
            py = 1 - y if k & 2 else y
            pc = 1 - c if k & 1 else c
            for a in range(n_a):
                copies.append(pltpu.make_async_remote_copy(
                    src_ref=g_refs[a].at[4 * px + 2 * py + pc], dst_ref=out_refs[a].at[my],
                    send_sem=send_sems.at[a * N_PEERS + k - 1], recv_sem=recv_sems.at[a * N_PEERS + k - 1],
                    device_id=(px, py, pc), device_id_type=pl.DeviceIdType.MESH))
        for cp in copies:
            if phase == "start":
                cp.start()
            else:
                cp.wait()

    return dict(arrays=list(arrays), run=run, out_shape=[jax.ShapeDtypeStruct(a.shape, a.dtype) for a in arrays])


SLAB_TS = 2048


def _sum_parts(name, parts):
    def body(rows, _v):
        tot = rows[0].astype(F32)
        for r in rows[1:]:
            tot = tot + r.astype(F32)
        return [tot], []

    return _rowwise(name, body, list(parts), [], [(parts[0].shape[1], F32)], ts=SLAB_TS)[0]


def _adamw(name, w, parts, m, v):
    def body(rows, _v):
        wv, mv, vv = rows[0], rows[1], rows[2]
        g = rows[3].astype(F32)
        for r in rows[4:]:
            g = g + r.astype(F32)
        m2 = ADAM_B1 * mv + (1.0 - ADAM_B1) * g
        v2 = ADAM_B2 * vv + (1.0 - ADAM_B2) * (g * g)
        m_hat = m2 / (1.0 - ADAM_B1 ** ADAM_STEP)
        v_hat = v2 / (1.0 - ADAM_B2 ** ADAM_STEP)
        delta = -ADAM_LR * (m_hat / (jnp.sqrt(v_hat) + ADAM_EPS) + ADAM_WD * wv)
        return [g, delta, m2, v2], []

    width = w.shape[1]
    return _rowwise(name, body, [w, m, v, *parts], [], [(width, F32)] * 4, ts=max(16, SLAB_TS * LANES // width))


def _to_slab(arrays, dtype):
    flat = jnp.concatenate([a.reshape(-1).astype(dtype) for a in arrays])
    rows = -(-flat.shape[0] // LANES)
    rows = -(-rows // 8) * 8 if rows <= SLAB_TS else -(-rows // SLAB_TS) * SLAB_TS
    return jnp.pad(flat, (0, rows * LANES - flat.shape[0])).reshape(rows, LANES)


def _from_slab(slab, shapes, lead=()):
    flat = slab.reshape(*lead, -1)
    out, at = [], 0
    for shp in shapes:
        size = 1
        for d in shp:
            size *= d
        out.append(flat[..., at:at + size].reshape(*lead, *shp))
        at += size
    return out


def _join(blocks, axis):
    full = jnp.moveaxis(blocks, 0, axis)
    return full.reshape(*full.shape[:axis], -1, *full.shape[axis + 2:])


def _split(full, axis):
    shp = full.shape
    return jnp.moveaxis(full.reshape(*shp[:axis], N_DEV, shp[axis] // N_DEV, *shp[axis + 1:]), axis, 0)


def kernel(x, c, ada_w, ada_b, norm_g, ffn_wg, ffn_wu, ffn_wd, w_in, attn_wo, conv_w, conv_b, conv_ln_g, conv_ln_b, conv_wo, w_out, final_g, loss_target, m_ada_w, m_ada_b, m_norm_g, m_ffn_wg, m_ffn_wu, m_ffn_wd, m_w_in, m_attn_wo, m_conv_w, m_conv_b, m_conv_ln_g, m_conv_ln_b, m_conv_wo, m_w_out, m_final_g, v_ada_w, v_ada_b, v_norm_g, v_ffn_wg, v_ffn_wu, v_ffn_wd, v_w_in, v_attn_wo, v_conv_w, v_conv_b, v_conv_ln_g, v_conv_ln_b, v_conv_wo, v_w_out, v_final_g):
    px, py, pc = _place()
    me = 4 * px + 2 * py + pc
    n_mod = ada_w.shape[2]
    big_w = dict(ffn_wg=ffn_wg, ffn_wu=ffn_wu, ffn_wd=ffn_wd, w_in=w_in, attn_wo=attn_wo, conv_wo=conv_wo, w_out=w_out)
    big_m = dict(ffn_wg=m_ffn_wg, ffn_wu=m_ffn_wu, ffn_wd=m_ffn_wd, w_in=m_w_in, attn_wo=m_attn_wo, conv_wo=m_conv_wo, w_out=m_w_out)
    big_v = dict(ffn_wg=v_ffn_wg, ffn_wu=v_ffn_wu, ffn_wd=v_ffn_wd, w_in=v_w_in, attn_wo=v_attn_wo, conv_wo=v_conv_wo, w_out=v_w_out)

    small_in = [c, norm_g, conv_w]
    g1 = _all_gather("gather_small", [_to_slab(small_in, F32)])[0]
    c_all, ng_blocks, cw_blocks = _from_slab(g1, [a.shape for a in small_in], lead=(N_DEV,))
    c_all = c_all.reshape(N_DEV, D)
    norm_g_full = _join(ng_blocks, 2)
    conv_w_full = _join(cw_blocks, 2)
    as2d = lambda a: a.reshape(-1, a.shape[-1])

    c_act = _rowwise("cond_silu", lambda rows, _v: ([rows[0] * _sigmoid(rows[0])], []), [c_all], [], [(D, BF16)])[0]
    c_pad = jnp.pad(c_act, ((0, LANES - N_DEV), (0, 0)))
    mod_cols = []
    for l in range(2):
        bias = lax.dynamic_slice_in_dim(ada_b[l], me * n_mod, n_mod)[None, :]
        out = _mm(f"mod{l}", [(c_pad, ada_w[l].astype(BF16))], lambda accs, _t, vs: [accs[0] + vs[0]], [F32], vecs=[bias])[0]
        mod_cols.append(out[:N_DEV])
    g2 = _all_gather("gather_mod", [_to_slab([jnp.stack(mod_cols)], F32)])[0]
    mod_all = _from_slab(g2, [(2, N_DEV, n_mod)], lead=(N_DEV,))[0]
    mod = lax.dynamic_index_in_dim(mod_all, me, axis=2, keepdims=False)
    mod = jnp.moveaxis(mod, 0, 1).reshape(2, 9, D)

    index = {grp: (grp[1], grp[2]) if grp[0] == "ff" else (grp[1],) for grp in GROUPS}
    names = {grp: FF_NAMES if grp[0] == "ff" else MX_NAMES for grp in GROUPS}
    shards = {grp: [big_w[k][index[grp]].astype(BF16) for k in names[grp]] for grp in GROUPS}
    small_l = [dict(conv_w=jnp.pad(conv_w_full[l], ((0, HALO - CONV_K), (0, 0))), conv_b=conv_b[l][None, :],
                    ln_g=conv_ln_g[l][None, :], ln_b=conv_ln_b[l][None, :]) for l in range(2)]

    loss, dx, dmod, small, recv = _local_step(x[0], loss_target[0], mod, norm_g_full, shards, small_l, final_g,
                                              _gather_job, _exchange_job)
    loss = lax.psum(loss, MESH_AXES)

    small_names = ["norm_g", "conv_w", "conv_b", "ln_g", "ln_b", "final_g"]
    small_parts = [dmod] + [small[k] for k in small_names]
    g3 = _all_gather("gather_small_grads", [_to_slab(small_parts, F32)])[0]
    tot = _sum_parts("sum_small_grads", [g3[k] for k in range(N_DEV)])
    _, g_ng, g_cw, g_cb, g_lg, g_lb, g_fg = _from_slab(tot, [a.shape for a in small_parts])
    g_ab = _from_slab(tot, [(2, 9 * D)])[0]
    dmod_all = _from_slab(g3, [dmod.shape], lead=(N_DEV,))[0].reshape(N_DEV, 2, 9 * D)
    dm_mine = lax.dynamic_slice_in_dim(dmod_all, me * n_mod, n_mod, axis=2)
    g_aw = jnp.stack([
        _mm_tn(f"dada_w{l}", c_pad, jnp.pad(dm_mine[:, l], ((0, LANES - N_DEV), (0, 0))).astype(BF16), out_dtype=F32)
        for l in range(2)])
    cols = lambda a: lax.dynamic_slice_in_dim(a, me * (D // N_DEV), D // N_DEV, axis=2)
    small_w = [ada_b, norm_g, conv_w, conv_b, conv_ln_g, conv_ln_b, final_g]
    small_m = [m_ada_b, m_norm_g, m_conv_w, m_conv_b, m_conv_ln_g, m_conv_ln_b, m_final_g]
    small_v = [v_ada_b, v_norm_g, v_conv_w, v_conv_b, v_conv_ln_g, v_conv_ln_b, v_final_g]
    small_g = [g_ab, cols(g_ng), cols(g_cw), g_cb, g_lg, g_lb, g_fg]
    s_shapes = [a.shape for a in small_w]
    s_out = _adamw("adamw_small", _to_slab(small_w, F32), [_to_slab(small_g, F32)], _to_slab(small_m, F32), _to_slab(small_v, F32))
    aw_out = [o.reshape(ada_w.shape) for o in _adamw("adamw_ada_w", as2d(ada_w), [as2d(g_aw)], as2d(m_ada_w), as2d(v_ada_w))]

    upd = {}
    for grp in GROUPS:
        for k, blocks in zip(names[grp], recv[grp]):
            parts = [(blocks, blocks.shape[2], 0, j) for j in range(N_DEV)]
            at = index[grp]
            upd[k, at] = _adamw(f"adamw_{k}_{_group_name(grp)}", big_w[k][at], parts, big_m[k][at], big_v[k][at])

    def stacked(k, i):
        if k in FF_NAMES:
            return jnp.stack([jnp.stack([upd[k, (l, j)][i] for j in range(2)]) for l in range(2)])
        return jnp.stack([upd[k, (l,)][i] for l in range(2)])

    def ordered(i):
        ab, ng, cw, cb, lg, lb, fg = _from_slab(s_out[i], s_shapes)
        bg = {k: stacked(k, i) for k in FF_NAMES + MX_NAMES}
        return [aw_out[i], ab, ng, bg["ffn_wg"], bg["ffn_wu"], bg["ffn_wd"], bg["w_in"], bg["attn_wo"], cw, cb, lg, lb,
                bg["conv_wo"], bg["w_out"], fg]

    return (loss, dx[None], *ordered(0), *ordered(1), *ordered(2), *ordered(3))
```

```python
import functools

import jax
import jax.numpy as jnp
from jax import lax
from jax.experimental import pallas as pl
from jax.experimental.pallas import tpu as pltpu

F32 = jnp.float32
BF16 = jnp.bfloat16

N_DEV = 8
D = 1024
D_FF = 2816
HEAD_DIM = 64
GROUP_W = 256
ATTN_DILATIONS = (1, 4, 16)
BLK = 128
QKV_W = 768
CONV_K = 31
ROPE_THETA = 10000.0
EPS = 1e-6
NEG_INF = -1e30
ADAM_LR, ADAM_B1, ADAM_B2, ADAM_EPS, ADAM_WD, ADAM_STEP = 0.001, 0.9, 0.999, 1e-08, 0.01, 10

V7X_VMEM_BYTES = 64 * 1024 * 1024
VMEM_LIMIT = V7X_VMEM_BYTES - 8 * 1024 * 1024
LANES = 128
MESH_AXES = ("x", "y", "c")


def _params(n_grid):
    return pltpu.CompilerParams(vmem_limit_bytes=VMEM_LIMIT, dimension_semantics=("arbitrary",) * n_grid)


def _sigmoid(v):
    return 1.0 / (1.0 + jnp.exp(-v))


def _mm(name, prods, epilogue, out_dtypes, *, nt=False, tiles=(), vecs=(), tm=512, tn=None, a_pre=None, chunk=None,
        job=None, n_sums=0):
    s = prods[0][0].shape[0]
    n = prods[0][1].shape[0] if nt else prods[0][1].shape[1]
    tn = n if tn is None else tn
    tm = min(tm, s)
    assert s % tm == 0 and n % tn == 0
    n_p, n_t, n_v = len(prods), len(tiles), len(vecs)
    dn = (((1,), (1,)), ((), ())) if nt else (((1,), (0,)), ((), ()))

    chunk = tn if chunk is None else chunk
    bounds = [(c0, min(chunk, tn - c0)) for c0 in range(0, tn, chunk)]

    n_o = len(out_dtypes) + n_sums
    n_j = len(job["arrays"]) if job else 0
    n_steps = (n // tn, s // tm)

    def body(*refs):
        p_refs, rest = refs[:2 * n_p], refs[2 * n_p:]
        t_refs, v_refs, rest = rest[:n_t], rest[n_t:n_t + n_v], rest[n_t + n_v:]
        j_in, o_refs, j_out, sems = rest[:n_j], rest[n_j:n_j + n_o], rest[n_j + n_o:2 * n_j + n_o], rest[2 * n_j + n_o:]
        o_refs, s_refs = o_refs[:n_o - n_sums], o_refs[n_o - n_sums:]
        if job:
            @pl.when(jnp.logical_and(pl.program_id(0) == 0, pl.program_id(1) == 0))
            def _():
                job["run"]("start", j_in, j_out, *sems)
        lhs = []
        for p in range(n_p):
            a = p_refs[2 * p][...]
            lhs.append(a if a_pre is None else a_pre(a))
        for c0, cw in bounds:
            cols = pl.ds(c0, cw)
            accs = []
            for p in range(n_p):
                b = p_refs[2 * p + 1][cols, :] if nt else p_refs[2 * p + 1][:, cols]
                accs.append(lax.dot_general(lhs[p], b, dn, preferred_element_type=F32))
            outs = epilogue(accs, [t[:, cols] for t in t_refs], [v[:, cols] for v in v_refs])
            for o_ref, o in zip(o_refs, outs[:len(o_refs)], strict=True):
                o_ref[:, cols] = o.astype(o_ref.dtype)
            if n_sums:
                first_row_tile = pl.program_id(1) == 0

                @pl.when(first_row_tile)
                def _():
                    for s_ref, part in zip(s_refs, outs[len(o_refs):], strict=True):
                        s_ref[:, cols] = part

                @pl.when(jnp.logical_not(first_row_tile))
                def _():
                    for s_ref, part in zip(s_refs, outs[len(o_refs):], strict=True):
                        s_ref[:, cols] += part
        if job:
            @pl.when(jnp.logical_and(pl.program_id(0) == n_steps[0] - 1, pl.program_id(1) == n_steps[1] - 1))
            def _():
                job["run"]("finish", j_in, j_out, *sems)

    in_specs = []
    operands = []
    for a, b in prods:
        k = a.shape[1]
        in_specs.append(pl.BlockSpec((tm, k), lambda j, i: (i, 0)))
        in_specs.append(pl.BlockSpec((tn, k), lambda j, i: (j, 0)) if nt else pl.BlockSpec((k, tn), lambda j, i: (0, j)))
        operands += [a, b]
    for arr, off in tiles:
        in_specs.append(pl.BlockSpec((tm, tn), functools.partial(lambda j, i, off: (i, j + off), off=off)))
        operands.append(arr)
    for v in vecs:
        in_specs.append(pl.BlockSpec((1, tn), lambda j, i: (0, j)))
        operands.append(v)
    out_specs = [pl.BlockSpec((tm, tn), lambda j, i: (i, j)) for _ in out_dtypes]
    out_specs += [pl.BlockSpec((1, tn), lambda j, i: (0, j)) for _ in range(n_sums)]
    out_shape = [jax.ShapeDtypeStruct((s, n), dt) for dt in out_dtypes] + [jax.ShapeDtypeStruct((1, n), F32)] * n_sums
    scratch = []
    if job:
        in_specs += [HBM_SPEC] * n_j
        operands += job["arrays"]
        out_specs += [HBM_SPEC] * n_j
        out_shape += job["out_shape"]
        scratch = _job_scratch(job)
    out = pl.pallas_call(
        body, name=name, grid=n_steps, in_specs=in_specs, out_specs=out_specs, out_shape=out_shape,
        scratch_shapes=scratch, compiler_params=_params(2),
    )(*operands)
    return (out[:n_o], out[n_o:]) if job else out


def _mm_tn(name, a, b, *, tk=512, tn=None, out_dtype=BF16):
    s, m = a.shape
    n = b.shape[1]
    tn = n if tn is None else tn
    tk = min(tk, s)
    n_k = s // tk
    assert s % tk == 0 and n % tn == 0

    def body(a_ref, b_ref, o_ref, acc_ref):
        k = pl.program_id(1)

        @pl.when(k == 0)
        def _():
            acc_ref[...] = jnp.zeros_like(acc_ref)

        acc_ref[...] += lax.dot_general(a_ref[...], b_ref[...], (((0,), (0,)), ((), ())), preferred_element_type=F32)

        @pl.when(k == n_k - 1)
        def _():
            o_ref[...] = acc_ref[...].astype(o_ref.dtype)

    return pl.pallas_call(
        body, name=name, grid=(n // tn, n_k),
        in_specs=[pl.BlockSpec((tk, m), lambda j, k: (k, 0)), pl.BlockSpec((tk, tn), lambda j, k: (k, j))],
        out_specs=pl.BlockSpec((m, tn), lambda j, k: (0, j)),
        out_shape=jax.ShapeDtypeStruct((m, n), out_dtype),
        scratch_shapes=[pltpu.VMEM((m, tn), F32)],
        compiler_params=_params(2),
    )(a, b)


def _pick_rows(s, target):
    if s <= target:
        return s
    return max(t for t in range(16, target + 1, 16) if s % t == 0)


def _rowwise(name, body, rows, vecs, outs, accs=(), *, ts=512):
    rows = [(r if isinstance(r, tuple) else (r, r.shape[1], 0)) for r in rows]
    rows = [r if len(r) == 4 else (*r, None) for r in rows]
    s = rows[0][0].shape[0]
    ts = _pick_rows(s, ts)
    n_r, n_v, n_o, n_a = len(rows), len(vecs), len(outs), len(accs)

    def kbody(*refs):
        r_refs, v_refs = refs[:n_r], refs[n_r:n_r + n_v]
        o_refs, a_refs = refs[n_r + n_v:n_r + n_v + n_o], refs[n_r + n_v + n_o:]
        res_o, res_a = body([r[...] for r in r_refs], [v[...] for v in v_refs])
        for o_ref, o in zip(o_refs, res_o, strict=True):
            o_ref[...] = o.astype(o_ref.dtype)
        if n_a:
            first = pl.program_id(0) == 0

            @pl.when(first)
            def _():
                for a_ref, a in zip(a_refs, res_a, strict=True):
                    a_ref[...] = a

            @pl.when(jnp.logical_not(first))
            def _():
                for a_ref, a in zip(a_refs, res_a, strict=True):
                    a_ref[...] += a

    in_specs = [
        pl.BlockSpec((ts, w), functools.partial(lambda i, cb: (i, cb), cb=cb)) if lead is None else
        pl.BlockSpec((None, ts, w), functools.partial(lambda i, cb, lead: (lead, i, cb), cb=cb, lead=lead))
        for _, w, cb, lead in rows]
    in_specs += [pl.BlockSpec(v.shape, functools.partial(lambda i, nd: (0,) * nd, nd=v.ndim)) for v in vecs]
    out_specs = [pl.BlockSpec((ts, w), lambda i: (i, 0)) for w, _ in outs]
    out_specs += [pl.BlockSpec(shp, functools.partial(lambda i, nd: (0,) * nd, nd=len(shp))) for shp in accs]
    out_shape = [jax.ShapeDtypeStruct((s, w), dt) for w, dt in outs] + [jax.ShapeDtypeStruct(shp, F32) for shp in accs]
    return pl.pallas_call(
        kbody, name=name, grid=(s // ts,), in_specs=in_specs, out_specs=out_specs, out_shape=out_shape,
        compiler_params=_params(1),
    )(*[r[0] for r in rows], *vecs)


def _colsum(v):
    return jnp.sum(v, axis=0, keepdims=True)


def _rms_parts(x):
    r = lax.rsqrt(jnp.mean(x * x, axis=-1, keepdims=True) + EPS)
    return x * r, r


def _rms_bwd(dxhat, xhat, r):
    return r * (dxhat - xhat * jnp.mean(dxhat * xhat, axis=-1, keepdims=True))


def _modulate(name, x, ng, sc, sh):
    def body(rows, vecs):
        (xv,), (g, s_, b) = rows, vecs
        xhat, _ = _rms_parts(xv)
        return [xhat * g * (1.0 + s_) + b], []

    return _rowwise(name, body, [x], [ng, sc, sh], [(D, BF16)])[0]


def _residual_epi(coef, with_next):
    def epi(accs, tl, vs):
        x_out = tl[0] + coef * vs[0] * accs[0]
        outs = [x_out, accs[0]]
        if with_next:
            xhat, _ = _rms_parts(x_out)
            outs.append(xhat * vs[1] * (1.0 + vs[2]) + vs[3])
        return outs

    return epi


def _bwd_last_epi(coef_prev):
    def epi(accs, tl, vs):
        dh = accs[0]
        for acc in accs[1:]:
            dh = dh + acc
        xhat, r = _rms_parts(tl[0])
        dn = dh * (1.0 + vs[1])
        dx_in = tl[1] + _rms_bwd(dn * vs[0], xhat, r)
        outs, sums = [dx_in], [_colsum(dh), _colsum(dh * xhat * vs[0]), _colsum(dn * xhat)]
        if coef_prev is not None:
            outs.append(coef_prev * vs[2] * dx_in)
            sums.append(_colsum(coef_prev * dx_in * tl[2].astype(F32)))
        return outs + sums

    return epi


BWD_LAST_TM = 256


def _bwd_last(name, prods, x, dxo, ng, sc, prev, job):
    tiles, vecs, dts = [(x, 0), (dxo, 0)], [ng, sc], [F32]
    if prev is not None:
        tiles, vecs, dts = tiles + [(prev[0], 0)], vecs + [prev[1]], dts + [BF16]
    outs, got = _mm(name, prods, _bwd_last_epi(None if prev is None else prev[2]), dts, nt=True, tiles=tiles, vecs=vecs,
                    n_sums=3 + (prev is not None), tm=BWD_LAST_TM, job=job)
    if prev is None:
        return outs[0], None, tuple(outs[1:4]), None, got
    return outs[0], outs[1], tuple(outs[2:5]), outs[5], got


def _with_job(result, job):
    return result if job else (result, None)


FF_COEF, MX_COEF = 0.5, 1.0


def _project_out(name, y, w, coef, x, gate, nxt):
    outs = _mm(name, [(y, w)], _residual_epi(coef, nxt is not None), [F32, BF16] + [BF16] * (nxt is not None),
               tiles=[(x, 0)], vecs=[gate, *(nxt or ())])
    return outs[0], outs[1], (outs[2] if nxt is not None else None)


def _ffn_fwd(tag, x, h, gate, wg, wu, wd, nxt, job=None):
    def up_epi(accs, _t, _v):
        a, u = accs
        return [a, u, a * _sigmoid(a) * u]

    (a, u, t), got = _with_job(_mm(f"{tag}_up", [(h, wg), (h, wu)], up_epi, [BF16] * 3, tn=D_FF // 2, job=job), job)
    x_out, f, h_next = _project_out(f"{tag}_down", t, wd, FF_COEF, x, gate, nxt)
    return x_out, h_next, (x, h, a, u, t, f), got


def _ffn_bwd(tag, dxo, df, saved, ng, sc, wg, wu, wd, prev, exchange_job):
    x, h, a, u, t, _f = saved

    def epi(accs, tl, _v):
        dt = accs[0]
        av, uv = tl[0].astype(F32), tl[1].astype(F32)
        sg = _sigmoid(av)
        return [dt * uv * (sg * (1.0 + av * (1.0 - sg))), dt * (av * sg)]

    da, du = _mm(f"{tag}_bdown", [(df, wd)], epi, [BF16] * 2, nt=True, tiles=[(a, 0), (u, 0)], tn=D_FF // 2,
                 chunk=3 * LANES)
    grads = (_mm_tn(f"{tag}_dwg", h, da), _mm_tn(f"{tag}_dwu", h, du), _mm_tn(f"{tag}_dwd", t, df))
    job = exchange_job([_split(g, ax) for g, ax in zip(grads, FF_AXES)])
    return _bwd_last(f"{tag}_bup", [(da, wg), (du, wu)], x, dxo, ng, sc, prev, job)


def _rope_tables(s):
    half = HEAD_DIM // 2
    inv_freq = ROPE_THETA ** (-(jnp.arange(half, dtype=F32) * 2.0 / HEAD_DIM))
    ang = jnp.arange(s, dtype=F32)[:, None] * inv_freq[None, :]
    cos, sin = jnp.cos(ang), jnp.sin(ang)
    return jnp.tile(jnp.concatenate([cos, cos], axis=1), (1, 2)), jnp.tile(jnp.concatenate([-sin, sin], axis=1), (1, 2))


def _rotate(v, cos2, sin2, sign):
    w = v.shape[1]
    lane = lax.broadcasted_iota(jnp.int32, v.shape, 1)
    partner = jnp.where(lane % HEAD_DIM < HEAD_DIM // 2, pltpu.roll(v, w - HEAD_DIM // 2, 1), pltpu.roll(v, HEAD_DIM // 2, 1))
    reps = w // LANES
    return v * jnp.tile(cos2, (1, reps)) + partner * (sign * jnp.tile(sin2, (1, reps)))


SPLIT_TS = 512


def _dilated_spec(dil, ts):
    return pl.BlockSpec((dil, ts // dil, GROUP_W), lambda i: (0, i, 0))


def _dilated_shape(s, dil, dtype):
    return jax.ShapeDtypeStruct((dil, s // dil, GROUP_W), dtype)


CHUNKS_PER_GROUP = GROUP_W // LANES


def _put(buf, chunk0, val):
    for c in range(val.shape[1] // LANES):
        buf[chunk0 + c] = val[:, c * LANES:(c + 1) * LANES]


def _get(buf, chunk0, n):
    return jnp.concatenate([buf[chunk0 + c] for c in range(n)], axis=1)


def _strided_rows(r, dil, ts):
    return pl.ds(r, ts // dil, stride=dil) if dil > 1 else pl.ds(0, ts)


def _deinterleave_one(buf, chunk0, out_ref, dil, ts):
    for half in range(CHUNKS_PER_GROUP):
        for r in range(dil):
            src = buf.at[chunk0 + half][_strided_rows(r, dil, ts), :]
            out_ref.at[r][:, pl.ds(half * LANES, LANES)] = src.astype(out_ref.dtype)


def _interleave_one(in_ref, buf, chunk0, dil, ts):
    for half in range(CHUNKS_PER_GROUP):
        for r in range(dil):
            src = in_ref.at[r][:, pl.ds(half * LANES, LANES)]
            buf.at[chunk0 + half][_strided_rows(r, dil, ts), :] = src.astype(F32)


def _deinterleave(buf, chunk0, out_refs, ts):
    for g, dil in enumerate(ATTN_DILATIONS):
        _deinterleave_one(buf, chunk0 + g * CHUNKS_PER_GROUP, out_refs[g], dil, ts)


def _interleave(in_refs, buf, chunk0, ts):
    for g, dil in enumerate(ATTN_DILATIONS):
        _interleave_one(in_refs[g], buf, chunk0 + g * CHUNKS_PER_GROUP, dil, ts)


def _rope_split(name, qkv, cos2, sin2):
    s, ts = qkv.shape[0], SPLIT_TS

    def body(qkv_ref, c_ref, s_ref, *rest):
        outs, buf = rest[:9], rest[9]
        c2, s2 = c_ref[...], s_ref[...]
        per = QKV_W // LANES
        _put(buf, 0, _rotate(qkv_ref[:, pl.ds(0, QKV_W)].astype(F32), c2, s2, 1.0))
        _put(buf, per, _rotate(qkv_ref[:, pl.ds(QKV_W, QKV_W)].astype(F32), c2, s2, 1.0))
        _put(buf, 2 * per, qkv_ref[:, pl.ds(2 * QKV_W, QKV_W)].astype(F32))
        for t in range(3):
            _deinterleave(buf, t * per, outs[3 * t:3 * t + 3], ts)

    tab = pl.BlockSpec((ts, LANES), lambda i: (i, 0))
    outs = pl.pallas_call(
        body, name=name, grid=(s // ts,), in_specs=[pl.BlockSpec((ts, 3 * QKV_W), lambda i: (i, 0)), tab, tab],
        out_specs=[_dilated_spec(d, ts) for _ in range(3) for d in ATTN_DILATIONS],
        out_shape=[_dilated_shape(s, d, BF16) for _ in range(3) for d in ATTN_DILATIONS],
        scratch_shapes=[pltpu.VMEM((3 * QKV_W // LANES, ts, LANES), F32)], compiler_params=_params(1),
    )(qkv, cos2, sin2)
    return outs[0:3], outs[3:6], outs[6:9]


def _rope_join(name, dq, dk, dv, cos2, sin2):
    s, ts = cos2.shape[0], SPLIT_TS

    def body(*refs):
        ins, c_ref, s_ref, o_ref, buf = refs[:9], refs[9], refs[10], refs[11], refs[12]
        per = QKV_W // LANES
        for t in range(3):
            _interleave(ins[3 * t:3 * t + 3], buf, t * per, ts)
        c2, s2 = c_ref[...], s_ref[...]
        o_ref[:, pl.ds(0, QKV_W)] = _rotate(_get(buf, 0, per), c2, s2, -1.0).astype(o_ref.dtype)
        o_ref[:, pl.ds(QKV_W, QKV_W)] = _rotate(_get(buf, per, per), c2, s2, -1.0).astype(o_ref.dtype)
        o_ref[:, pl.ds(2 * QKV_W, QKV_W)] = _get(buf, 2 * per, per).astype(o_ref.dtype)

    tab = pl.BlockSpec((ts, LANES), lambda i: (i, 0))
    return pl.pallas_call(
        body, name=name, grid=(s // ts,),
        in_specs=[_dilated_spec(d, ts) for _ in range(3) for d in ATTN_DILATIONS] + [tab, tab],
        out_specs=pl.BlockSpec((ts, 3 * QKV_W), lambda i: (i, 0)), out_shape=jax.ShapeDtypeStruct((s, 3 * QKV_W), BF16),
        scratch_shapes=[pltpu.VMEM((3 * QKV_W // LANES, ts, LANES), F32)], compiler_params=_params(1),
    )(*dq, *dk, *dv, cos2, sin2)


def _head_masks(shape):
    lane = lax.broadcasted_iota(jnp.int32, shape, 1)
    return [jnp.logical_and(lane >= h * HEAD_DIM, lane < (h + 1) * HEAD_DIM) for h in range(GROUP_W // HEAD_DIM)]


def _grad_split(name, do, o, lse):
    s, ts = do.shape[0], SPLIT_TS

    def body(do_ref, o_ref, l_ref, *rest):
        outs, buf = rest[:9], rest[9]
        dov = do_ref[...].astype(F32)
        prod = dov * o_ref[...].astype(F32)
        delta = jnp.zeros_like(prod)
        for hm in _head_masks(prod.shape):
            delta = jnp.where(hm, jnp.sum(jnp.where(hm, prod, 0.0), axis=1, keepdims=True), delta)
        _put(buf, 0, dov)
        _put(buf, CHUNKS_PER_GROUP, delta)
        _put(buf, 2 * CHUNKS_PER_GROUP, l_ref[...])
        for t in range(3):
            for g in range(3):
                _deinterleave_one(buf, t * CHUNKS_PER_GROUP, outs[3 * t + g], ATTN_DILATIONS[g], ts)

    nat = pl.BlockSpec((ts, GROUP_W), lambda i: (i, 0))
    dts = [BF16, F32, F32]
    outs = pl.pallas_call(
        body, name=name, grid=(s // ts,), in_specs=[nat, nat, nat],
        out_specs=[_dilated_spec(d, ts) for _ in range(3) for d in ATTN_DILATIONS],
        out_shape=[_dilated_shape(s, d, dt) for dt in dts for d in ATTN_DILATIONS],
        scratch_shapes=[pltpu.VMEM((3 * CHUNKS_PER_GROUP, ts, LANES), F32)], compiler_params=_params(1),
    )(do, o, lse)
    return outs[0:3], outs[3:6], outs[6:9]


def _band_mask(has_prev):
    qi = lax.broadcasted_iota(jnp.int32, (BLK, 2 * BLK), 0)
    kj = lax.broadcasted_iota(jnp.int32, (BLK, 2 * BLK), 1)
    in_prev = jnp.logical_and(jnp.logical_and(kj < BLK, kj >= qi), has_prev)
    return jnp.logical_or(in_prev, jnp.logical_and(kj >= BLK, kj - BLK <= qi))


def _dot_nt(a, b):
    return lax.dot_general(a, b, (((1,), (1,)), ((), ())), preferred_element_type=F32)


def _dot_tn(a, b):
    return lax.dot_general(a, b, (((0,), (0,)), ((), ())), preferred_element_type=F32)


def _dot(a, b):
    return jnp.dot(a, b, preferred_element_type=F32)


ATTN_BLK = (None, BLK, GROUP_W)


def _attn_specs(clamp):
    cur = pl.BlockSpec(ATTN_BLK, lambda r, n: (r, clamp(n), 0))
    prev = pl.BlockSpec(ATTN_BLK, lambda r, n: (r, jnp.maximum(clamp(n) - 1, 0), 0))
    return [cur, cur, prev, cur, prev]


def _attn_fwd(name, q, k, v):
    dil, rows, _ = q.shape
    nb = rows // BLK
    scale = HEAD_DIM ** -0.5

    def one_block(q, kp, kc, vp, vc, has_prev):
        qi = lax.broadcasted_iota(jnp.int32, (BLK, BLK), 0)
        kj = lax.broadcasted_iota(jnp.int32, (BLK, BLK), 1)
        mask_c, mask_p = kj <= qi, jnp.logical_and(kj >= qi, has_prev)
        o_acc = jnp.zeros((BLK, GROUP_W), F32)
        l_acc = jnp.zeros((BLK, GROUP_W), F32)
        for hm in _head_masks((BLK, GROUP_W)):
            qm = jnp.where(hm, q, jnp.zeros_like(q))
            sc = jnp.where(mask_c, _dot_nt(qm, kc) * scale, NEG_INF)
            sp = jnp.where(mask_p, _dot_nt(qm, kp) * scale, NEG_INF)
            m = jnp.maximum(jnp.max(sc, axis=1, keepdims=True), jnp.max(sp, axis=1, keepdims=True))
            pc, pp = jnp.exp(sc - m), jnp.exp(sp - m)
            den = jnp.sum(pc, axis=1, keepdims=True) + jnp.sum(pp, axis=1, keepdims=True)
            oh = (_dot(pc.astype(BF16), vc) + _dot(pp.astype(BF16), vp)) / den
            o_acc = jnp.where(hm, oh, o_acc)
            l_acc = jnp.where(hm, m + jnp.log(den), l_acc)
        return o_acc, l_acc

    def body(q_ref, kc_ref, kp_ref, vc_ref, vp_ref, o_ref, l_ref):
        lo, hi = pl.ds(0, BLK), pl.ds(BLK, BLK)
        o_ref[lo, :], l_ref[lo, :] = one_block(q_ref[lo, :], kp_ref[...], kc_ref[lo, :], vp_ref[...], vc_ref[lo, :],
                                               pl.program_id(1) > 0)
        o_ref[hi, :], l_ref[hi, :] = one_block(q_ref[hi, :], kc_ref[lo, :], kc_ref[hi, :], vc_ref[lo, :], vc_ref[hi, :], True)

    pair = pl.BlockSpec((None, 2 * BLK, GROUP_W), lambda r, n: (r, n, 0))
    prev = pl.BlockSpec(ATTN_BLK, lambda r, n: (r, jnp.maximum(2 * n - 1, 0), 0))
    return pl.pallas_call(
        body, name=name, grid=(dil, nb // 2), in_specs=[pair, pair, prev, pair, prev], out_specs=[pair, pair],
        out_shape=[jax.ShapeDtypeStruct(q.shape, F32)] * 2, compiler_params=_params(2),
    )(q, k, k, v, v)


def _attn_merge(name, os_, ls_):
    s, ts = os_[0].shape[0] * os_[0].shape[1], SPLIT_TS

    def body(*refs):
        o_refs, l_refs, o_ref, l_ref, buf = refs[0:3], refs[3:6], refs[6], refs[7], refs[8]
        _interleave(o_refs, buf, 0, ts)
        _interleave(l_refs, buf, 3 * CHUNKS_PER_GROUP, ts)
        o0, o1, o2 = [_get(buf, g * CHUNKS_PER_GROUP, CHUNKS_PER_GROUP) for g in range(3)]
        l0, l1, l2 = [_get(buf, (3 + g) * CHUNKS_PER_GROUP, CHUNKS_PER_GROUP) for g in range(3)]
        m = jnp.maximum(jnp.maximum(l0, l1), l2)
        e0, e1, e2 = jnp.exp(l0 - m), jnp.exp(l1 - m), jnp.exp(l2 - m)
        tot = e0 + e1 + e2
        o_ref[...] = ((e0 * o0 + e1 * o1 + e2 * o2) / tot).astype(o_ref.dtype)
        l_ref[...] = m + jnp.log(tot)

    nat = pl.BlockSpec((ts, GROUP_W), lambda i: (i, 0))
    return pl.pallas_call(
        body, name=name, grid=(s // ts,), in_specs=[_dilated_spec(d, ts) for _ in range(2) for d in ATTN_DILATIONS],
        out_specs=[nat, nat], out_shape=[jax.ShapeDtypeStruct((s, GROUP_W), BF16), jax.ShapeDtypeStruct((s, GROUP_W), F32)],
        scratch_shapes=[pltpu.VMEM((6 * CHUNKS_PER_GROUP, ts, LANES), F32)], compiler_params=_params(1),
    )(*os_, *ls_)


def _attn_bwd(name, q, k, v, do, delta, lse):
    dil, rows, _ = q.shape
    nb = rows // BLK
    scale = HEAD_DIM ** -0.5

    def body(q_ref, kc_ref, kp_ref, vc_ref, vp_ref, do_ref, dl_ref, l_ref, dq_ref, dk_ref, dv_ref, ck_ref, cv_ref):
        n = pl.program_id(1)

        @pl.when(n == 0)
        def _():
            ck_ref[...] = jnp.zeros_like(ck_ref)
            cv_ref[...] = jnp.zeros_like(cv_ref)

        @pl.when(n < nb)
        def _():
            mask = _band_mask(n > 0)
            q, dov = q_ref[...], do_ref[...]
            k = jnp.concatenate([kp_ref[...], kc_ref[...]], axis=0)
            v = jnp.concatenate([vp_ref[...], vc_ref[...]], axis=0)
            lb, db = l_ref[...], dl_ref[...]
            dq_acc = jnp.zeros((BLK, GROUP_W), F32)
            dk = jnp.zeros((2 * BLK, GROUP_W), F32)
            dv = jnp.zeros((2 * BLK, GROUP_W), F32)
            for hm in _head_masks((BLK, GROUP_W)):
                qm = jnp.where(hm, q, jnp.zeros_like(q))
                dom = jnp.where(hm, dov, jnp.zeros_like(dov))
                lh = jnp.max(jnp.where(hm, lb, NEG_INF), axis=1, keepdims=True)
                delta = jnp.max(jnp.where(hm, db, NEG_INF), axis=1, keepdims=True)
                p = jnp.exp(jnp.where(mask, _dot_nt(qm, k) * scale, NEG_INF) - lh)
                ds = (p * (_dot_nt(dom, v) - delta) * scale).astype(BF16)
                dq_acc = jnp.where(hm, _dot(ds, k), dq_acc)
                dk += _dot_tn(ds, qm)
                dv += _dot_tn(p.astype(BF16), dom)
            dq_ref[...] = dq_acc.astype(dq_ref.dtype)
            dk_ref[...] = (ck_ref[...] + dk[:BLK]).astype(dk_ref.dtype)
            dv_ref[...] = (cv_ref[...] + dv[:BLK]).astype(dv_ref.dtype)
            ck_ref[...] = dk[BLK:]
            cv_ref[...] = dv[BLK:]

        @pl.when(n == nb)
        def _():
            dk_ref[...] = ck_ref[...].astype(dk_ref.dtype)
            dv_ref[...] = cv_ref[...].astype(dv_ref.dtype)

    clamp = lambda n: jnp.minimum(n, nb - 1)
    qspec = pl.BlockSpec(ATTN_BLK, lambda r, n: (r, clamp(n), 0))
    kspec = pl.BlockSpec(ATTN_BLK, lambda r, n: (r, jnp.maximum(n - 1, 0), 0))
    return pl.pallas_call(
        body, name=name, grid=(dil, nb + 1), in_specs=_attn_specs(clamp) + [qspec, qspec, qspec],
        out_specs=[qspec, kspec, kspec], out_shape=[jax.ShapeDtypeStruct(q.shape, BF16)] * 3,
        scratch_shapes=[pltpu.VMEM((BLK, GROUP_W), F32), pltpu.VMEM((BLK, GROUP_W), F32)], compiler_params=_params(2),
    )(q, k, k, v, v, do, delta, lse)


CONV_TS = 128
HALO = 32
SHIFT_ROWS = CONV_TS + HALO - 8


def _make_shifts(buf, sh):
    for s_ in range(1, 8):
        sh[s_ - 1] = buf[pl.ds(s_, SHIFT_ROWS), :]


def _window(buf, sh, off, ts, cols):
    q, s_ = divmod(off, 8)
    if s_ == 0:
        return buf[pl.ds(off, ts), cols]
    return sh[s_ - 1, pl.ds(8 * q, ts), cols]


def _conv_fwd(name, u, w, b, lg, lb):
    s = u.shape[0]
    ts, per = CONV_TS, CONV_TS // HALO

    def body(a_ref, g_ref, ap_ref, gp_ref, w_ref, b_ref, lg_ref, lb_ref, c_ref, act_ref, buf, cbuf, sh):
        i = pl.program_id(0)
        buf[pl.ds(HALO, ts), :] = a_ref[...].astype(F32) * _sigmoid(g_ref[...].astype(F32))
        prev = ap_ref[...].astype(F32) * _sigmoid(gp_ref[...].astype(F32))
        buf[pl.ds(0, HALO), :] = jnp.where(i > 0, prev, 0.0)
        _make_shifts(buf, sh)
        rb = ts // 2
        for lc in range(D // LANES):
            cols = pl.ds(lc * LANES, LANES)
            for r0 in range(0, ts, rb):
                acc = jnp.broadcast_to(b_ref[:, cols], (rb, LANES))
                for j in range(CONV_K):
                    acc = acc + w_ref[pl.ds(j, 1), cols] * _window(buf, sh, r0 + HALO - (CONV_K - 1) + j, rb, cols)
                cbuf[pl.ds(r0, rb), cols] = acc
        c = cbuf[...]
        mu = jnp.mean(c, axis=-1, keepdims=True)
        xc = c - mu
        ln = xc * lax.rsqrt(jnp.mean(xc * xc, axis=-1, keepdims=True) + EPS) * lg_ref[...] + lb_ref[...]
        c_ref[...] = c.astype(c_ref.dtype)
        act_ref[...] = (ln * _sigmoid(ln)).astype(act_ref.dtype)

    halo = lambda cb: pl.BlockSpec((HALO, D), functools.partial(lambda i, cb: (jnp.maximum(i * per - 1, 0), cb), cb=cb))
    vec = pl.BlockSpec((1, D), lambda i: (0, 0))
    return pl.pallas_call(
        body, name=name, grid=(s // ts,),
        in_specs=[pl.BlockSpec((ts, D), lambda i: (i, 0)), pl.BlockSpec((ts, D), lambda i: (i, 1)), halo(0), halo(1),
                  pl.BlockSpec((HALO, D), lambda i: (0, 0)), vec, vec, vec],
        out_specs=[pl.BlockSpec((ts, D), lambda i: (i, 0))] * 2,
        out_shape=[jax.ShapeDtypeStruct((s, D), BF16)] * 2,
        scratch_shapes=[pltpu.VMEM((ts + HALO, D), F32), pltpu.VMEM((ts, D), F32), pltpu.VMEM((7, SHIFT_ROWS, D), F32)],
        compiler_params=_params(1),
    )(u, u, u, u, w, b, lg, lb)


def _conv_bwd(name, dc, u, w):
    s = u.shape[0]
    ts, per = CONV_TS, CONV_TS // HALO
    n_t = s // ts

    def body(dc_ref, dn_ref, a_ref, g_ref, ap_ref, gp_ref, w_ref, du_ref, dw_ref, db_ref, buf, dbuf, hbuf, sh, dsh):
        i = pl.program_id(0)
        a = a_ref[...].astype(F32)
        sg = _sigmoid(g_ref[...].astype(F32))
        buf[pl.ds(HALO, ts), :] = a * sg
        prev = ap_ref[...].astype(F32) * _sigmoid(gp_ref[...].astype(F32))
        buf[pl.ds(0, HALO), :] = jnp.where(i > 0, prev, 0.0)
        dcv = dc_ref[...]
        dbuf[pl.ds(0, ts), :] = dcv
        dbuf[pl.ds(ts, HALO), :] = jnp.where(i < n_t - 1, dn_ref[...], 0.0)

        @pl.when(i == 0)
        def _():
            dw_ref[...] = jnp.zeros_like(dw_ref)
            db_ref[...] = jnp.zeros_like(db_ref)

        db_ref[...] += _colsum(dcv)
        _make_shifts(buf, sh)
        _make_shifts(dbuf, dsh)
        rb = ts // 4
        for lc in range(D // LANES):
            cols = pl.ds(lc * LANES, LANES)
            for r0 in range(0, ts, rb):
                d0 = dbuf[pl.ds(r0, rb), cols]
                acc = jnp.zeros((rb, LANES), F32)
                for j in range(CONV_K):
                    acc = acc + w_ref[pl.ds(j, 1), cols] * _window(dbuf, dsh, r0 + CONV_K - 1 - j, rb, cols)
                    part = d0 * _window(buf, sh, r0 + HALO - (CONV_K - 1) + j, rb, cols)
                    dw_ref[pl.ds(8 * j, 8), cols] += jnp.sum(part.reshape(rb // 8, 8, LANES), axis=0)
                hbuf[pl.ds(r0, rb), cols] = acc
        dh = hbuf[...]
        du_ref[:, pl.ds(0, D)] = (dh * sg).astype(du_ref.dtype)
        du_ref[:, pl.ds(D, D)] = (dh * a * sg * (1.0 - sg)).astype(du_ref.dtype)

    halo = lambda cb: pl.BlockSpec((HALO, D), functools.partial(lambda i, cb: (jnp.maximum(i * per - 1, 0), cb), cb=cb))
    nxt = pl.BlockSpec((HALO, D), lambda i: (jnp.minimum((i + 1) * per, s // HALO - 1), 0))
    return pl.pallas_call(
        body, name=name, grid=(n_t,),
        in_specs=[pl.BlockSpec((ts, D), lambda i: (i, 0)), nxt, pl.BlockSpec((ts, D), lambda i: (i, 0)),
                  pl.BlockSpec((ts, D), lambda i: (i, 1)), halo(0), halo(1), pl.BlockSpec((HALO, D), lambda i: (0, 0))],
        out_specs=[pl.BlockSpec((ts, 2 * D), lambda i: (i, 0)), pl.BlockSpec((8 * CONV_K, D), lambda i: (0, 0)),
                   pl.BlockSpec((1, D), lambda i: (0, 0))],
        out_shape=[jax.ShapeDtypeStruct((s, 2 * D), BF16), jax.ShapeDtypeStruct((8 * CONV_K, D), F32),
                   jax.ShapeDtypeStruct((1, D), F32)],
        scratch_shapes=[pltpu.VMEM((ts + HALO, D), F32), pltpu.VMEM((ts + HALO, D), F32), pltpu.VMEM((ts, D), F32),
                        pltpu.VMEM((7, SHIFT_ROWS, D), F32), pltpu.VMEM((7, SHIFT_ROWS, D), F32)],
        compiler_params=_params(1),
    )(dc, dc, u, u, u, u, w)


def _mix_fwd(tag, x1, h1, gate, wts, cos2, sin2, nxt, job=None):
    ident = lambda accs, _t, _v: accs
    (qkv,), got = _with_job(_mm(f"{tag}_qkv", [(h1, wts["w_qkv"])], ident, [BF16], job=job), job)
    u = _mm(f"{tag}_u", [(h1, wts["w_u"])], ident, [BF16], tn=D)[0]
    ga, gc = _mm(f"{tag}_gates", [(h1, wts["w_ga"]), (h1, wts["w_gc"])], ident, [BF16] * 2)
    qd, kd, vd = _rope_split(f"{tag}_rope", qkv, cos2, sin2)
    per_group = [_attn_fwd(f"{tag}_attn{g}", qd[g], kd[g], vd[g]) for g in range(len(ATTN_DILATIONS))]
    o, lse = _attn_merge(f"{tag}_merge", [p[0] for p in per_group], [p[1] for p in per_group])
    cpre, act = _conv_fwd(f"{tag}_conv", u, wts["conv_w"], wts["conv_b"], wts["ln_g"], wts["ln_b"])

    def gate_epi(accs, tl, _v):
        ya, yc = accs
        return [_sigmoid(tl[0].astype(F32)) * ya + _sigmoid(tl[1].astype(F32)) * yc, ya, yc]

    y, ya, yc = _mm(f"{tag}_branch", [(o, wts["attn_wo"]), (act, wts["conv_wo"])], gate_epi, [BF16] * 3,
                    tiles=[(ga, 0), (gc, 0)])

    x2, f1, h_next = _project_out(f"{tag}_out", y, wts["w_out"], MX_COEF, x1, gate, nxt)
    return x2, h_next, (x1, h1, u, ga, gc, qd, kd, vd, o, lse, cpre, act, y, ya, yc, f1), got


def _ln_bwd_epi(accs, tl, vs):
    cv = tl[0].astype(F32)
    g, b = vs
    xc = cv - jnp.mean(cv, axis=-1, keepdims=True)
    rstd = lax.rsqrt(jnp.mean(xc * xc, axis=-1, keepdims=True) + EPS)
    xh = xc * rstd
    ln = xh * g + b
    sg = _sigmoid(ln)
    dln = accs[0] * (sg * (1.0 + ln * (1.0 - sg)))
    dxh = dln * g
    dc = rstd * (dxh - jnp.mean(dxh, axis=-1, keepdims=True) - xh * jnp.mean(dxh * xh, axis=-1, keepdims=True))
    return [dc, _colsum(dln * xh), _colsum(dln)]


def _mix_bwd(tag, dx2, dfm, saved, ng, sc, wts, cos2, sin2, prev, exchange_job):
    x1, h1, u, ga, gc, qd, kd, vd, o, lse, cpre, act, y, ya, yc, _f1 = saved

    def epi(accs, tl, _v):
        dy = accs[0]
        sa, sc_ = _sigmoid(tl[0].astype(F32)), _sigmoid(tl[1].astype(F32))
        return [dy * sa, dy * sc_, dy * tl[2].astype(F32) * sa * (1.0 - sa), dy * tl[3].astype(F32) * sc_ * (1.0 - sc_)]

    dya, dyc, dga, dgc = _mm(f"{tag}_bout", [(dfm, wts["w_out"])], epi, [BF16] * 4, nt=True,
                             tiles=[(ga, 0), (gc, 0), (ya, 0), (yc, 0)], chunk=2 * LANES)
    grads = {"w_out": _mm_tn(f"{tag}_dwout", y, dfm), "attn_wo": _mm_tn(f"{tag}_dwattn", o, dya),
             "conv_wo": _mm_tn(f"{tag}_dwconv", act, dyc)}
    ident = lambda accs, _t, _v: accs
    do = _mm(f"{tag}_battn", [(dya, wts["attn_wo"])], ident, [BF16], nt=True)[0]
    dc, dlg, dlb = _mm(f"{tag}_bconv", [(dyc, wts["conv_wo"])], _ln_bwd_epi, [F32], nt=True, tiles=[(cpre, 0)],
                       vecs=[wts["ln_g"], wts["ln_b"]], n_sums=2, tm=BWD_LAST_TM)
    du, dw8, dcb = _conv_bwd(f"{tag}_convb", dc, u, wts["conv_w"])
    dod, deltad, lsed = _grad_split(f"{tag}_gsplit", do, o, lse)
    dqs, dks, dvs = [], [], []
    for g in range(len(ATTN_DILATIONS)):
        dq, dk, dv = _attn_bwd(f"{tag}_attnb{g}", qd[g], kd[g], vd[g], dod[g], deltad[g], lsed[g])
        dqs.append(dq); dks.append(dk); dvs.append(dv)
    dqkv = _rope_join(f"{tag}_ropeb", dqs, dks, dvs, cos2, sin2)
    grads["w_in"] = jnp.concatenate([_mm_tn(f"{tag}_dwqkv", h1, dqkv), _mm_tn(f"{tag}_dwu", h1, du),
                                     _mm_tn(f"{tag}_dwga", h1, dga), _mm_tn(f"{tag}_dwgc", h1, dgc)], axis=1)
    job = exchange_job([_split(grads[k], ax) for k, ax in zip(MX_NAMES, MX_AXES)])
    prods = [(dqkv, wts["w_qkv"]), (du, wts["w_u"]), (dga, wts["w_ga"]), (dgc, wts["w_gc"])]
    dx1, df_prev, sums, dgate_prev, got = _bwd_last(f"{tag}_bin", prods, x1, dx2, ng, sc, prev, job)
    small = {"conv_w": dw8.reshape(CONV_K, 8, D).sum(axis=1), "conv_b": dcb, "ln_g": dlg, "ln_b": dlb}
    return dx1, df_prev, sums, dgate_prev, small, got


def _loss_head(name, x, target, fg, f_last, gate_last, coef_last):
    def body(rows, vecs):
        (xv, tv, fv), (g, gl) = rows, vecs
        xhat, r = _rms_parts(xv)
        err = xhat * g - tv
        dy = err * (1.0 / D)
        dx = _rms_bwd(dy * g, xhat, r)
        return [dx, coef_last * gl * dx], [_colsum(err * err), _colsum(dy * xhat), _colsum(coef_last * dx * fv.astype(F32))]

    return _rowwise(name, body, [x, target, f_last], [fg, gate_last], [(D, F32), (D, BF16)], [(1, D)] * 3)


FF_NAMES, FF_AXES = ("ffn_wg", "ffn_wu", "ffn_wd"), (1, 1, 0)
MX_NAMES, MX_AXES = ("w_in", "attn_wo", "conv_wo", "w_out"), (1, 1, 0, 0)
GROUPS = (("ff", 0, 0), ("mx", 0), ("ff", 0, 1), ("ff", 1, 0), ("mx", 1), ("ff", 1, 1))


def _group_name(grp):
    return "_".join(str(p) for p in grp)


def _mix_weights(blocks, small):
    w_in, attn_wo, conv_wo, w_out = [_join(b, a) for b, a in zip(blocks, MX_AXES)]
    return dict(small, w_qkv=w_in[:, :3 * QKV_W], w_u=w_in[:, 3 * QKV_W:3 * QKV_W + 2 * D],
                w_ga=w_in[:, 3 * QKV_W + 2 * D:3 * QKV_W + 3 * D], w_gc=w_in[:, 3 * QKV_W + 3 * D:],
                attn_wo=attn_wo, conv_wo=conv_wo, w_out=w_out)


def _local_step(x, target, mod, norm_g, shards, small_w, final_g, gather_job, exchange_job):
    cos2, sin2 = _rope_tables(x.shape[0])
    row = lambda a: a[None, :]
    sub = lambda grp: (grp[1], 2 * grp[2] if grp[0] == "ff" else 1)
    coef = lambda grp: FF_COEF if grp[0] == "ff" else MX_COEF
    ng = lambda grp: row(norm_g[sub(grp)])
    shift = lambda grp: row(mod[sub(grp)[0], 3 * sub(grp)[1]])
    scale = lambda grp: row(mod[sub(grp)[0], 3 * sub(grp)[1] + 1])
    gate = lambda grp: row(mod[sub(grp)[0], 3 * sub(grp)[1] + 2])
    blocks = _run_job("gather_" + _group_name(GROUPS[0]), gather_job(shards[GROUPS[0]]))
    h = _modulate("first_mod", x, ng(GROUPS[0]), scale(GROUPS[0]), shift(GROUPS[0]))
    saved, wts = [], []
    for n, grp in enumerate(GROUPS):
        last = n + 1 == len(GROUPS)
        job = None if last else gather_job(shards[GROUPS[n + 1]])
        nxt = None if last else (ng(GROUPS[n + 1]), scale(GROUPS[n + 1]), shift(GROUPS[n + 1]))
        if grp[0] == "ff":
            wts.append([_join(b, a) for b, a in zip(blocks, FF_AXES)])
            x, h, sv, blocks = _ffn_fwd(f"l{grp[1]}f{grp[2]}", x, h, gate(grp), *wts[-1], nxt, job=job)
        else:
            wts.append(_mix_weights(blocks, small_w[grp[1]]))
            x, h, sv, blocks = _mix_fwd(f"l{grp[1]}mx", x, h, gate(grp), wts[-1], cos2, sin2, nxt, job=job)
        saved.append(sv)
    dx, df, sq, dfg, dgate = _loss_head("loss_head", x, target, row(final_g), saved[-1][-1], gate(GROUPS[-1]), coef(GROUPS[-1]))
    loss = (0.5 / D) * jnp.sum(sq)
    dmod = [[None] * 3, [None] * 3]
    dng = [[None] * 3, [None] * 3]
    small, recv = [None, None], {}
    for n in reversed(range(len(GROUPS))):
        grp = GROUPS[n]
        l, i = sub(grp)
        prev = None if n == 0 else (saved[n - 1][-1], gate(GROUPS[n - 1]), coef(GROUPS[n - 1]))
        if grp[0] == "ff":
            dx, df_prev, sums, dgate_prev, recv[grp] = _ffn_bwd(
                f"l{l}f{grp[2]}", dx, df, saved[n], ng(grp), scale(grp), *wts[n], prev, exchange_job)
        else:
            dx, df_prev, sums, dgate_prev, small[l], recv[grp] = _mix_bwd(
                f"l{l}mx", dx, df, saved[n], ng(grp), scale(grp), wts[n], cos2, sin2, prev, exchange_job)
        dmod[l][i] = jnp.concatenate([sums[0], sums[1], dgate], axis=0)
        dng[l][i] = sums[2]
        df, dgate = df_prev, dgate_prev
    dmod = jnp.stack([jnp.concatenate(d, axis=0) for d in dmod])
    small = {k: jnp.stack([small[0][k].reshape(-1, D), small[1][k].reshape(-1, D)]) for k in small[0]}
    small = dict(small, norm_g=jnp.stack([jnp.concatenate(d, axis=0) for d in dng]), final_g=dfg.reshape(D))
    return loss, dx, dmod, small, recv


HBM_SPEC = pl.BlockSpec(memory_space=pl.ANY)


def _place():
    return lax.axis_index("x"), lax.axis_index("y"), lax.axis_index("c")


N_PEERS = N_DEV - 1


def _gather_job(arrays):
    n_a = len(arrays)

    def run(phase, x_refs, out_refs, send_sems, recv_sems, local_sems):
        x, y, c = _place()
        me, sibling = (x, y, c), (x, y, 1 - c)
        chips = [(1 - x, y), (x, 1 - y), (1 - x, 1 - y)]

        def copy(a, k, block, to, from_input=False):
            px, py, pc = block
            rows = out_refs[a].at[4 * px + 2 * py + pc]
            return pltpu.make_async_remote_copy(
                src_ref=x_refs[a] if from_input else rows, dst_ref=rows, send_sem=send_sems.at[a * N_PEERS + k],
                recv_sem=recv_sems.at[a * N_PEERS + k], device_id=to, device_id_type=pl.DeviceIdType.MESH)

        mine = [pltpu.make_async_copy(x_refs[a], out_refs[a].at[4 * x + 2 * y + c], local_sems.at[a]) for a in range(n_a)]
        first = []
        for j, chip in enumerate(chips):
            first += [copy(a, 1 + j, me, (*chip, c), from_input=True) for a in range(n_a)]
        first += [copy(a, 0, me, sibling, from_input=True) for a in range(n_a)]
        if phase == "start":
            for cp in mine + first:
                cp.start()
            return
        passed = []
        for j, chip in enumerate(chips):
            for a in range(n_a):
                copy(a, 1 + j, (*chip, c), me).wait_recv()
                passed.append(copy(a, 4 + j, (*chip, c), sibling))
                passed[-1].start()
        for a in range(n_a):
            copy(a, 0, sibling, me).wait_recv()
        for j, chip in enumerate(chips):
            for a in range(n_a):
                copy(a, 4 + j, (*chip, 1 - c), me).wait_recv()
        for cp in first + passed:
            cp.wait_send()
        for cp in mine:
            cp.wait()

    return dict(arrays=list(arrays), run=run,
                out_shape=[jax.ShapeDtypeStruct((N_DEV, *a.shape), a.dtype) for a in arrays])


def _job_scratch(job):
    n_a = len(job["arrays"])
    return [pltpu.SemaphoreType.DMA((n_a * N_PEERS,)), pltpu.SemaphoreType.DMA((n_a * N_PEERS,)), pltpu.SemaphoreType.DMA((n_a,))]


def _run_job(name, job):
    n_a = len(job["arrays"])

    def body(*refs):
        job["run"]("start", refs[:n_a], refs[n_a:2 * n_a], *refs[2 * n_a:])
        job["run"]("finish", refs[:n_a], refs[n_a:2 * n_a], *refs[2 * n_a:])

    return pl.pallas_call(
        body, name=name, out_shape=job["out_shape"], in_specs=[HBM_SPEC] * n_a, out_specs=[HBM_SPEC] * n_a,
        scratch_shapes=_job_scratch(job),
    )(*job["arrays"])


def _all_gather(name, arrays):
    return _run_job(name, _gather_job(arrays))


def _exchange_job(arrays):
    n_a = len(arrays)

    def run(phase, g_refs, out_refs, send_sems, recv_sems, local_sems):
        x, y, c = _place()
        my = 4 * x + 2 * y + c
        copies = [pltpu.make_async_copy(g_refs[a].at[my], out_refs[a].at[my], local_sems.at[a]) for a in range(n_a)]
        for k in (4, 2, 6, 1, 5, 3, 7):
            px = 1 - x if k & 4 else x
            py = 1 - y if k & 2 else y
            pc = 1 - c if k & 1 else c
            for a in range(n_a):
                copies.append(pltpu.make_async_remote_copy(
                    src_ref=g_refs[a].at[4 * px + 2 * py + pc], dst_ref=out_refs[a].at[my],
                    send_sem=send_sems.at[a * N_PEERS + k - 1], recv_sem=recv_sems.at[a * N_PEERS + k - 1],
                    device_id=(px, py, pc), device_id_type=pl.DeviceIdType.MESH))
        for cp in copies:
            if phase == "start":
                cp.start()
            else:
                cp.wait()

    return dict(arrays=list(arrays), run=run, out_shape=[jax.ShapeDtypeStruct(a.shape, a.dtype) for a in arrays])


SLAB_TS = 2048


def _sum_parts(name, parts):
    def body(rows, _v):
        tot = rows[0].astype(F32)
        for r in rows[1:]:
            tot = tot + r.astype(F32)
        return [tot], []

    return _rowwise(name, body, list(parts), [], [(parts[0].shape[1], F32)], ts=SLAB_TS)[0]


def _adamw(name, w, parts, m, v):
    def body(rows, _v):
        wv, mv, vv = rows[0], rows[1], rows[2]
        g = rows[3].astype(F32)
        for r in rows[4:]:
            g = g + r.astype(F32)
        m2 = ADAM_B1 * mv + (1.0 - ADAM_B1) * g
        v2 = ADAM_B2 * vv + (1.0 - ADAM_B2) * (g * g)
        m_hat = m2 / (1.0 - ADAM_B1 ** ADAM_STEP)
        v_hat = v2 / (1.0 - ADAM_B2 ** ADAM_STEP)
        delta = -ADAM_LR * (m_hat / (jnp.sqrt(v_hat) + ADAM_EPS) + ADAM_WD * wv)
        return [g, delta, m2, v2], []

    width = w.shape[1]
    return _rowwise(name, body, [w, m, v, *parts], [], [(width, F32)] * 4, ts=max(16, SLAB_TS * LANES // width))


def _to_slab(arrays, dtype):
    flat = jnp.concatenate([a.reshape(-1).astype(dtype) for a in arrays])
    rows = -(-flat.shape[0] // LANES)
    rows = -(-rows // 8) * 8 if rows <= SLAB_TS else -(-rows // SLAB_TS) * SLAB_TS
    return jnp.pad(flat, (0, rows * LANES - flat.shape[0])).reshape(rows, LANES)


def _from_slab(slab, shapes, lead=()):
    flat = slab.reshape(*lead, -1)
    out, at = [], 0
    for shp in shapes:
        size = 1
        for d in shp:
            size *= d
        out.append(flat[..., at:at + size].reshape(*lead, *shp))
        at += size
    return out


def _join(blocks, axis):
    full = jnp.moveaxis(blocks, 0, axis)
    return full.reshape(*full.shape[:axis], -1, *full.shape[axis + 2:])


def _split(full, axis):
    shp = full.shape
    return jnp.moveaxis(full.reshape(*shp[:axis], N_DEV, shp[axis] // N_DEV, *shp[axis + 1:]), axis, 0)


def kernel(x, c, ada_w, ada_b, norm_g, ffn_wg, ffn_wu, ffn_wd, w_in, attn_wo, conv_w, conv_b, conv_ln_g, conv_ln_b, conv_wo, w_out, final_g, loss_target, m_ada_w, m_ada_b, m_norm_g, m_ffn_wg, m_ffn_wu, m_ffn_wd, m_w_in, m_attn_wo, m_conv_w, m_conv_b, m_conv_ln_g, m_conv_ln_b, m_conv_wo, m_w_out, m_final_g, v_ada_w, v_ada_b, v_norm_g, v_ffn_wg, v_ffn_wu, v_ffn_wd, v_w_in, v_attn_wo, v_conv_w, v_conv_b, v_conv_ln_g, v_conv_ln_b, v_conv_wo, v_w_out, v_final_g):
    px, py, pc = _place()
    me = 4 * px + 2 * py + pc
    n_mod = ada_w.shape[2]
    big_w = dict(ffn_wg=ffn_wg, ffn_wu=ffn_wu, ffn_wd=ffn_wd, w_in=w_in, attn_wo=attn_wo, conv_wo=conv_wo, w_out=w_out)
    big_m = dict(ffn_wg=m_ffn_wg, ffn_wu=m_ffn_wu, ffn_wd=m_ffn_wd, w_in=m_w_in, attn_wo=m_attn_wo, conv_wo=m_conv_wo, w_out=m_w_out)
    big_v = dict(ffn_wg=v_ffn_wg, ffn_wu=v_ffn_wu, ffn_wd=v_ffn_wd, w_in=v_w_in, attn_wo=v_attn_wo, conv_wo=v_conv_wo, w_out=v_w_out)

    small_in = [c, norm_g, conv_w]
    g1 = _all_gather("gather_small", [_to_slab(small_in, F32)])[0]
    c_all, ng_blocks, cw_blocks = _from_slab(g1, [a.shape for a in small_in], lead=(N_DEV,))
    c_all = c_all.reshape(N_DEV, D)
    norm_g_full = _join(ng_blocks, 2)
    conv_w_full = _join(cw_blocks, 2)
    as2d = lambda a: a.reshape(-1, a.shape[-1])

    c_act = _rowwise("cond_silu", lambda rows, _v: ([rows[0] * _sigmoid(rows[0])], []), [c_all], [], [(D, BF16)])[0]
    c_pad = jnp.pad(c_act, ((0, LANES - N_DEV), (0, 0)))
    mod_cols = []
    for l in range(2):
        bias = lax.dynamic_slice_in_dim(ada_b[l], me * n_mod, n_mod)[None, :]
        out = _mm(f"mod{l}", [(c_pad, ada_w[l].astype(BF16))], lambda accs, _t, vs: [accs[0] + vs[0]], [F32], vecs=[bias])[0]
        mod_cols.append(out[:N_DEV])
    g2 = _all_gather("gather_mod", [_to_slab([jnp.stack(mod_cols)], F32)])[0]
    mod_all = _from_slab(g2, [(2, N_DEV, n_mod)], lead=(N_DEV,))[0]
    mod = lax.dynamic_index_in_dim(mod_all, me, axis=2, keepdims=False)
    mod = jnp.moveaxis(mod, 0, 1).reshape(2, 9, D)

    index = {grp: (grp[1], grp[2]) if grp[0] == "ff" else (grp[1],) for grp in GROUPS}
    names = {grp: FF_NAMES if grp[0] == "ff" else MX_NAMES for grp in GROUPS}
    shards = {grp: [big_w[k][index[grp]].astype(BF16) for k in names[grp]] for grp in GROUPS}
    small_l = [dict(conv_w=jnp.pad(conv_w_full[l], ((0, HALO - CONV_K), (0, 0))), conv_b=conv_b[l][None, :],
                    ln_g=conv_ln_g[l][None, :], ln_b=conv_ln_b[l][None, :]) for l in range(2)]

    loss, dx, dmod, small, recv = _local_step(x[0], loss_target[0], mod, norm_g_full, shards, small_l, final_g,
                                              _gather_job, _exchange_job)
    loss = lax.psum(loss, MESH_AXES)

    small_names = ["norm_g", "conv_w", "conv_b", "ln_g", "ln_b", "final_g"]
    small_parts = [dmod] + [small[k] for k in small_names]
    g3 = _all_gather("gather_small_grads", [_to_slab(small_parts, F32)])[0]
    tot = _sum_parts("sum_small_grads", [g3[k] for k in range(N_DEV)])
    _, g_ng, g_cw, g_cb, g_lg, g_lb, g_fg = _from_slab(tot, [a.shape for a in small_parts])
    g_ab = _from_slab(tot, [(2, 9 * D)])[0]
    dmod_all = _from_slab(g3, [dmod.shape], lead=(N_DEV,))[0].reshape(N_DEV, 2, 9 * D)
    dm_mine = lax.dynamic_slice_in_dim(dmod_all, me * n_mod, n_mod, axis=2)
    g_aw = jnp.stack([
        _mm_tn(f"dada_w{l}", c_pad, jnp.pad(dm_mine[:, l], ((0, LANES - N_DEV), (0, 0))).astype(BF16), out_dtype=F32)
        for l in range(2)])
    cols = lambda a: lax.dynamic_slice_in_dim(a, me * (D // N_DEV), D // N_DEV, axis=2)
    small_w = [ada_b, norm_g, conv_w, conv_b, conv_ln_g, conv_ln_b, final_g]
    small_m = [m_ada_b, m_norm_g, m_conv_w, m_conv_b, m_conv_ln_g, m_conv_ln_b, m_final_g]
    small_v = [v_ada_b, v_norm_g, v_conv_w, v_conv_b, v_conv_ln_g, v_conv_ln_b, v_final_g]
    small_g = [g_ab, cols(g_ng), cols(g_cw), g_cb, g_lg, g_lb, g_fg]
    s_shapes = [a.shape for a in small_w]
    s_out = _adamw("adamw_small", _to_slab(small_w, F32), [_to_slab(small_g, F32)], _to_slab(small_m, F32), _to_slab(small_v, F32))
    aw_out = [o.reshape(ada_w.shape) for o in _adamw("adamw_ada_w", as2d(ada_w), [as2d(g_aw)], as2d(m_ada_w), as2d(v_ada_w))]

    upd = {}
    for grp in GROUPS:
        for k, blocks in zip(names[grp], recv[grp]):
            parts = [(blocks, blocks.shape[2], 0, j) for j in range(N_DEV)]
            at = index[grp]
            upd[k, at] = _adamw(f"adamw_{k}_{_group_name(grp)}", big_w[k][at], parts, big_m[k][at], big_v[k][at])

    def stacked(k, i):
        if k in FF_NAMES:
            return jnp.stack([jnp.stack([upd[k, (l, j)][i] for j in range(2)]) for l in range(2)])
        return jnp.stack([upd[k, (l,)][i] for l in range(2)])

    def ordered(i):
        ab, ng, cw, cb, lg, lb, fg = _from_slab(s_out[i], s_shapes)
        bg = {k: stacked(k, i) for k in FF_NAMES + MX_NAMES}
        return [aw_out[i], ab, ng, bg["ffn_wg"], bg["ffn_wu"], bg["ffn_wd"], bg["w_in"], bg["attn_wo"], cw, cb, lg, lb,
                bg["conv_wo"], bg["w_out"], fg]

    return (loss, dx[None], *ordered(0), *ordered(1), *ordered(2), *ordered(3))
```

```python
import functools

import jax
import jax.numpy as jnp
from jax import lax
from jax.experimental import pallas as pl
from jax.experimental.pallas import tpu as pltpu

F32 = jnp.float32
BF16 = jnp.bfloat16

N_DEV = 8
D = 1024
D_FF = 2816
HEAD_DIM = 64
GROUP_W = 256
ATTN_DILATIONS = (1, 4, 16)
BLK = 128
QKV_W = 768
CONV_K = 31
ROPE_THETA = 10000.0
EPS = 1e-6
NEG_INF = -1e30
ADAM_LR, ADAM_B1, ADAM_B2, ADAM_EPS, ADAM_WD, ADAM_STEP = 0.001, 0.9, 0.999, 1e-08, 0.01, 10

V7X_VMEM_BYTES = 64 * 1024 * 1024
VMEM_LIMIT = V7X_VMEM_BYTES - 8 * 1024 * 1024
LANES = 128
MESH_AXES = ("x", "y", "c")


def _params(n_grid):
    return pltpu.CompilerParams(vmem_limit_bytes=VMEM_LIMIT, dimension_semantics=("arbitrary",) * n_grid)


def _sigmoid(v):
    return 1.0 / (1.0 + jnp.exp(-v))


def _mm(name, prods, epilogue, out_dtypes, *, nt=False, tiles=(), vecs=(), tm=512, tn=None, a_pre=None, chunk=None,
        job=None, n_sums=0):
    s = prods[0][0].shape[0]
    n = prods[0][1].shape[0] if nt else prods[0][1].shape[1]
    tn = n if tn is None else tn
    tm = min(tm, s)
    assert s % tm == 0 and n % tn == 0
    n_p, n_t, n_v = len(prods), len(tiles), len(vecs)
    dn = (((1,), (1,)), ((), ())) if nt else (((1,), (0,)), ((), ()))

    chunk = tn if chunk is None else chunk
    bounds = [(c0, min(chunk, tn - c0)) for c0 in range(0, tn, chunk)]

    n_o = len(out_dtypes) + n_sums
    n_j = len(job["arrays"]) if job else 0
    n_steps = (n // tn, s // tm)

    def body(*refs):
        p_refs, rest = refs[:2 * n_p], refs[2 * n_p:]
        t_refs, v_refs, rest = rest[:n_t], rest[n_t:n_t + n_v], rest[n_t + n_v:]
        j_in, o_refs, j_out, sems = rest[:n_j], rest[n_j:n_j + n_o], rest[n_j + n_o:2 * n_j + n_o], rest[2 * n_j + n_o:]
        o_refs, s_refs = o_refs[:n_o - n_sums], o_refs[n_o - n_sums:]
        if job:
            @pl.when(jnp.logical_and(pl.program_id(0) == 0, pl.program_id(1) == 0))
            def _():
                job["run"]("start", j_in, j_out, *sems)
        lhs = []
        for p in range(n_p):
            a = p_refs[2 * p][...]
            lhs.append(a if a_pre is None else a_pre(a))
        for c0, cw in bounds:
            cols = pl.ds(c0, cw)
            accs = []
            for p in range(n_p):
                b = p_refs[2 * p + 1][cols, :] if nt else p_refs[2 * p + 1][:, cols]
                accs.append(lax.dot_general(lhs[p], b, dn, preferred_element_type=F32))
            outs = epilogue(accs, [t[:, cols] for t in t_refs], [v[:, cols] for v in v_refs])
            for o_ref, o in zip(o_refs, outs[:len(o_refs)], strict=True):
                o_ref[:, cols] = o.astype(o_ref.dtype)
            if n_sums:
                first_row_tile = pl.program_id(1) == 0

                @pl.when(first_row_tile)
                def _():
                    for s_ref, part in zip(s_refs, outs[len(o_refs):], strict=True):
                        s_ref[:, cols] = part

                @pl.when(jnp.logical_not(first_row_tile))
                def _():
                    for s_ref, part in zip(s_refs, outs[len(o_refs):], strict=True):
                        s_ref[:, cols] += part
        if job:
            @pl.when(jnp.logical_and(pl.program_id(0) == n_steps[0] - 1, pl.program_id(1) == n_steps[1] - 1))
            def _():
                job["run"]("finish", j_in, j_out, *sems)

    in_specs = []
    operands = []
    for a, b in prods:
        k = a.shape[1]
        in_specs.append(pl.BlockSpec((tm, k), lambda j, i: (i, 0)))
        in_specs.append(pl.BlockSpec((tn, k), lambda j, i: (j, 0)) if nt else pl.BlockSpec((k, tn), lambda j, i: (0, j)))
        operands += [a, b]
    for arr, off in tiles:
        in_specs.append(pl.BlockSpec((tm, tn), functools.partial(lambda j, i, off: (i, j + off), off=off)))
        operands.append(arr)
    for v in vecs:
        in_specs.append(pl.BlockSpec((1, tn), lambda j, i: (0, j)))
        operands.append(v)
    out_specs = [pl.BlockSpec((tm, tn), lambda j, i: (i, j)) for _ in out_dtypes]
    out_specs += [pl.BlockSpec((1, tn), lambda j, i: (0, j)) for _ in range(n_sums)]
    out_shape = [jax.ShapeDtypeStruct((s, n), dt) for dt in out_dtypes] + [jax.ShapeDtypeStruct((1, n), F32)] * n_sums
    scratch = []
    if job:
        in_specs += [HBM_SPEC] * n_j
        operands += job["arrays"]
        out_specs += [HBM_SPEC] * n_j
        out_shape += job["out_shape"]
        scratch = _job_scratch(job)
    out = pl.pallas_call(
        body, name=name, grid=n_steps, in_specs=in_specs, out_specs=out_specs, out_shape=out_shape,
        scratch_shapes=scratch, compiler_params=_params(2),
    )(*operands)
    return (out[:n_o], out[n_o:]) if job else out


def _mm_tn(name, a, b, *, tk=512, tn=None, out_dtype=BF16):
    s, m = a.shape
    n = b.shape[1]
    tn = n if tn is None else tn
    tk = min(tk, s)
    n_k = s // tk
    assert s % tk == 0 and n % tn == 0

    def body(a_ref, b_ref, o_ref, acc_ref):
        k = pl.program_id(1)

        @pl.when(k == 0)
        def _():
            acc_ref[...] = jnp.zeros_like(acc_ref)

        acc_ref[...] += lax.dot_general(a_ref[...], b_ref[...], (((0,), (0,)), ((), ())), preferred_element_type=F32)

        @pl.when(k == n_k - 1)
        def _():
            o_ref[...] = acc_ref[...].astype(o_ref.dtype)

    return pl.pallas_call(
        body, name=name, grid=(n // tn, n_k),
        in_specs=[pl.BlockSpec((tk, m), lambda j, k: (k, 0)), pl.BlockSpec((tk, tn), lambda j, k: (k, j))],
        out_specs=pl.BlockSpec((m, tn), lambda j, k: (0, j)),
        out_shape=jax.ShapeDtypeStruct((m, n), out_dtype),
        scratch_shapes=[pltpu.VMEM((m, tn), F32)],
        compiler_params=_params(2),
    )(a, b)


def _pick_rows(s, target):
    if s <= target:
        return s
    return max(t for t in range(16, target + 1, 16) if s % t == 0)


def _rowwise(name, body, rows, vecs, outs, accs=(), *, ts=512):
    rows = [(r if isinstance(r, tuple) else (r, r.shape[1], 0)) for r in rows]
    rows = [r if len(r) == 4 else (*r, None) for r in rows]
    s = rows[0][0].shape[0]
    ts = _pick_rows(s, ts)
    n_r, n_v, n_o, n_a = len(rows), len(vecs), len(outs), len(accs)

    def kbody(*refs):
        r_refs, v_refs = refs[:n_r], refs[n_r:n_r + n_v]
        o_refs, a_refs = refs[n_r + n_v:n_r + n_v + n_o], refs[n_r + n_v + n_o:]
        res_o, res_a = body([r[...] for r in r_refs], [v[...] for v in v_refs])
        for o_ref, o in zip(o_refs, res_o, strict=True):
            o_ref[...] = o.astype(o_ref.dtype)
        if n_a:
            first = pl.program_id(0) == 0

            @pl.when(first)
            def _():
                for a_ref, a in zip(a_refs, res_a, strict=True):
                    a_ref[...] = a

            @pl.when(jnp.logical_not(first))
            def _():
                for a_ref, a in zip(a_refs, res_a, strict=True):
                    a_ref[...] += a

    in_specs = [
        pl.BlockSpec((ts, w), functools.partial(lambda i, cb: (i, cb), cb=cb)) if lead is None else
        pl.BlockSpec((None, ts, w), functools.partial(lambda i, cb, lead: (lead, i, cb), cb=cb, lead=lead))
        for _, w, cb, lead in rows]
    in_specs += [pl.BlockSpec(v.shape, functools.partial(lambda i, nd: (0,) * nd, nd=v.ndim)) for v in vecs]
    out_specs = [pl.BlockSpec((ts, w), lambda i: (i, 0)) for w, _ in outs]
    out_specs += [pl.BlockSpec(shp, functools.partial(lambda i, nd: (0,) * nd, nd=len(shp))) for shp in accs]
    out_shape = [jax.ShapeDtypeStruct((s, w), dt) for w, dt in outs] + [jax.ShapeDtypeStruct(shp, F32) for shp in accs]
    return pl.pallas_call(
        kbody, name=name, grid=(s // ts,), in_specs=in_specs, out_specs=out_specs, out_shape=out_shape,
        compiler_params=_params(1),
    )(*[r[0] for r in rows], *vecs)


def _colsum(v):
    return jnp.sum(v, axis=0, keepdims=True)


def _rms_parts(x):
    r = lax.rsqrt(jnp.mean(x * x, axis=-1, keepdims=True) + EPS)
    return x * r, r


def _rms_bwd(dxhat, xhat, r):
    return r * (dxhat - xhat * jnp.mean(dxhat * xhat, axis=-1, keepdims=True))


def _modulate(name, x, ng, sc, sh):
    def body(rows, vecs):
        (xv,), (g, s_, b) = rows, vecs
        xhat, _ = _rms_parts(xv)
        return [xhat * g * (1.0 + s_) + b], []

    return _rowwise(name, body, [x], [ng, sc, sh], [(D, BF16)])[0]


def _residual_epi(coef, with_next):
    def epi(accs, tl, vs):
        x_out = tl[0] + coef * vs[0] * accs[0]
        outs = [x_out, accs[0]]
        if with_next:
            xhat, _ = _rms_parts(x_out)
            outs.append(xhat * vs[1] * (1.0 + vs[2]) + vs[3])
        return outs

    return epi


def _bwd_last_epi(coef_prev):
    def epi(accs, tl, vs):
        dh = accs[0]
        for acc in accs[1:]:
            dh = dh + acc
        xhat, r = _rms_parts(tl[0])
        dn = dh * (1.0 + vs[1])
        dx_in = tl[1] + _rms_bwd(dn * vs[0], xhat, r)
        outs, sums = [dx_in], [_colsum(dh), _colsum(dh * xhat * vs[0]), _colsum(dn * xhat)]
        if coef_prev is not None:
            outs.append(coef_prev * vs[2] * dx_in)
            sums.append(_colsum(coef_prev * dx_in * tl[2].astype(F32)))
        return outs + sums

    return epi


BWD_LAST_TM = 256


def _bwd_last(name, prods, x, dxo, ng, sc, prev, job):
    tiles, vecs, dts = [(x, 0), (dxo, 0)], [ng, sc], [F32]
    if prev is not None:
        tiles, vecs, dts = tiles + [(prev[0], 0)], vecs + [prev[1]], dts + [BF16]
    outs, got = _mm(name, prods, _bwd_last_epi(None if prev is None else prev[2]), dts, nt=True, tiles=tiles, vecs=vecs,
                    n_sums=3 + (prev is not None), tm=BWD_LAST_TM, job=job)
    if prev is None:
        return outs[0], None, tuple(outs[1:4]), None, got
    return outs[0], outs[1], tuple(outs[2:5]), outs[5], got


def _with_job(result, job):
    return result if job else (result, None)


FF_COEF, MX_COEF = 0.5, 1.0


def _project_out(name, y, w, coef, x, gate, nxt):
    outs = _mm(name, [(y, w)], _residual_epi(coef, nxt is not None), [F32, BF16] + [BF16] * (nxt is not None),
               tiles=[(x, 0)], vecs=[gate, *(nxt or ())])
    return outs[0], outs[1], (outs[2] if nxt is not None else None)


def _ffn_fwd(tag, x, h, gate, wg, wu, wd, nxt, job=None):
    def up_epi(accs, _t, _v):
        a, u = accs
        return [a, u, a * _sigmoid(a) * u]

    (a, u, t), got = _with_job(_mm(f"{tag}_up", [(h, wg), (h, wu)], up_epi, [BF16] * 3, tn=D_FF // 2, job=job), job)
    x_out, f, h_next = _project_out(f"{tag}_down", t, wd, FF_COEF, x, gate, nxt)
    return x_out, h_next, (x, h, a, u, t, f), got


def _ffn_bwd(tag, dxo, df, saved, ng, sc, wg, wu, wd, prev, exchange_job):
    x, h, a, u, t, _f = saved

    def epi(accs, tl, _v):
        dt = accs[0]
        av, uv = tl[0].astype(F32), tl[1].astype(F32)
        sg = _sigmoid(av)
        return [dt * uv * (sg * (1.0 + av * (1.0 - sg))), dt * (av * sg)]

    da, du = _mm(f"{tag}_bdown", [(df, wd)], epi, [BF16] * 2, nt=True, tiles=[(a, 0), (u, 0)], tn=D_FF // 2,
                 chunk=3 * LANES)
    grads = (_mm_tn(f"{tag}_dwg", h, da), _mm_tn(f"{tag}_dwu", h, du), _mm_tn(f"{tag}_dwd", t, df))
    job = exchange_job([_split(g, ax) for g, ax in zip(grads, FF_AXES)])
    return _bwd_last(f"{tag}_bup", [(da, wg), (du, wu)], x, dxo, ng, sc, prev, job)


def _rope_tables(s):
    half = HEAD_DIM // 2
    inv_freq = ROPE_THETA ** (-(jnp.arange(half, dtype=F32) * 2.0 / HEAD_DIM))
    ang = jnp.arange(s, dtype=F32)[:, None] * inv_freq[None, :]
    cos, sin = jnp.cos(ang), jnp.sin(ang)
    return jnp.tile(jnp.concatenate([cos, cos], axis=1), (1, 2)), jnp.tile(jnp.concatenate([-sin, sin], axis=1), (1, 2))


def _rotate(v, cos2, sin2, sign):
    w = v.shape[1]
    lane = lax.broadcasted_iota(jnp.int32, v.shape, 1)
    partner = jnp.where(lane % HEAD_DIM < HEAD_DIM // 2, pltpu.roll(v, w - HEAD_DIM // 2, 1), pltpu.roll(v, HEAD_DIM // 2, 1))
    reps = w // LANES
    return v * jnp.tile(cos2, (1, reps)) + partner * (sign * jnp.tile(sin2, (1, reps)))


SPLIT_TS = 512


def _dilated_spec(dil, ts):
    return pl.BlockSpec((dil, ts // dil, GROUP_W), lambda i: (0, i, 0))


def _dilated_shape(s, dil, dtype):
    return jax.ShapeDtypeStruct((dil, s // dil, GROUP_W), dtype)


CHUNKS_PER_GROUP = GROUP_W // LANES


def _put(buf, chunk0, val):
    for c in range(val.shape[1] // LANES):
        buf[chunk0 + c] = val[:, c * LANES:(c + 1) * LANES]


def _get(buf, chunk0, n):
    return jnp.concatenate([buf[chunk0 + c] for c in range(n)], axis=1)


def _strided_rows(r, dil, ts):
    return pl.ds(r, ts // dil, stride=dil) if dil > 1 else pl.ds(0, ts)


def _deinterleave_one(buf, chunk0, out_ref, dil, ts):
    for half in range(CHUNKS_PER_GROUP):
        for r in range(dil):
            src = buf.at[chunk0 + half][_strided_rows(r, dil, ts), :]
            out_ref.at[r][:, pl.ds(half * LANES, LANES)] = src.astype(out_ref.dtype)


def _interleave_one(in_ref, buf, chunk0, dil, ts):
    for half in range(CHUNKS_PER_GROUP):
        for r in range(dil):
            src = in_ref.at[r][:, pl.ds(half * LANES, LANES)]
            buf.at[chunk0 + half][_strided_rows(r, dil, ts), :] = src.astype(F32)


def _deinterleave(buf, chunk0, out_refs, ts):
    for g, dil in enumerate(ATTN_DILATIONS):
        _deinterleave_one(buf, chunk0 + g * CHUNKS_PER_GROUP, out_refs[g], dil, ts)


def _interleave(in_refs, buf, chunk0, ts):
    for g, dil in enumerate(ATTN_DILATIONS):
        _interleave_one(in_refs[g], buf, chunk0 + g * CHUNKS_PER_GROUP, dil, ts)


def _rope_split(name, qkv, cos2, sin2):
    s, ts = qkv.shape[0], SPLIT_TS

    def body(qkv_ref, c_ref, s_ref, *rest):
        outs, buf = rest[:9], rest[9]
        c2, s2 = c_ref[...], s_ref[...]
        per = QKV_W // LANES
        _put(buf, 0, _rotate(qkv_ref[:, pl.ds(0, QKV_W)].astype(F32), c2, s2, 1.0))
        _put(buf, per, _rotate(qkv_ref[:, pl.ds(QKV_W, QKV_W)].astype(F32), c2, s2, 1.0))
        _put(buf, 2 * per, qkv_ref[:, pl.ds(2 * QKV_W, QKV_W)].astype(F32))
        for t in range(3):
            _deinterleave(buf, t * per, outs[3 * t:3 * t + 3], ts)

    tab = pl.BlockSpec((ts, LANES), lambda i: (i, 0))
    outs = pl.pallas_call(
        body, name=name, grid=(s // ts,), in_specs=[pl.BlockSpec((ts, 3 * QKV_W), lambda i: (i, 0)), tab, tab],
        out_specs=[_dilated_spec(d, ts) for _ in range(3) for d in ATTN_DILATIONS],
        out_shape=[_dilated_shape(s, d, BF16) for _ in range(3) for d in ATTN_DILATIONS],
        scratch_shapes=[pltpu.VMEM((3 * QKV_W // LANES, ts, LANES), F32)], compiler_params=_params(1),
    )(qkv, cos2, sin2)
    return outs[0:3], outs[3:6], outs[6:9]


def _rope_join(name, dq, dk, dv, cos2, sin2):
    s, ts = cos2.shape[0], SPLIT_TS

    def body(*refs):
        ins, c_ref, s_ref, o_ref, buf = refs[:9], refs[9], refs[10], refs[11], refs[12]
        per = QKV_W // LANES
        for t in range(3):
            _interleave(ins[3 * t:3 * t + 3], buf, t * per, ts)
        c2, s2 = c_ref[...], s_ref[...]
        o_ref[:, pl.ds(0, QKV_W)] = _rotate(_get(buf, 0, per), c2, s2, -1.0).astype(o_ref.dtype)
        o_ref[:, pl.ds(QKV_W, QKV_W)] = _rotate(_get(buf, per, per), c2, s2, -1.0).astype(o_ref.dtype)
        o_ref[:, pl.ds(2 * QKV_W, QKV_W)] = _get(buf, 2 * per, per).astype(o_ref.dtype)

    tab = pl.BlockSpec((ts, LANES), lambda i: (i, 0))
    return pl.pallas_call(
        body, name=name, grid=(s // ts,),
        in_specs=[_dilated_spec(d, ts) for _ in range(3) for d in ATTN_DILATIONS] + [tab, tab],
        out_specs=pl.BlockSpec((ts, 3 * QKV_W), lambda i: (i, 0)), out_shape=jax.ShapeDtypeStruct((s, 3 * QKV_W), BF16),
        scratch_shapes=[pltpu.VMEM((3 * QKV_W // LANES, ts, LANES), F32)], compiler_params=_params(1),
    )(*dq, *dk, *dv, cos2, sin2)


def _head_masks(shape):
    lane = lax.broadcasted_iota(jnp.int32, shape, 1)
    return [jnp.logical_and(lane >= h * HEAD_DIM, lane < (h + 1) * HEAD_DIM) for h in range(GROUP_W // HEAD_DIM)]


def _grad_split(name, do, o, lse):
    s, ts = do.shape[0], SPLIT_TS

    def body(do_ref, o_ref, l_ref, *rest):
        outs, buf = rest[:9], rest[9]
        dov = do_ref[...].astype(F32)
        prod = dov * o_ref[...].astype(F32)
        delta = jnp.zeros_like(prod)
        for hm in _head_masks(prod.shape):
            delta = jnp.where(hm, jnp.sum(jnp.where(hm, prod, 0.0), axis=1, keepdims=True), delta)
        _put(buf, 0, dov)
        _put(buf, CHUNKS_PER_GROUP, delta)
        _put(buf, 2 * CHUNKS_PER_GROUP, l_ref[...])
        for t in range(3):
            for g in range(3):
                _deinterleave_one(buf, t * CHUNKS_PER_GROUP, outs[3 * t + g], ATTN_DILATIONS[g], ts)

    nat = pl.BlockSpec((ts, GROUP_W), lambda i: (i, 0))
    dts = [BF16, F32, F32]
    outs = pl.pallas_call(
        body, name=name, grid=(s // ts,), in_specs=[nat, nat, nat],
        out_specs=[_dilated_spec(d, ts) for _ in range(3) for d in ATTN_DILATIONS],
        out_shape=[_dilated_shape(s, d, dt) for dt in dts for d in ATTN_DILATIONS],
        scratch_shapes=[pltpu.VMEM((3 * CHUNKS_PER_GROUP, ts, LANES), F32)], compiler_params=_params(1),
    )(do, o, lse)
    return outs[0:3], outs[3:6], outs[6:9]


def _band_mask(has_prev):
    qi = lax.broadcasted_iota(jnp.int32, (BLK, 2 * BLK), 0)
    kj = lax.broadcasted_iota(jnp.int32, (BLK, 2 * BLK), 1)
    in_prev = jnp.logical_and(jnp.logical_and(kj < BLK, kj >= qi), has_prev)
    return jnp.logical_or(in_prev, jnp.logical_and(kj >= BLK, kj - BLK <= qi))


def _dot_nt(a, b):
    return lax.dot_general(a, b, (((1,), (1,)), ((), ())), preferred_element_type=F32)


def _dot_tn(a, b):
    return lax.dot_general(a, b, (((0,), (0,)), ((), ())), preferred_element_type=F32)


def _dot(a, b):
    return jnp.dot(a, b, preferred_element_type=F32)


ATTN_BLK = (None, BLK, GROUP_W)


def _attn_specs(clamp):
    cur = pl.BlockSpec(ATTN_BLK, lambda r, n: (r, clamp(n), 0))
    prev = pl.BlockSpec(ATTN_BLK, lambda r, n: (r, jnp.maximum(clamp(n) - 1, 0), 0))
    return [cur, cur, prev, cur, prev]


def _attn_fwd(name, q, k, v):
    dil, rows, _ = q.shape
    nb = rows // BLK
    scale = HEAD_DIM ** -0.5

    def one_block(q, kp, kc, vp, vc, has_prev):
        qi = lax.broadcasted_iota(jnp.int32, (BLK, BLK), 0)
        kj = lax.broadcasted_iota(jnp.int32, (BLK, BLK), 1)
        mask_c, mask_p = kj <= qi, jnp.logical_and(kj >= qi, has_prev)
        o_acc = jnp.zeros((BLK, GROUP_W), F32)
        l_acc = jnp.zeros((BLK, GROUP_W), F32)
        for hm in _head_masks((BLK, GROUP_W)):
            qm = jnp.where(hm, q, jnp.zeros_like(q))
            sc = jnp.where(mask_c, _dot_nt(qm, kc) * scale, NEG_INF)
            sp = jnp.where(mask_p, _dot_nt(qm, kp) * scale, NEG_INF)
            m = jnp.maximum(jnp.max(sc, axis=1, keepdims=True), jnp.max(sp, axis=1, keepdims=True))
            pc, pp = jnp.exp(sc - m), jnp.exp(sp - m)
            den = jnp.sum(pc, axis=1, keepdims=True) + jnp.sum(pp, axis=1, keepdims=True)
            oh = (_dot(pc.astype(BF16), vc) + _dot(pp.astype(BF16), vp)) / den
            o_acc = jnp.where(hm, oh, o_acc)
            l_acc = jnp.where(hm, m + jnp.log(den), l_acc)
        return o_acc, l_acc

    def body(q_ref, kc_ref, kp_ref, vc_ref, vp_ref, o_ref, l_ref):
        lo, hi = pl.ds(0, BLK), pl.ds(BLK, BLK)
        o_ref[lo, :], l_ref[lo, :] = one_block(q_ref[lo, :], kp_ref[...], kc_ref[lo, :], vp_ref[...], vc_ref[lo, :],
                                               pl.program_id(1) > 0)
        o_ref[hi, :], l_ref[hi, :] = one_block(q_ref[hi, :], kc_ref[lo, :], kc_ref[hi, :], vc_ref[lo, :], vc_ref[hi, :], True)

    pair = pl.BlockSpec((None, 2 * BLK, GROUP_W), lambda r, n: (r, n, 0))
    prev = pl.BlockSpec(ATTN_BLK, lambda r, n: (r, jnp.maximum(2 * n - 1, 0), 0))
    return pl.pallas_call(
        body, name=name, grid=(dil, nb // 2), in_specs=[pair, pair, prev, pair, prev], out_specs=[pair, pair],
        out_shape=[jax.ShapeDtypeStruct(q.shape, F32)] * 2, compiler_params=_params(2),
    )(q, k, k, v, v)


def _attn_merge(name, os_, ls_):
    s, ts = os_[0].shape[0] * os_[0].shape[1], SPLIT_TS

    def body(*refs):
        o_refs, l_refs, o_ref, l_ref, buf = refs[0:3], refs[3:6], refs[6], refs[7], refs[8]
        _interleave(o_refs, buf, 0, ts)
        _interleave(l_refs, buf, 3 * CHUNKS_PER_GROUP, ts)
        o0, o1, o2 = [_get(buf, g * CHUNKS_PER_GROUP, CHUNKS_PER_GROUP) for g in range(3)]
        l0, l1, l2 = [_get(buf, (3 + g) * CHUNKS_PER_GROUP, CHUNKS_PER_GROUP) for g in range(3)]
        m = jnp.maximum(jnp.maximum(l0, l1), l2)
        e0, e1, e2 = jnp.exp(l0 - m), jnp.exp(l1 - m), jnp.exp(l2 - m)
        tot = e0 + e1 + e2
        o_ref[...] = ((e0 * o0 + e1 * o1 + e2 * o2) / tot).astype(o_ref.dtype)
        l_ref[...] = m + jnp.log(tot)

    nat = pl.BlockSpec((ts, GROUP_W), lambda i: (i, 0))
    return pl.pallas_call(
        body, name=name, grid=(s // ts,), in_specs=[_dilated_spec(d, ts) for _ in range(2) for d in ATTN_DILATIONS],
        out_specs=[nat, nat], out_shape=[jax.ShapeDtypeStruct((s, GROUP_W), BF16), jax.ShapeDtypeStruct((s, GROUP_W), F32)],
        scratch_shapes=[pltpu.VMEM((6 * CHUNKS_PER_GROUP, ts, LANES), F32)], compiler_params=_params(1),
    )(*os_, *ls_)


def _attn_bwd(name, q, k, v, do, delta, lse):
    dil, rows, _ = q.shape
    nb = rows // BLK
    scale = HEAD_DIM ** -0.5

    def one_block(q, dov, lb, db, kp, kc, vp, vc, has_prev):
        mask = _band_mask(has_prev)
        k = jnp.concatenate([kp, kc], axis=0)
        v = jnp.concatenate([vp, vc], axis=0)
        dq_acc = jnp.zeros((BLK, GROUP_W), F32)
        dk = jnp.zeros((2 * BLK, GROUP_W), F32)
        dv = jnp.zeros((2 * BLK, GROUP_W), F32)
        for hm in _head_masks((BLK, GROUP_W)):
            qm = jnp.where(hm, q, jnp.zeros_like(q))
            dom = jnp.where(hm, dov, jnp.zeros_like(dov))
            lh = jnp.max(jnp.where(hm, lb, NEG_INF), axis=1, keepdims=True)
            delta = jnp.max(jnp.where(hm, db, NEG_INF), axis=1, keepdims=True)
            p = jnp.exp(jnp.where(mask, _dot_nt(qm, k) * scale, NEG_INF) - lh)
            ds = (p * (_dot_nt(dom, v) - delta) * scale).astype(BF16)
            dq_acc = jnp.where(hm, _dot(ds, k), dq_acc)
            dk += _dot_tn(ds, qm)
            dv += _dot_tn(p.astype(BF16), dom)
        return dq_acc, dk, dv

    n_pairs = nb // 2

    def body(q_ref, kc_ref, kp_ref, vc_ref, vp_ref, do_ref, dl_ref, l_ref, dq_ref, dk_ref, dv_ref, hk_ref, hv_ref):
        n = pl.program_id(1)
        lo, hi = pl.ds(0, BLK), pl.ds(BLK, BLK)

        @pl.when(n == 0)
        def _():
            hk_ref[...] = jnp.zeros_like(hk_ref)
            hv_ref[...] = jnp.zeros_like(hv_ref)

        @pl.when(n < n_pairs)
        def _():
            dq0, dk0, dv0 = one_block(q_ref[lo, :], do_ref[lo, :], l_ref[lo, :], dl_ref[lo, :],
                                      kp_ref[...], kc_ref[lo, :], vp_ref[...], vc_ref[lo, :], n > 0)
            dq1, dk1, dv1 = one_block(q_ref[hi, :], do_ref[hi, :], l_ref[hi, :], dl_ref[hi, :],
                                      kc_ref[lo, :], kc_ref[hi, :], vc_ref[lo, :], vc_ref[hi, :], True)
            dq_ref[lo, :] = dq0.astype(dq_ref.dtype)
            dq_ref[hi, :] = dq1.astype(dq_ref.dtype)
            dk_ref[lo, :] = hk_ref[lo, :].astype(dk_ref.dtype)
            dv_ref[lo, :] = hv_ref[lo, :].astype(dv_ref.dtype)
            dk_ref[hi, :] = (hk_ref[hi, :] + dk0[:BLK]).astype(dk_ref.dtype)
            dv_ref[hi, :] = (hv_ref[hi, :] + dv0[:BLK]).astype(dv_ref.dtype)
            hk_ref[lo, :] = dk0[BLK:] + dk1[:BLK]
            hv_ref[lo, :] = dv0[BLK:] + dv1[:BLK]
            hk_ref[hi, :] = dk1[BLK:]
            hv_ref[hi, :] = dv1[BLK:]

        @pl.when(n == n_pairs)
        def _():
            dk_ref[...] = hk_ref[...].astype(dk_ref.dtype)
            dv_ref[...] = hv_ref[...].astype(dv_ref.dtype)

    pair_blk = (None, 2 * BLK, GROUP_W)
    clamp = lambda n: jnp.minimum(n, n_pairs - 1)
    qspec = pl.BlockSpec(pair_blk, lambda r, n: (r, clamp(n), 0))
    prev = pl.BlockSpec(ATTN_BLK, lambda r, n: (r, jnp.maximum(2 * clamp(n) - 1, 0), 0))
    kspec = pl.BlockSpec(pair_blk, lambda r, n: (r, jnp.maximum(n - 1, 0), 0))
    return pl.pallas_call(
        body, name=name, grid=(dil, n_pairs + 1), in_specs=[qspec, qspec, prev, qspec, prev, qspec, qspec, qspec],
        out_specs=[qspec, kspec, kspec], out_shape=[jax.ShapeDtypeStruct(q.shape, BF16)] * 3,
        scratch_shapes=[pltpu.VMEM((2 * BLK, GROUP_W), F32), pltpu.VMEM((2 * BLK, GROUP_W), F32)],
        compiler_params=_params(2),
    )(q, k, k, v, v, do, delta, lse)


CONV_TS = 128
HALO = 32
SHIFT_ROWS = CONV_TS + HALO - 8


def _make_shifts(buf, sh):
    for s_ in range(1, 8):
        sh[s_ - 1] = buf[pl.ds(s_, SHIFT_ROWS), :]


def _window(buf, sh, off, ts, cols):
    q, s_ = divmod(off, 8)
    if s_ == 0:
        return buf[pl.ds(off, ts), cols]
    return sh[s_ - 1, pl.ds(8 * q, ts), cols]


def _conv_fwd(name, u, w, b, lg, lb):
    s = u.shape[0]
    ts, per = CONV_TS, CONV_TS // HALO

    def body(a_ref, g_ref, ap_ref, gp_ref, w_ref, b_ref, lg_ref, lb_ref, c_ref, act_ref, buf, cbuf, sh):
        i = pl.program_id(0)
        buf[pl.ds(HALO, ts), :] = a_ref[...].astype(F32) * _sigmoid(g_ref[...].astype(F32))
        prev = ap_ref[...].astype(F32) * _sigmoid(gp_ref[...].astype(F32))
        buf[pl.ds(0, HALO), :] = jnp.where(i > 0, prev, 0.0)
        _make_shifts(buf, sh)
        rb = ts // 2
        for lc in range(D // LANES):
            cols = pl.ds(lc * LANES, LANES)
            for r0 in range(0, ts, rb):
                acc = jnp.broadcast_to(b_ref[:, cols], (rb, LANES))
                for j in range(CONV_K):
                    acc = acc + w_ref[pl.ds(j, 1), cols] * _window(buf, sh, r0 + HALO - (CONV_K - 1) + j, rb, cols)
                cbuf[pl.ds(r0, rb), cols] = acc
        c = cbuf[...]
        mu = jnp.mean(c, axis=-1, keepdims=True)
        xc = c - mu
        ln = xc * lax.rsqrt(jnp.mean(xc * xc, axis=-1, keepdims=True) + EPS) * lg_ref[...] + lb_ref[...]
        c_ref[...] = c.astype(c_ref.dtype)
        act_ref[...] = (ln * _sigmoid(ln)).astype(act_ref.dtype)

    halo = lambda cb: pl.BlockSpec((HALO, D), functools.partial(lambda i, cb: (jnp.maximum(i * per - 1, 0), cb), cb=cb))
    vec = pl.BlockSpec((1, D), lambda i: (0, 0))
    return pl.pallas_call(
        body, name=name, grid=(s // ts,),
        in_specs=[pl.BlockSpec((ts, D), lambda i: (i, 0)), pl.BlockSpec((ts, D), lambda i: (i, 1)), halo(0), halo(1),
                  pl.BlockSpec((HALO, D), lambda i: (0, 0)), vec, vec, vec],
        out_specs=[pl.BlockSpec((ts, D), lambda i: (i, 0))] * 2,
        out_shape=[jax.ShapeDtypeStruct((s, D), BF16)] * 2,
        scratch_shapes=[pltpu.VMEM((ts + HALO, D), F32), pltpu.VMEM((ts, D), F32), pltpu.VMEM((7, SHIFT_ROWS, D), F32)],
        compiler_params=_params(1),
    )(u, u, u, u, w, b, lg, lb)


def _conv_bwd(name, dc, u, w):
    s = u.shape[0]
    ts, per = CONV_TS, CONV_TS // HALO
    n_t = s // ts

    def body(dc_ref, dn_ref, a_ref, g_ref, ap_ref, gp_ref, w_ref, du_ref, dw_ref, db_ref, buf, dbuf, hbuf, sh, dsh):
        i = pl.program_id(0)
        a = a_ref[...].astype(F32)
        sg = _sigmoid(g_ref[...].astype(F32))
        buf[pl.ds(HALO, ts), :] = a * sg
        prev = ap_ref[...].astype(F32) * _sigmoid(gp_ref[...].astype(F32))
        buf[pl.ds(0, HALO), :] = jnp.where(i > 0, prev, 0.0)
        dcv = dc_ref[...]
        dbuf[pl.ds(0, ts), :] = dcv
        dbuf[pl.ds(ts, HALO), :] = jnp.where(i < n_t - 1, dn_ref[...], 0.0)

        @pl.when(i == 0)
        def _():
            dw_ref[...] = jnp.zeros_like(dw_ref)
            db_ref[...] = jnp.zeros_like(db_ref)

        db_ref[...] += _colsum(dcv)
        _make_shifts(buf, sh)
        _make_shifts(dbuf, dsh)
        rb = ts // 4
        for lc in range(D // LANES):
            cols = pl.ds(lc * LANES, LANES)
            for r0 in range(0, ts, rb):
                d0 = dbuf[pl.ds(r0, rb), cols]
                acc = jnp.zeros((rb, LANES), F32)
                for j in range(CONV_K):
                    acc = acc + w_ref[pl.ds(j, 1), cols] * _window(dbuf, dsh, r0 + CONV_K - 1 - j, rb, cols)
                    part = d0 * _window(buf, sh, r0 + HALO - (CONV_K - 1) + j, rb, cols)
                    dw_ref[pl.ds(8 * j, 8), cols] += jnp.sum(part.reshape(rb // 8, 8, LANES), axis=0)
                hbuf[pl.ds(r0, rb), cols] = acc
        dh = hbuf[...]
        du_ref[:, pl.ds(0, D)] = (dh * sg).astype(du_ref.dtype)
        du_ref[:, pl.ds(D, D)] = (dh * a * sg * (1.0 - sg)).astype(du_ref.dtype)

    halo = lambda cb: pl.BlockSpec((HALO, D), functools.partial(lambda i, cb: (jnp.maximum(i * per - 1, 0), cb), cb=cb))
    nxt = pl.BlockSpec((HALO, D), lambda i: (jnp.minimum((i + 1) * per, s // HALO - 1), 0))
    return pl.pallas_call(
        body, name=name, grid=(n_t,),
        in_specs=[pl.BlockSpec((ts, D), lambda i: (i, 0)), nxt, pl.BlockSpec((ts, D), lambda i: (i, 0)),
                  pl.BlockSpec((ts, D), lambda i: (i, 1)), halo(0), halo(1), pl.BlockSpec((HALO, D), lambda i: (0, 0))],
        out_specs=[pl.BlockSpec((ts, 2 * D), lambda i: (i, 0)), pl.BlockSpec((8 * CONV_K, D), lambda i: (0, 0)),
                   pl.BlockSpec((1, D), lambda i: (0, 0))],
        out_shape=[jax.ShapeDtypeStruct((s, 2 * D), BF16), jax.ShapeDtypeStruct((8 * CONV_K, D), F32),
                   jax.ShapeDtypeStruct((1, D), F32)],
        scratch_shapes=[pltpu.VMEM((ts + HALO, D), F32), pltpu.VMEM((ts + HALO, D), F32), pltpu.VMEM((ts, D), F32),
                        pltpu.VMEM((7, SHIFT_ROWS, D), F32), pltpu.VMEM((7, SHIFT_ROWS, D), F32)],
        compiler_params=_params(1),
    )(dc, dc, u, u, u, u, w)


def _mix_fwd(tag, x1, h1, gate, wts, cos2, sin2, nxt, job=None):
    ident = lambda accs, _t, _v: accs
    (qkv,), got = _with_job(_mm(f"{tag}_qkv", [(h1, wts["w_qkv"])], ident, [BF16], job=job), job)
    u = _mm(f"{tag}_u", [(h1, wts["w_u"])], ident, [BF16], tn=D)[0]
    ga, gc = _mm(f"{tag}_gates", [(h1, wts["w_ga"]), (h1, wts["w_gc"])], ident, [BF16] * 2)
    qd, kd, vd = _rope_split(f"{tag}_rope", qkv, cos2, sin2)
    per_group = [_attn_fwd(f"{tag}_attn{g}", qd[g], kd[g], vd[g]) for g in range(len(ATTN_DILATIONS))]
    o, lse = _attn_merge(f"{tag}_merge", [p[0] for p in per_group], [p[1] for p in per_group])
    cpre, act = _conv_fwd(f"{tag}_conv", u, wts["conv_w"], wts["conv_b"], wts["ln_g"], wts["ln_b"])

    def gate_epi(accs, tl, _v):
        ya, yc = accs
        return [_sigmoid(tl[0].astype(F32)) * ya + _sigmoid(tl[1].astype(F32)) * yc, ya, yc]

    y, ya, yc = _mm(f"{tag}_branch", [(o, wts["attn_wo"]), (act, wts["conv_wo"])], gate_epi, [BF16] * 3,
                    tiles=[(ga, 0), (gc, 0)])

    x2, f1, h_next = _project_out(f"{tag}_out", y, wts["w_out"], MX_COEF, x1, gate, nxt)
    return x2, h_next, (x1, h1, u, ga, gc, qd, kd, vd, o, lse, cpre, act, y, ya, yc, f1), got


def _ln_bwd_epi(accs, tl, vs):
    cv = tl[0].astype(F32)
    g, b = vs
    xc = cv - jnp.mean(cv, axis=-1, keepdims=True)
    rstd = lax.rsqrt(jnp.mean(xc * xc, axis=-1, keepdims=True) + EPS)
    xh = xc * rstd
    ln = xh * g + b
    sg = _sigmoid(ln)
    dln = accs[0] * (sg * (1.0 + ln * (1.0 - sg)))
    dxh = dln * g
    dc = rstd * (dxh - jnp.mean(dxh, axis=-1, keepdims=True) - xh * jnp.mean(dxh * xh, axis=-1, keepdims=True))
    return [dc, _colsum(dln * xh), _colsum(dln)]


def _mix_bwd(tag, dx2, dfm, saved, ng, sc, wts, cos2, sin2, prev, exchange_job):
    x1, h1, u, ga, gc, qd, kd, vd, o, lse, cpre, act, y, ya, yc, _f1 = saved

    def epi(accs, tl, _v):
        dy = accs[0]
        sa, sc_ = _sigmoid(tl[0].astype(F32)), _sigmoid(tl[1].astype(F32))
        return [dy * sa, dy * sc_, dy * tl[2].astype(F32) * sa * (1.0 - sa), dy * tl[3].astype(F32) * sc_ * (1.0 - sc_)]

    dya, dyc, dga, dgc = _mm(f"{tag}_bout", [(dfm, wts["w_out"])], epi, [BF16] * 4, nt=True,
                             tiles=[(ga, 0), (gc, 0), (ya, 0), (yc, 0)], chunk=2 * LANES)
    grads = {"w_out": _mm_tn(f"{tag}_dwout", y, dfm), "attn_wo": _mm_tn(f"{tag}_dwattn", o, dya),
             "conv_wo": _mm_tn(f"{tag}_dwconv", act, dyc)}
    ident = lambda accs, _t, _v: accs
    do = _mm(f"{tag}_battn", [(dya, wts["attn_wo"])], ident, [BF16], nt=True)[0]
    dc, dlg, dlb = _mm(f"{tag}_bconv", [(dyc, wts["conv_wo"])], _ln_bwd_epi, [F32], nt=True, tiles=[(cpre, 0)],
                       vecs=[wts["ln_g"], wts["ln_b"]], n_sums=2, tm=BWD_LAST_TM)
    du, dw8, dcb = _conv_bwd(f"{tag}_convb", dc, u, wts["conv_w"])
    dod, deltad, lsed = _grad_split(f"{tag}_gsplit", do, o, lse)
    dqs, dks, dvs = [], [], []
    for g in range(len(ATTN_DILATIONS)):
        dq, dk, dv = _attn_bwd(f"{tag}_attnb{g}", qd[g], kd[g], vd[g], dod[g], deltad[g], lsed[g])
        dqs.append(dq); dks.append(dk); dvs.append(dv)
    dqkv = _rope_join(f"{tag}_ropeb", dqs, dks, dvs, cos2, sin2)
    grads["w_in"] = jnp.concatenate([_mm_tn(f"{tag}_dwqkv", h1, dqkv), _mm_tn(f"{tag}_dwu", h1, du),
                                     _mm_tn(f"{tag}_dwga", h1, dga), _mm_tn(f"{tag}_dwgc", h1, dgc)], axis=1)
    job = exchange_job([_split(grads[k], ax) for k, ax in zip(MX_NAMES, MX_AXES)])
    prods = [(dqkv, wts["w_qkv"]), (du, wts["w_u"]), (dga, wts["w_ga"]), (dgc, wts["w_gc"])]
    dx1, df_prev, sums, dgate_prev, got = _bwd_last(f"{tag}_bin", prods, x1, dx2, ng, sc, prev, job)
    small = {"conv_w": dw8.reshape(CONV_K, 8, D).sum(axis=1), "conv_b": dcb, "ln_g": dlg, "ln_b": dlb}
    return dx1, df_prev, sums, dgate_prev, small, got


def _loss_head(name, x, target, fg, f_last, gate_last, coef_last):
    def body(rows, vecs):
        (xv, tv, fv), (g, gl) = rows, vecs
        xhat, r = _rms_parts(xv)
        err = xhat * g - tv
        dy = err * (1.0 / D)
        dx = _rms_bwd(dy * g, xhat, r)
        return [dx, coef_last * gl * dx], [_colsum(err * err), _colsum(dy * xhat), _colsum(coef_last * dx * fv.astype(F32))]

    return _rowwise(name, body, [x, target, f_last], [fg, gate_last], [(D, F32), (D, BF16)], [(1, D)] * 3)


FF_NAMES, FF_AXES = ("ffn_wg", "ffn_wu", "ffn_wd"), (1, 1, 0)
MX_NAMES, MX_AXES = ("w_in", "attn_wo", "conv_wo", "w_out"), (1, 1, 0, 0)
GROUPS = (("ff", 0, 0), ("mx", 0), ("ff", 0, 1), ("ff", 1, 0), ("mx", 1), ("ff", 1, 1))


def _group_name(grp):
    return "_".join(str(p) for p in grp)


def _mix_weights(blocks, small):
    w_in, attn_wo, conv_wo, w_out = [_join(b, a) for b, a in zip(blocks, MX_AXES)]
    return dict(small, w_qkv=w_in[:, :3 * QKV_W], w_u=w_in[:, 3 * QKV_W:3 * QKV_W + 2 * D],
                w_ga=w_in[:, 3 * QKV_W + 2 * D:3 * QKV_W + 3 * D], w_gc=w_in[:, 3 * QKV_W + 3 * D:],
                attn_wo=attn_wo, conv_wo=conv_wo, w_out=w_out)


def _local_step(x, target, mod, norm_g, shards, small_w, final_g, gather_job, exchange_job):
    cos2, sin2 = _rope_tables(x.shape[0])
    row = lambda a: a[None, :]
    sub = lambda grp: (grp[1], 2 * grp[2] if grp[0] == "ff" else 1)
    coef = lambda grp: FF_COEF if grp[0] == "ff" else MX_COEF
    ng = lambda grp: row(norm_g[sub(grp)])
    shift = lambda grp: row(mod[sub(grp)[0], 3 * sub(grp)[1]])
    scale = lambda grp: row(mod[sub(grp)[0], 3 * sub(grp)[1] + 1])
    gate = lambda grp: row(mod[sub(grp)[0], 3 * sub(grp)[1] + 2])
    blocks = _run_job("gather_" + _group_name(GROUPS[0]), gather_job(shards[GROUPS[0]]))
    h = _modulate("first_mod", x, ng(GROUPS[0]), scale(GROUPS[0]), shift(GROUPS[0]))
    saved, wts = [], []
    for n, grp in enumerate(GROUPS):
        last = n + 1 == len(GROUPS)
        job = None if last else gather_job(shards[GROUPS[n + 1]])
        nxt = None if last else (ng(GROUPS[n + 1]), scale(GROUPS[n + 1]), shift(GROUPS[n + 1]))
        if grp[0] == "ff":
            wts.append([_join(b, a) for b, a in zip(blocks, FF_AXES)])
            x, h, sv, blocks = _ffn_fwd(f"l{grp[1]}f{grp[2]}", x, h, gate(grp), *wts[-1], nxt, job=job)
        else:
            wts.append(_mix_weights(blocks, small_w[grp[1]]))
            x, h, sv, blocks = _mix_fwd(f"l{grp[1]}mx", x, h, gate(grp), wts[-1], cos2, sin2, nxt, job=job)
        saved.append(sv)
    dx, df, sq, dfg, dgate = _loss_head("loss_head", x, target, row(final_g), saved[-1][-1], gate(GROUPS[-1]), coef(GROUPS[-1]))
    loss = (0.5 / D) * jnp.sum(sq)
    dmod = [[None] * 3, [None] * 3]
    dng = [[None] * 3, [None] * 3]
    small, recv = [None, None], {}
    for n in reversed(range(len(GROUPS))):
        grp = GROUPS[n]
        l, i = sub(grp)
        prev = None if n == 0 else (saved[n - 1][-1], gate(GROUPS[n - 1]), coef(GROUPS[n - 1]))
        if grp[0] == "ff":
            dx, df_prev, sums, dgate_prev, recv[grp] = _ffn_bwd(
                f"l{l}f{grp[2]}", dx, df, saved[n], ng(grp), scale(grp), *wts[n], prev, exchange_job)
        else:
            dx, df_prev, sums, dgate_prev, small[l], recv[grp] = _mix_bwd(
                f"l{l}mx", dx, df, saved[n], ng(grp), scale(grp), wts[n], cos2, sin2, prev, exchange_job)
        dmod[l][i] = jnp.concatenate([sums[0], sums[1], dgate], axis=0)
        dng[l][i] = sums[2]
        df, dgate = df_prev, dgate_prev
    dmod = jnp.stack([jnp.concatenate(d, axis=0) for d in dmod])
    small = {k: jnp.stack([small[0][k].reshape(-1, D), small[1][k].reshape(-1, D)]) for k in small[0]}
    small = dict(small, norm_g=jnp.stack([jnp.concatenate(d, axis=0) for d in dng]), final_g=dfg.reshape(D))
    return loss, dx, dmod, small, recv


HBM_SPEC = pl.BlockSpec(memory_space=pl.ANY)


def _place():
    return lax.axis_index("x"), lax.axis_index("y"), lax.axis_index("c")


N_PEERS = N_DEV - 1


def _gather_job(arrays):
    n_a = len(arrays)

    def run(phase, x_refs, out_refs, send_sems, recv_sems, local_sems):
        x, y, c = _place()
        me, sibling = (x, y, c), (x, y, 1 - c)
        chips = [(1 - x, y), (x, 1 - y), (1 - x, 1 - y)]

        def copy(a, k, block, to, from_input=False):
            px, py, pc = block
            rows = out_refs[a].at[4 * px + 2 * py + pc]
            return pltpu.make_async_remote_copy(
                src_ref=x_refs[a] if from_input else rows, dst_ref=rows, send_sem=send_sems.at[a * N_PEERS + k],
                recv_sem=recv_sems.at[a * N_PEERS + k], device_id=to, device_id_type=pl.DeviceIdType.MESH)

        mine = [pltpu.make_async_copy(x_refs[a], out_refs[a].at[4 * x + 2 * y + c], local_sems.at[a]) for a in range(n_a)]
        first = []
        for j, chip in enumerate(chips):
            first += [copy(a, 1 + j, me, (*chip, c), from_input=True) for a in range(n_a)]
        first += [copy(a, 0, me, sibling, from_input=True) for a in range(n_a)]
        if phase == "start":
            for cp in mine + first:
                cp.start()
            return
        passed = []
        for j, chip in enumerate(chips):
            for a in range(n_a):
                copy(a, 1 + j, (*chip, c), me).wait_recv()
                passed.append(copy(a, 4 + j, (*chip, c), sibling))
                passed[-1].start()
        for a in range(n_a):
            copy(a, 0, sibling, me).wait_recv()
        for j, chip in enumerate(chips):
            for a in range(n_a):
                copy(a, 4 + j, (*chip, 1 - c), me).wait_recv()
        for cp in first + passed:
            cp.wait_send()
        for cp in mine:
            cp.wait()

    return dict(arrays=list(arrays), run=run,
                out_shape=[jax.ShapeDtypeStruct((N_DEV, *a.shape), a.dtype) for a in arrays])


def _job_scratch(job):
    n_a = len(job["arrays"])
    return [pltpu.SemaphoreType.DMA((n_a * N_PEERS,)), pltpu.SemaphoreType.DMA((n_a * N_PEERS,)), pltpu.SemaphoreType.DMA((n_a,))]


def _run_job(name, job):
    n_a = len(job["arrays"])

    def body(*refs):
        job["run"]("start", refs[:n_a], refs[n_a:2 * n_a], *refs[2 * n_a:])
        job["run"]("finish", refs[:n_a], refs[n_a:2 * n_a], *refs[2 * n_a:])

    return pl.pallas_call(
        body, name=name, out_shape=job["out_shape"], in_specs=[HBM_SPEC] * n_a, out_specs=[HBM_SPEC] * n_a,
        scratch_shapes=_job_scratch(job),
    )(*job["arrays"])


def _all_gather(name, arrays):
    return _run_job(name, _gather_job(arrays))


def _exchange_job(arrays):
    n_a = len(arrays)

    def run(phase, g_refs, out_refs, send_sems, recv_sems, local_sems):
        x, y, c = _place()
        my = 4 * x + 2 * y + c
        copies = [pltpu.make_async_copy(g_refs[a].at[my], out_refs[a].at[my], local_sems.at[a]) for a in range(n_a)]
        for k in (4, 2, 6, 1, 5, 3, 7):
            px = 1 - x if k & 4 else x
            py = 1 - y if k & 2 else y
            pc = 1 - c if k & 1 else c
            for a in range(n_a):
                copies.append(pltpu.make_async_remote_copy(
                    src_ref=g_refs[a].at[4 * px + 2 * py + pc], dst_ref=out_refs[a].at[my],
                    send_sem=send_sems.at[a * N_PEERS + k - 1], recv_sem=recv_sems.at[a * N_PEERS + k - 1],
                    device_id=(px, py, pc), device_id_type=pl.DeviceIdType.MESH))
        for cp in copies:
            if phase == "start":
                cp.start()
            else:
                cp.wait()

    return dict(arrays=list(arrays), run=run, out_shape=[jax.ShapeDtypeStruct(a.shape, a.dtype) for a in arrays])


SLAB_TS = 2048


def _sum_parts(name, parts):
    def body(rows, _v):
        tot = rows[0].astype(F32)
        for r in rows[1:]:
            tot = tot + r.astype(F32)
        return [tot], []

    return _rowwise(name, body, list(parts), [], [(parts[0].shape[1], F32)], ts=SLAB_TS)[0]


def _adamw(name, w, parts, m, v):
    def body(rows, _v):
        wv, mv, vv = rows[0], rows[1], rows[2]
        g = rows[3].astype(F32)
        for r in rows[4:]:
            g = g + r.astype(F32)
        m2 = ADAM_B1 * mv + (1.0 - ADAM_B1) * g
        v2 = ADAM_B2 * vv + (1.0 - ADAM_B2) * (g * g)
        m_hat = m2 / (1.0 - ADAM_B1 ** ADAM_STEP)
        v_hat = v2 / (1.0 - ADAM_B2 ** ADAM_STEP)
        delta = -ADAM_LR * (m_hat / (jnp.sqrt(v_hat) + ADAM_EPS) + ADAM_WD * wv)
        return [g, delta, m2, v2], []

    width = w.shape[1]
    return _rowwise(name, body, [w, m, v, *parts], [], [(width, F32)] * 4, ts=max(16, SLAB_TS * LANES // width))


def _to_slab(arrays, dtype):
    flat = jnp.concatenate([a.reshape(-1).astype(dtype) for a in arrays])
    rows = -(-flat.shape[0] // LANES)
    rows = -(-rows // 8) * 8 if rows <= SLAB_TS else -(-rows // SLAB_TS) * SLAB_TS
    return jnp.pad(flat, (0, rows * LANES - flat.shape[0])).reshape(rows, LANES)


def _from_slab(slab, shapes, lead=()):
    flat = slab.reshape(*lead, -1)
    out, at = [], 0
    for shp in shapes:
        size = 1
        for d in shp:
            size *= d
        out.append(flat[..., at:at + size].reshape(*lead, *shp))
        at += size
    return out


def _join(blocks, axis):
    full = jnp.moveaxis(blocks, 0, axis)
    return full.reshape(*full.shape[:axis], -1, *full.shape[axis + 2:])


def _split(full, axis):
    shp = full.shape
    return jnp.moveaxis(full.reshape(*shp[:axis], N_DEV, shp[axis] // N_DEV, *shp[axis + 1:]), axis, 0)


def kernel(x, c, ada_w, ada_b, norm_g, ffn_wg, ffn_wu, ffn_wd, w_in, attn_wo, conv_w, conv_b, conv_ln_g, conv_ln_b, conv_wo, w_out, final_g, loss_target, m_ada_w, m_ada_b, m_norm_g, m_ffn_wg, m_ffn_wu, m_ffn_wd, m_w_in, m_attn_wo, m_conv_w, m_conv_b, m_conv_ln_g, m_conv_ln_b, m_conv_wo, m_w_out, m_final_g, v_ada_w, v_ada_b, v_norm_g, v_ffn_wg, v_ffn_wu, v_ffn_wd, v_w_in, v_attn_wo, v_conv_w, v_conv_b, v_conv_ln_g, v_conv_ln_b, v_conv_wo, v_w_out, v_final_g):
    px, py, pc = _place()
    me = 4 * px + 2 * py + pc
    n_mod = ada_w.shape[2]
    big_w = dict(ffn_wg=ffn_wg, ffn_wu=ffn_wu, ffn_wd=ffn_wd, w_in=w_in, attn_wo=attn_wo, conv_wo=conv_wo, w_out=w_out)
    big_m = dict(ffn_wg=m_ffn_wg, ffn_wu=m_ffn_wu, ffn_wd=m_ffn_wd, w_in=m_w_in, attn_wo=m_attn_wo, conv_wo=m_conv_wo, w_out=m_w_out)
    big_v = dict(ffn_wg=v_ffn_wg, ffn_wu=v_ffn_wu, ffn_wd=v_ffn_wd, w_in=v_w_in, attn_wo=v_attn_wo, conv_wo=v_conv_wo, w_out=v_w_out)

    small_in = [c, norm_g, conv_w]
    g1 = _all_gather("gather_small", [_to_slab(small_in, F32)])[0]
    c_all, ng_blocks, cw_blocks = _from_slab(g1, [a.shape for a in small_in], lead=(N_DEV,))
    c_all = c_all.reshape(N_DEV, D)
    norm_g_full = _join(ng_blocks, 2)
    conv_w_full = _join(cw_blocks, 2)
    as2d = lambda a: a.reshape(-1, a.shape[-1])

    c_act = _rowwise("cond_silu", lambda rows, _v: ([rows[0] * _sigmoid(rows[0])], []), [c_all], [], [(D, BF16)])[0]
    c_pad = jnp.pad(c_act, ((0, LANES - N_DEV), (0, 0)))
    mod_cols = []
    for l in range(2):
        bias = lax.dynamic_slice_in_dim(ada_b[l], me * n_mod, n_mod)[None, :]
        out = _mm(f"mod{l}", [(c_pad, ada_w[l].astype(BF16))], lambda accs, _t, vs: [accs[0] + vs[0]], [F32], vecs=[bias])[0]
        mod_cols.append(out[:N_DEV])
    g2 = _all_gather("gather_mod", [_to_slab([jnp.stack(mod_cols)], F32)])[0]
    mod_all = _from_slab(g2, [(2, N_DEV, n_mod)], lead=(N_DEV,))[0]
    mod = lax.dynamic_index_in_dim(mod_all, me, axis=2, keepdims=False)
    mod = jnp.moveaxis(mod, 0, 1).reshape(2, 9, D)

    index = {grp: (grp[1], grp[2]) if grp[0] == "ff" else (grp[1],) for grp in GROUPS}
    names = {grp: FF_NAMES if grp[0] == "ff" else MX_NAMES for grp in GROUPS}
    shards = {grp: [big_w[k][index[grp]].astype(BF16) for k in names[grp]] for grp in GROUPS}
    small_l = [dict(conv_w=jnp.pad(conv_w_full[l], ((0, HALO - CONV_K), (0, 0))), conv_b=conv_b[l][None, :],
                    ln_g=conv_ln_g[l][None, :], ln_b=conv_ln_b[l][None, :]) for l in range(2)]

    loss, dx, dmod, small, recv = _local_step(x[0], loss_target[0], mod, norm_g_full, shards, small_l, final_g,
                                              _gather_job, _exchange_job)
    loss = lax.psum(loss, MESH_AXES)

    small_names = ["norm_g", "conv_w", "conv_b", "ln_g", "ln_b", "final_g"]
    small_parts = [dmod] + [small[k] for k in small_names]
    g3 = _all_gather("gather_small_grads", [_to_slab(small_parts, F32)])[0]
    tot = _sum_parts("sum_small_grads", [g3[k] for k in range(N_DEV)])
    _, g_ng, g_cw, g_cb, g_lg, g_lb, g_fg = _from_slab(tot, [a.shape for a in small_parts])
    g_ab = _from_slab(tot, [(2, 9 * D)])[0]
    dmod_all = _from_slab(g3, [dmod.shape], lead=(N_DEV,))[0].reshape(N_DEV, 2, 9 * D)
    dm_mine = lax.dynamic_slice_in_dim(dmod_all, me * n_mod, n_mod, axis=2)
    g_aw = jnp.stack([
        _mm_tn(f"dada_w{l}", c_pad, jnp.pad(dm_mine[:, l], ((0, LANES - N_DEV), (0, 0))).astype(BF16), out_dtype=F32)
        for l in range(2)])
    cols = lambda a: lax.dynamic_slice_in_dim(a, me * (D // N_DEV), D // N_DEV, axis=2)
    small_w = [ada_b, norm_g, conv_w, conv_b, conv_ln_g, conv_ln_b, final_g]
    small_m = [m_ada_b, m_norm_g, m_conv_w, m_conv_b, m_conv_ln_g, m_conv_ln_b, m_final_g]
    small_v = [v_ada_b, v_norm_g, v_conv_w, v_conv_b, v_conv_ln_g, v_conv_ln_b, v_final_g]
    small_g = [g_ab, cols(g_ng), cols(g_cw), g_cb, g_lg, g_lb, g_fg]
    s_shapes = [a.shape for a in small_w]
    s_out = _adamw("adamw_small", _to_slab(small_w, F32), [_to_slab(small_g, F32)], _to_slab(small_m, F32), _to_slab(small_v, F32))
    aw_out = [o.reshape(ada_w.shape) for o in _adamw("adamw_ada_w", as2d(ada_w), [as2d(g_aw)], as2d(m_ada_w), as2d(v_ada_w))]

    upd = {}
    for grp in GROUPS:
        for k, blocks in zip(names[grp], recv[grp]):
            parts = [(blocks, blocks.shape[2], 0, j) for j in range(N_DEV)]
            at = index[grp]
            upd[k, at] = _adamw(f"adamw_{k}_{_group_name(grp)}", big_w[k][at], parts, big_m[k][at], big_v[k][at])

    def stacked(k, i):
        if k in FF_NAMES:
            return jnp.stack([jnp.stack([upd[k, (l, j)][i] for j in range(2)]) for l in range(2)])
        return jnp.stack([upd[k, (l,)][i] for l in range(2)])

    def ordered(i):
        ab, ng, cw, cb, lg, lb, fg = _from_slab(s_out[i], s_shapes)
        bg = {k: stacked(k, i) for k in FF_NAMES + MX_NAMES}
        return [aw_out[i], ab, ng, bg["ffn_wg"], bg["ffn_wu"], bg["ffn_wd"], bg["w_in"], bg["attn_wo"], cw, cb, lg, lb,
                bg["conv_wo"], bg["w_out"], fg]

    return (loss, dx[None], *ordered(0), *ordered(1), *ordered(2), *ordered(3))
```

```python
import functools

import jax
import jax.numpy as jnp
from jax import lax
from jax.experimental import pallas as pl
from jax.experimental.pallas import tpu as pltpu

F32 = jnp.float32
BF16 = jnp.bfloat16

N_DEV = 8
D = 1024
D_FF = 2816
HEAD_DIM = 64
GROUP_W = 256
ATTN_DILATIONS = (1, 4, 16)
BLK = 128
QKV_W = 768
CONV_K = 31
ROPE_THETA = 10000.0
EPS = 1e-6
NEG_INF = -1e30
ADAM_LR, ADAM_B1, ADAM_B2, ADAM_EPS, ADAM_WD, ADAM_STEP = 0.001, 0.9, 0.999, 1e-08, 0.01, 10

V7X_VMEM_BYTES = 64 * 1024 * 1024
VMEM_LIMIT = V7X_VMEM_BYTES - 8 * 1024 * 1024
LANES = 128
MESH_AXES = ("x", "y", "c")


def _params(n_grid):
    return pltpu.CompilerParams(vmem_limit_bytes=VMEM_LIMIT, dimension_semantics=("arbitrary",) * n_grid)


def _sigmoid(v):
    return 1.0 / (1.0 + jnp.exp(-v))


def _mm(name, prods, epilogue, out_dtypes, *, nt=False, tiles=(), vecs=(), tm=512, tn=None, a_pre=None, chunk=None,
        job=None, n_sums=0):
    s = prods[0][0].shape[0]
    n = prods[0][1].shape[0] if nt else prods[0][1].shape[1]
    tn = n if tn is None else tn
    tm = min(tm, s)
    assert s % tm == 0 and n % tn == 0
    n_p, n_t, n_v = len(prods), len(tiles), len(vecs)
    dn = (((1,), (1,)), ((), ())) if nt else (((1,), (0,)), ((), ()))

    chunk = tn if chunk is None else chunk
    bounds = [(c0, min(chunk, tn - c0)) for c0 in range(0, tn, chunk)]

    n_o = len(out_dtypes) + n_sums
    n_j = len(job["arrays"]) if job else 0
    n_steps = (n // tn, s // tm)

    def body(*refs):
        p_refs, rest = refs[:2 * n_p], refs[2 * n_p:]
        t_refs, v_refs, rest = rest[:n_t], rest[n_t:n_t + n_v], rest[n_t + n_v:]
        j_in, o_refs, j_out, sems = rest[:n_j], rest[n_j:n_j + n_o], rest[n_j + n_o:2 * n_j + n_o], rest[2 * n_j + n_o:]
        o_refs, s_refs = o_refs[:n_o - n_sums], o_refs[n_o - n_sums:]
        if job:
            @pl.when(jnp.logical_and(pl.program_id(0) == 0, pl.program_id(1) == 0))
            def _():
                job["run"]("start", j_in, j_out, *sems)
        lhs = []
        for p in range(n_p):
            a = p_refs[2 * p][...]
            lhs.append(a if a_pre is None else a_pre(a))
        for c0, cw in bounds:
            cols = pl.ds(c0, cw)
            accs = []
            for p in range(n_p):
                b = p_refs[2 * p + 1][cols, :] if nt else p_refs[2 * p + 1][:, cols]
                accs.append(lax.dot_general(lhs[p], b, dn, preferred_element_type=F32))
            outs = epilogue(accs, [t[:, cols] for t in t_refs], [v[:, cols] for v in v_refs])
            for o_ref, o in zip(o_refs, outs[:len(o_refs)], strict=True):
                o_ref[:, cols] = o.astype(o_ref.dtype)
            if n_sums:
                first_row_tile = pl.program_id(1) == 0

                @pl.when(first_row_tile)
                def _():
                    for s_ref, part in zip(s_refs, outs[len(o_refs):], strict=True):
                        s_ref[:, cols] = part

                @pl.when(jnp.logical_not(first_row_tile))
                def _():
                    for s_ref, part in zip(s_refs, outs[len(o_refs):], strict=True):
                        s_ref[:, cols] += part
        if job:
            @pl.when(jnp.logical_and(pl.program_id(0) == n_steps[0] - 1, pl.program_id(1) == n_steps[1] - 1))
            def _():
                job["run"]("finish", j_in, j_out, *sems)

    in_specs = []
    operands = []
    for a, b in prods:
        k = a.shape[1]
        in_specs.append(pl.BlockSpec((tm, k), lambda j, i: (i, 0)))
        in_specs.append(pl.BlockSpec((tn, k), lambda j, i: (j, 0)) if nt else pl.BlockSpec((k, tn), lambda j, i: (0, j)))
        operands += [a, b]
    for arr, off in tiles:
        in_specs.append(pl.BlockSpec((tm, tn), functools.partial(lambda j, i, off: (i, j + off), off=off)))
        operands.append(arr)
    for v in vecs:
        in_specs.append(pl.BlockSpec((1, tn), lambda j, i: (0, j)))
        operands.append(v)
    out_specs = [pl.BlockSpec((tm, tn), lambda j, i: (i, j)) for _ in out_dtypes]
    out_specs += [pl.BlockSpec((1, tn), lambda j, i: (0, j)) for _ in range(n_sums)]
    out_shape = [jax.ShapeDtypeStruct((s, n), dt) for dt in out_dtypes] + [jax.ShapeDtypeStruct((1, n), F32)] * n_sums
    scratch = []
    if job:
        in_specs += [HBM_SPEC] * n_j
        operands += job["arrays"]
        out_specs += [HBM_SPEC] * n_j
        out_shape += job["out_shape"]
        scratch = _job_scratch(job)
    out = pl.pallas_call(
        body, name=name, grid=n_steps, in_specs=in_specs, out_specs=out_specs, out_shape=out_shape,
        scratch_shapes=scratch, compiler_params=_params(2),
    )(*operands)
    return (out[:n_o], out[n_o:]) if job else out


def _mm_tn(name, a, b, *, tk=512, tn=None, out_dtype=BF16):
    s, m = a.shape
    n = b.shape[1]
    tn = n if tn is None else tn
    tk = min(tk, s)
    n_k = s // tk
    assert s % tk == 0 and n % tn == 0

    def body(a_ref, b_ref, o_ref, acc_ref):
        k = pl.program_id(1)

        @pl.when(k == 0)
        def _():
            acc_ref[...] = jnp.zeros_like(acc_ref)

        acc_ref[...] += lax.dot_general(a_ref[...], b_ref[...], (((0,), (0,)), ((), ())), preferred_element_type=F32)

        @pl.when(k == n_k - 1)
        def _():
            o_ref[...] = acc_ref[...].astype(o_ref.dtype)

    return pl.pallas_call(
        body, name=name, grid=(n // tn, n_k),
        in_specs=[pl.BlockSpec((tk, m), lambda j, k: (k, 0)), pl.BlockSpec((tk, tn), lambda j, k: (k, j))],
        out_specs=pl.BlockSpec((m, tn), lambda j, k: (0, j)),
        out_shape=jax.ShapeDtypeStruct((m, n), out_dtype),
        scratch_shapes=[pltpu.VMEM((m, tn), F32)],
        compiler_params=_params(2),
    )(a, b)


def _pick_rows(s, target):
    if s <= target:
        return s
    return max(t for t in range(16, target + 1, 16) if s % t == 0)


def _rowwise(name, body, rows, vecs, outs, accs=(), *, ts=512):
    rows = [(r if isinstance(r, tuple) else (r, r.shape[1], 0)) for r in rows]
    rows = [r if len(r) == 4 else (*r, None) for r in rows]
    s = rows[0][0].shape[0]
    ts = _pick_rows(s, ts)
    n_r, n_v, n_o, n_a = len(rows), len(vecs), len(outs), len(accs)

    def kbody(*refs):
        r_refs, v_refs = refs[:n_r], refs[n_r:n_r + n_v]
        o_refs, a_refs = refs[n_r + n_v:n_r + n_v + n_o], refs[n_r + n_v + n_o:]
        res_o, res_a = body([r[...] for r in r_refs], [v[...] for v in v_refs])
        for o_ref, o in zip(o_refs, res_o, strict=True):
            o_ref[...] = o.astype(o_ref.dtype)
        if n_a:
            first = pl.program_id(0) == 0

            @pl.when(first)
            def _():
                for a_ref, a in zip(a_refs, res_a, strict=True):
                    a_ref[...] = a

            @pl.when(jnp.logical_not(first))
            def _():
                for a_ref, a in zip(a_refs, res_a, strict=True):
                    a_ref[...] += a

    in_specs = [
        pl.BlockSpec((ts, w), functools.partial(lambda i, cb: (i, cb), cb=cb)) if lead is None else
        pl.BlockSpec((None, ts, w), functools.partial(lambda i, cb, lead: (lead, i, cb), cb=cb, lead=lead))
        for _, w, cb, lead in rows]
    in_specs += [pl.BlockSpec(v.shape, functools.partial(lambda i, nd: (0,) * nd, nd=v.ndim)) for v in vecs]
    out_specs = [pl.BlockSpec((ts, w), lambda i: (i, 0)) for w, _ in outs]
    out_specs += [pl.BlockSpec(shp, functools.partial(lambda i, nd: (0,) * nd, nd=len(shp))) for shp in accs]
    out_shape = [jax.ShapeDtypeStruct((s, w), dt) for w, dt in outs] + [jax.ShapeDtypeStruct(shp, F32) for shp in accs]
    return pl.pallas_call(
        kbody, name=name, grid=(s // ts,), in_specs=in_specs, out_specs=out_specs, out_shape=out_shape,
        compiler_params=_params(1),
    )(*[r[0] for r in rows], *vecs)


def _colsum(v):
    return jnp.sum(v, axis=0, keepdims=True)


def _rms_parts(x):
    r = lax.rsqrt(jnp.mean(x * x, axis=-1, keepdims=True) + EPS)
    return x * r, r


def _rms_bwd(dxhat, xhat, r):
    return r * (dxhat - xhat * jnp.mean(dxhat * xhat, axis=-1, keepdims=True))


def _modulate(name, x, ng, sc, sh):
    def body(rows, vecs):
        (xv,), (g, s_, b) = rows, vecs
        xhat, _ = _rms_parts(xv)
        return [xhat * g * (1.0 + s_) + b], []

    return _rowwise(name, body, [x], [ng, sc, sh], [(D, BF16)])[0]


def _residual_epi(coef, with_next):
    def epi(accs, tl, vs):
        x_out = tl[0] + coef * vs[0] * accs[0]
        outs = [x_out, accs[0]]
        if with_next:
            xhat, _ = _rms_parts(x_out)
            outs.append(xhat * vs[1] * (1.0 + vs[2]) + vs[3])
        return outs

    return epi


def _bwd_last_epi(coef_prev):
    def epi(accs, tl, vs):
        dh = accs[0]
        for acc in accs[1:]:
            dh = dh + acc
        xhat, r = _rms_parts(tl[0])
        dn = dh * (1.0 + vs[1])
        dx_in = tl[1] + _rms_bwd(dn * vs[0], xhat, r)
        outs, sums = [dx_in], [_colsum(dh), _colsum(dh * xhat * vs[0]), _colsum(dn * xhat)]
        if coef_prev is not None:
            outs.append(coef_prev * vs[2] * dx_in)
            sums.append(_colsum(coef_prev * dx_in * tl[2].astype(F32)))
        return outs + sums

    return epi


BWD_LAST_TM = 256


def _bwd_last(name, prods, x, dxo, ng, sc, prev, job):
    tiles, vecs, dts = [(x, 0), (dxo, 0)], [ng, sc], [F32]
    if prev is not None:
        tiles, vecs, dts = tiles + [(prev[0], 0)], vecs + [prev[1]], dts + [BF16]
    outs, got = _mm(name, prods, _bwd_last_epi(None if prev is None else prev[2]), dts, nt=True, tiles=tiles, vecs=vecs,
                    n_sums=3 + (prev is not None), tm=BWD_LAST_TM, job=job)
    if prev is None:
        return outs[0], None, tuple(outs[1:4]), None, got
    return outs[0], outs[1], tuple(outs[2:5]), outs[5], got


def _with_job(result, job):
    return result if job else (result, None)


FF_COEF, MX_COEF = 0.5, 1.0


def _project_out(name, y, w, coef, x, gate, nxt):
    outs = _mm(name, [(y, w)], _residual_epi(coef, nxt is not None), [F32, BF16] + [BF16] * (nxt is not None),
               tiles=[(x, 0)], vecs=[gate, *(nxt or ())])
    return outs[0], outs[1], (outs[2] if nxt is not None else None)


def _ffn_fwd(tag, x, h, gate, wg, wu, wd, nxt, job=None):
    def up_epi(accs, _t, _v):
        a, u = accs
        return [a, u, a * _sigmoid(a) * u]

    (a, u, t), got = _with_job(_mm(f"{tag}_up", [(h, wg), (h, wu)], up_epi, [BF16] * 3, tn=D_FF // 2, job=job), job)
    x_out, f, h_next = _project_out(f"{tag}_down", t, wd, FF_COEF, x, gate, nxt)
    return x_out, h_next, (x, h, a, u, t, f), got


def _ffn_bwd(tag, dxo, df, saved, ng, sc, wg, wu, wd, prev, exchange_job):
    x, h, a, u, t, _f = saved

    def epi(accs, tl, _v):
        dt = accs[0]
        av, uv = tl[0].astype(F32), tl[1].astype(F32)
        sg = _sigmoid(av)
        return [dt * uv * (sg * (1.0 + av * (1.0 - sg))), dt * (av * sg)]

    da, du = _mm(f"{tag}_bdown", [(df, wd)], epi, [BF16] * 2, nt=True, tiles=[(a, 0), (u, 0)], tn=D_FF // 2,
                 chunk=3 * LANES)
    grads = (_mm_tn(f"{tag}_dwg", h, da), _mm_tn(f"{tag}_dwu", h, du), _mm_tn(f"{tag}_dwd", t, df))
    job = exchange_job([_split(g, ax) for g, ax in zip(grads, FF_AXES)])
    return _bwd_last(f"{tag}_bup", [(da, wg), (du, wu)], x, dxo, ng, sc, prev, job)


def _rope_tables(s):
    half = HEAD_DIM // 2
    inv_freq = ROPE_THETA ** (-(jnp.arange(half, dtype=F32) * 2.0 / HEAD_DIM))
    ang = jnp.arange(s, dtype=F32)[:, None] * inv_freq[None, :]
    cos, sin = jnp.cos(ang), jnp.sin(ang)
    return jnp.tile(jnp.concatenate([cos, cos], axis=1), (1, 2)), jnp.tile(jnp.concatenate([-sin, sin], axis=1), (1, 2))


def _rotate(v, cos2, sin2, sign):
    w = v.shape[1]
    lane = lax.broadcasted_iota(jnp.int32, v.shape, 1)
    partner = jnp.where(lane % HEAD_DIM < HEAD_DIM // 2, pltpu.roll(v, w - HEAD_DIM // 2, 1), pltpu.roll(v, HEAD_DIM // 2, 1))
    reps = w // LANES
    return v * jnp.tile(cos2, (1, reps)) + partner * (sign * jnp.tile(sin2, (1, reps)))


SPLIT_TS = 512


def _dilated_spec(dil, ts):
    return pl.BlockSpec((dil, ts // dil, GROUP_W), lambda i: (0, i, 0))


def _dilated_shape(s, dil, dtype):
    return jax.ShapeDtypeStruct((dil, s // dil, GROUP_W), dtype)


CHUNKS_PER_GROUP = GROUP_W // LANES


def _put(buf, chunk0, val):
    for c in range(val.shape[1] // LANES):
        buf[chunk0 + c] = val[:, c * LANES:(c + 1) * LANES]


def _get(buf, chunk0, n):
    return jnp.concatenate([buf[chunk0 + c] for c in range(n)], axis=1)


def _strided_rows(r, dil, ts):
    return pl.ds(r, ts // dil, stride=dil) if dil > 1 else pl.ds(0, ts)


def _deinterleave_one(buf, chunk0, out_ref, dil, ts):
    for half in range(CHUNKS_PER_GROUP):
        for r in range(dil):
            src = buf.at[chunk0 + half][_strided_rows(r, dil, ts), :]
            out_ref.at[r][:, pl.ds(half * LANES, LANES)] = src.astype(out_ref.dtype)


def _interleave_one(in_ref, buf, chunk0, dil, ts):
    for half in range(CHUNKS_PER_GROUP):
        for r in range(dil):
            src = in_ref.at[r][:, pl.ds(half * LANES, LANES)]
            buf.at[chunk0 + half][_strided_rows(r, dil, ts), :] = src.astype(F32)


def _deinterleave(buf, chunk0, out_refs, ts):
    for g, dil in enumerate(ATTN_DILATIONS):
        _deinterleave_one(buf, chunk0 + g * CHUNKS_PER_GROUP, out_refs[g], dil, ts)


def _interleave(in_refs, buf, chunk0, ts):
    for g, dil in enumerate(ATTN_DILATIONS):
        _interleave_one(in_refs[g], buf, chunk0 + g * CHUNKS_PER_GROUP, dil, ts)


def _rope_split(name, qkv, cos2, sin2):
    s, ts = qkv.shape[0], SPLIT_TS

    def body(qkv_ref, c_ref, s_ref, *rest):
        outs, buf = rest[:9], rest[9]
        c2, s2 = c_ref[...], s_ref[...]
        per = QKV_W // LANES
        _put(buf, 0, _rotate(qkv_ref[:, pl.ds(0, QKV_W)].astype(F32), c2, s2, 1.0))
        _put(buf, per, _rotate(qkv_ref[:, pl.ds(QKV_W, QKV_W)].astype(F32), c2, s2, 1.0))
        _put(buf, 2 * per, qkv_ref[:, pl.ds(2 * QKV_W, QKV_W)].astype(F32))
        for t in range(3):
            _deinterleave(buf, t * per, outs[3 * t:3 * t + 3], ts)

    tab = pl.BlockSpec((ts, LANES), lambda i: (i, 0))
    outs = pl.pallas_call(
        body, name=name, grid=(s // ts,), in_specs=[pl.BlockSpec((ts, 3 * QKV_W), lambda i: (i, 0)), tab, tab],
        out_specs=[_dilated_spec(d, ts) for _ in range(3) for d in ATTN_DILATIONS],
        out_shape=[_dilated_shape(s, d, BF16) for _ in range(3) for d in ATTN_DILATIONS],
        scratch_shapes=[pltpu.VMEM((3 * QKV_W // LANES, ts, LANES), F32)], compiler_params=_params(1),
    )(qkv, cos2, sin2)
    return outs[0:3], outs[3:6], outs[6:9]


def _rope_join(name, dq, dk, dv, cos2, sin2):
    s, ts = cos2.shape[0], SPLIT_TS

    def body(*refs):
        ins, c_ref, s_ref, o_ref, buf = refs[:9], refs[9], refs[10], refs[11], refs[12]
        per = QKV_W // LANES
        for t in range(3):
            _interleave(ins[3 * t:3 * t + 3], buf, t * per, ts)
        c2, s2 = c_ref[...], s_ref[...]
        o_ref[:, pl.ds(0, QKV_W)] = _rotate(_get(buf, 0, per), c2, s2, -1.0).astype(o_ref.dtype)
        o_ref[:, pl.ds(QKV_W, QKV_W)] = _rotate(_get(buf, per, per), c2, s2, -1.0).astype(o_ref.dtype)
        o_ref[:, pl.ds(2 * QKV_W, QKV_W)] = _get(buf, 2 * per, per).astype(o_ref.dtype)

    tab = pl.BlockSpec((ts, LANES), lambda i: (i, 0))
    return pl.pallas_call(
        body, name=name, grid=(s // ts,),
        in_specs=[_dilated_spec(d, ts) for _ in range(3) for d in ATTN_DILATIONS] + [tab, tab],
        out_specs=pl.BlockSpec((ts, 3 * QKV_W), lambda i: (i, 0)), out_shape=jax.ShapeDtypeStruct((s, 3 * QKV_W), BF16),
        scratch_shapes=[pltpu.VMEM((3 * QKV_W // LANES, ts, LANES), F32)], compiler_params=_params(1),
    )(*dq, *dk, *dv, cos2, sin2)


def _head_masks(shape):
    lane = lax.broadcasted_iota(jnp.int32, shape, 1)
    return [jnp.logical_and(lane >= h * HEAD_DIM, lane < (h + 1) * HEAD_DIM) for h in range(GROUP_W // HEAD_DIM)]


def _grad_split(name, do, o, lse):
    s, ts = do.shape[0], SPLIT_TS

    def body(do_ref, o_ref, l_ref, *rest):
        outs, buf = rest[:9], rest[9]
        dov = do_ref[...].astype(F32)
        prod = dov * o_ref[...].astype(F32)
        delta = jnp.zeros_like(prod)
        for hm in _head_masks(prod.shape):
            delta = jnp.where(hm, jnp.sum(jnp.where(hm, prod, 0.0), axis=1, keepdims=True), delta)
        _put(buf, 0, dov)
        _put(buf, CHUNKS_PER_GROUP, delta)
        _put(buf, 2 * CHUNKS_PER_GROUP, l_ref[...])
        for t in range(3):
            for g in range(3):
                _deinterleave_one(buf, t * CHUNKS_PER_GROUP, outs[3 * t + g], ATTN_DILATIONS[g], ts)

    nat = pl.BlockSpec((ts, GROUP_W), lambda i: (i, 0))
    dts = [BF16, F32, F32]
    outs = pl.pallas_call(
        body, name=name, grid=(s // ts,), in_specs=[nat, nat, nat],
        out_specs=[_dilated_spec(d, ts) for _ in range(3) for d in ATTN_DILATIONS],
        out_shape=[_dilated_shape(s, d, dt) for dt in dts for d in ATTN_DILATIONS],
        scratch_shapes=[pltpu.VMEM((3 * CHUNKS_PER_GROUP, ts, LANES), F32)], compiler_params=_params(1),
    )(do, o, lse)
    return outs[0:3], outs[3:6], outs[6:9]


def _band_mask(has_prev):
    qi = lax.broadcasted_iota(jnp.int32, (BLK, 2 * BLK), 0)
    kj = lax.broadcasted_iota(jnp.int32, (BLK, 2 * BLK), 1)
    in_prev = jnp.logical_and(jnp.logical_and(kj < BLK, kj >= qi), has_prev)
    return jnp.logical_or(in_prev, jnp.logical_and(kj >= BLK, kj - BLK <= qi))


def _dot_nt(a, b):
    return lax.dot_general(a, b, (((1,), (1,)), ((), ())), preferred_element_type=F32)


def _dot_tn(a, b):
    return lax.dot_general(a, b, (((0,), (0,)), ((), ())), preferred_element_type=F32)


def _dot(a, b):
    return jnp.dot(a, b, preferred_element_type=F32)


ATTN_BLK = (None, BLK, GROUP_W)
FWD_Q_BLOCKS = 4


def _attn_specs(clamp):
    cur = pl.BlockSpec(ATTN_BLK, lambda r, n: (r, clamp(n), 0))
    prev = pl.BlockSpec(ATTN_BLK, lambda r, n: (r, jnp.maximum(clamp(n) - 1, 0), 0))
    return [cur, cur, prev, cur, prev]


def _attn_fwd(name, q, k, v):
    dil, rows, _ = q.shape
    nb = rows // BLK
    scale = HEAD_DIM ** -0.5

    def one_block(q, kp, kc, vp, vc, has_prev):
        qi = lax.broadcasted_iota(jnp.int32, (BLK, BLK), 0)
        kj = lax.broadcasted_iota(jnp.int32, (BLK, BLK), 1)
        mask_c, mask_p = kj <= qi, jnp.logical_and(kj >= qi, has_prev)
        o_acc = jnp.zeros((BLK, GROUP_W), F32)
        l_acc = jnp.zeros((BLK, GROUP_W), F32)
        for hm in _head_masks((BLK, GROUP_W)):
            qm = jnp.where(hm, q, jnp.zeros_like(q))
            sc = jnp.where(mask_c, _dot_nt(qm, kc) * scale, NEG_INF)
            sp = jnp.where(mask_p, _dot_nt(qm, kp) * scale, NEG_INF)
            m = jnp.maximum(jnp.max(sc, axis=1, keepdims=True), jnp.max(sp, axis=1, keepdims=True))
            pc, pp = jnp.exp(sc - m), jnp.exp(sp - m)
            den = jnp.sum(pc, axis=1, keepdims=True) + jnp.sum(pp, axis=1, keepdims=True)
            oh = (_dot(pc.astype(BF16), vc) + _dot(pp.astype(BF16), vp)) / den
            o_acc = jnp.where(hm, oh, o_acc)
            l_acc = jnp.where(hm, m + jnp.log(den), l_acc)
        return o_acc, l_acc

    n_q = FWD_Q_BLOCKS

    def body(q_ref, kc_ref, kp_ref, vc_ref, vp_ref, o_ref, l_ref):
        for b in range(n_q):
            rows, before = pl.ds(b * BLK, BLK), pl.ds((b - 1) * BLK, BLK)
            kp, vp = (kp_ref[...], vp_ref[...]) if b == 0 else (kc_ref[before, :], vc_ref[before, :])
            o_ref[rows, :], l_ref[rows, :] = one_block(q_ref[rows, :], kp, kc_ref[rows, :], vp, vc_ref[rows, :],
                                                       pl.program_id(1) > 0 if b == 0 else True)

    cur = pl.BlockSpec((None, n_q * BLK, GROUP_W), lambda r, n: (r, n, 0))
    prev = pl.BlockSpec(ATTN_BLK, lambda r, n: (r, jnp.maximum(n_q * n - 1, 0), 0))
    return pl.pallas_call(
        body, name=name, grid=(dil, nb // n_q), in_specs=[cur, cur, prev, cur, prev], out_specs=[cur, cur],
        out_shape=[jax.ShapeDtypeStruct(q.shape, F32)] * 2, compiler_params=_params(2),
    )(q, k, k, v, v)


def _attn_merge(name, os_, ls_):
    s, ts = os_[0].shape[0] * os_[0].shape[1], SPLIT_TS

    def body(*refs):
        o_refs, l_refs, o_ref, l_ref, buf = refs[0:3], refs[3:6], refs[6], refs[7], refs[8]
        _interleave(o_refs, buf, 0, ts)
        _interleave(l_refs, buf, 3 * CHUNKS_PER_GROUP, ts)
        o0, o1, o2 = [_get(buf, g * CHUNKS_PER_GROUP, CHUNKS_PER_GROUP) for g in range(3)]
        l0, l1, l2 = [_get(buf, (3 + g) * CHUNKS_PER_GROUP, CHUNKS_PER_GROUP) for g in range(3)]
        m = jnp.maximum(jnp.maximum(l0, l1), l2)
        e0, e1, e2 = jnp.exp(l0 - m), jnp.exp(l1 - m), jnp.exp(l2 - m)
        tot = e0 + e1 + e2
        o_ref[...] = ((e0 * o0 + e1 * o1 + e2 * o2) / tot).astype(o_ref.dtype)
        l_ref[...] = m + jnp.log(tot)

    nat = pl.BlockSpec((ts, GROUP_W), lambda i: (i, 0))
    return pl.pallas_call(
        body, name=name, grid=(s // ts,), in_specs=[_dilated_spec(d, ts) for _ in range(2) for d in ATTN_DILATIONS],
        out_specs=[nat, nat], out_shape=[jax.ShapeDtypeStruct((s, GROUP_W), BF16), jax.ShapeDtypeStruct((s, GROUP_W), F32)],
        scratch_shapes=[pltpu.VMEM((6 * CHUNKS_PER_GROUP, ts, LANES), F32)], compiler_params=_params(1),
    )(*os_, *ls_)


def _attn_bwd(name, q, k, v, do, delta, lse):
    dil, rows, _ = q.shape
    nb = rows // BLK
    scale = HEAD_DIM ** -0.5

    def one_block(q, dov, lb, db, kp, kc, vp, vc, has_prev):
        mask = _band_mask(has_prev)
        k = jnp.concatenate([kp, kc], axis=0)
        v = jnp.concatenate([vp, vc], axis=0)
        dq_acc = jnp.zeros((BLK, GROUP_W), F32)
        dk = jnp.zeros((2 * BLK, GROUP_W), F32)
        dv = jnp.zeros((2 * BLK, GROUP_W), F32)
        for hm in _head_masks((BLK, GROUP_W)):
            qm = jnp.where(hm, q, jnp.zeros_like(q))
            dom = jnp.where(hm, dov, jnp.zeros_like(dov))
            lh = jnp.max(jnp.where(hm, lb, NEG_INF), axis=1, keepdims=True)
            delta = jnp.max(jnp.where(hm, db, NEG_INF), axis=1, keepdims=True)
            p = jnp.exp(jnp.where(mask, _dot_nt(qm, k) * scale, NEG_INF) - lh)
            ds = (p * (_dot_nt(dom, v) - delta) * scale).astype(BF16)
            dq_acc = jnp.where(hm, _dot(ds, k), dq_acc)
            dk += _dot_tn(ds, qm)
            dv += _dot_tn(p.astype(BF16), dom)
        return dq_acc, dk, dv

    n_pairs = nb // 2

    def body(q_ref, kc_ref, kp_ref, vc_ref, vp_ref, do_ref, dl_ref, l_ref, dq_ref, dk_ref, dv_ref, hk_ref, hv_ref):
        n = pl.program_id(1)
        lo, hi = pl.ds(0, BLK), pl.ds(BLK, BLK)

        @pl.when(n == 0)
        def _():
            hk_ref[...] = jnp.zeros_like(hk_ref)
            hv_ref[...] = jnp.zeros_like(hv_ref)

        @pl.when(n < n_pairs)
        def _():
            dq0, dk0, dv0 = one_block(q_ref[lo, :], do_ref[lo, :], l_ref[lo, :], dl_ref[lo, :],
                                      kp_ref[...], kc_ref[lo, :], vp_ref[...], vc_ref[lo, :], n > 0)
            dq1, dk1, dv1 = one_block(q_ref[hi, :], do_ref[hi, :], l_ref[hi, :], dl_ref[hi, :],
                                      kc_ref[lo, :], kc_ref[hi, :], vc_ref[lo, :], vc_ref[hi, :], True)
            dq_ref[lo, :] = dq0.astype(dq_ref.dtype)
            dq_ref[hi, :] = dq1.astype(dq_ref.dtype)
            dk_ref[lo, :] = hk_ref[lo, :].astype(dk_ref.dtype)
            dv_ref[lo, :] = hv_ref[lo, :].astype(dv_ref.dtype)
            dk_ref[hi, :] = (hk_ref[hi, :] + dk0[:BLK]).astype(dk_ref.dtype)
            dv_ref[hi, :] = (hv_ref[hi, :] + dv0[:BLK]).astype(dv_ref.dtype)
            hk_ref[lo, :] = dk0[BLK:] + dk1[:BLK]
            hv_ref[lo, :] = dv0[BLK:] + dv1[:BLK]
            hk_ref[hi, :] = dk1[BLK:]
            hv_ref[hi, :] = dv1[BLK:]

        @pl.when(n == n_pairs)
        def _():
            dk_ref[...] = hk_ref[...].astype(dk_ref.dtype)
            dv_ref[...] = hv_ref[...].astype(dv_ref.dtype)

    pair_blk = (None, 2 * BLK, GROUP_W)
    clamp = lambda n: jnp.minimum(n, n_pairs - 1)
    qspec = pl.BlockSpec(pair_blk, lambda r, n: (r, clamp(n), 0))
    prev = pl.BlockSpec(ATTN_BLK, lambda r, n: (r, jnp.maximum(2 * clamp(n) - 1, 0), 0))
    kspec = pl.BlockSpec(pair_blk, lambda r, n: (r, jnp.maximum(n - 1, 0), 0))
    return pl.pallas_call(
        body, name=name, grid=(dil, n_pairs + 1), in_specs=[qspec, qspec, prev, qspec, prev, qspec, qspec, qspec],
        out_specs=[qspec, kspec, kspec], out_shape=[jax.ShapeDtypeStruct(q.shape, BF16)] * 3,
        scratch_shapes=[pltpu.VMEM((2 * BLK, GROUP_W), F32), pltpu.VMEM((2 * BLK, GROUP_W), F32)],
        compiler_params=_params(2),
    )(q, k, k, v, v, do, delta, lse)


CONV_TS = 128
HALO = 32
SHIFT_ROWS = CONV_TS + HALO - 8


def _make_shifts(buf, sh):
    for s_ in range(1, 8):
        sh[s_ - 1] = buf[pl.ds(s_, SHIFT_ROWS), :]


def _window(buf, sh, off, ts, cols):
    q, s_ = divmod(off, 8)
    if s_ == 0:
        return buf[pl.ds(off, ts), cols]
    return sh[s_ - 1, pl.ds(8 * q, ts), cols]


def _conv_fwd(name, u, w, b, lg, lb):
    s = u.shape[0]
    ts, per = CONV_TS, CONV_TS // HALO

    def body(a_ref, g_ref, ap_ref, gp_ref, w_ref, b_ref, lg_ref, lb_ref, c_ref, act_ref, buf, cbuf, sh):
        i = pl.program_id(0)
        buf[pl.ds(HALO, ts), :] = a_ref[...].astype(F32) * _sigmoid(g_ref[...].astype(F32))
        prev = ap_ref[...].astype(F32) * _sigmoid(gp_ref[...].astype(F32))
        buf[pl.ds(0, HALO), :] = jnp.where(i > 0, prev, 0.0)
        _make_shifts(buf, sh)
        rb = ts // 2
        for lc in range(D // LANES):
            cols = pl.ds(lc * LANES, LANES)
            for r0 in range(0, ts, rb):
                acc = jnp.broadcast_to(b_ref[:, cols], (rb, LANES))
                for j in range(CONV_K):
                    acc = acc + w_ref[pl.ds(j, 1), cols] * _window(buf, sh, r0 + HALO - (CONV_K - 1) + j, rb, cols)
                cbuf[pl.ds(r0, rb), cols] = acc
        c = cbuf[...]
        mu = jnp.mean(c, axis=-1, keepdims=True)
        xc = c - mu
        ln = xc * lax.rsqrt(jnp.mean(xc * xc, axis=-1, keepdims=True) + EPS) * lg_ref[...] + lb_ref[...]
        c_ref[...] = c.astype(c_ref.dtype)
        act_ref[...] = (ln * _sigmoid(ln)).astype(act_ref.dtype)

    halo = lambda cb: pl.BlockSpec((HALO, D), functools.partial(lambda i, cb: (jnp.maximum(i * per - 1, 0), cb), cb=cb))
    vec = pl.BlockSpec((1, D), lambda i: (0, 0))
    return pl.pallas_call(
        body, name=name, grid=(s // ts,),
        in_specs=[pl.BlockSpec((ts, D), lambda i: (i, 0)), pl.BlockSpec((ts, D), lambda i: (i, 1)), halo(0), halo(1),
                  pl.BlockSpec((HALO, D), lambda i: (0, 0)), vec, vec, vec],
        out_specs=[pl.BlockSpec((ts, D), lambda i: (i, 0))] * 2,
        out_shape=[jax.ShapeDtypeStruct((s, D), BF16)] * 2,
        scratch_shapes=[pltpu.VMEM((ts + HALO, D), F32), pltpu.VMEM((ts, D), F32), pltpu.VMEM((7, SHIFT_ROWS, D), F32)],
        compiler_params=_params(1),
    )(u, u, u, u, w, b, lg, lb)


def _conv_bwd(name, dc, u, w):
    s = u.shape[0]
    ts, per = CONV_TS, CONV_TS // HALO
    n_t = s // ts

    def body(dc_ref, dn_ref, a_ref, g_ref, ap_ref, gp_ref, w_ref, du_ref, dw_ref, db_ref, buf, dbuf, hbuf, sh, dsh):
        i = pl.program_id(0)
        a = a_ref[...].astype(F32)
        sg = _sigmoid(g_ref[...].astype(F32))
        buf[pl.ds(HALO, ts), :] = a * sg
        prev = ap_ref[...].astype(F32) * _sigmoid(gp_ref[...].astype(F32))
        buf[pl.ds(0, HALO), :] = jnp.where(i > 0, prev, 0.0)
        dcv = dc_ref[...]
        dbuf[pl.ds(0, ts), :] = dcv
        dbuf[pl.ds(ts, HALO), :] = jnp.where(i < n_t - 1, dn_ref[...], 0.0)

        @pl.when(i == 0)
        def _():
            dw_ref[...] = jnp.zeros_like(dw_ref)
            db_ref[...] = jnp.zeros_like(db_ref)

        db_ref[...] += _colsum(dcv)
        _make_shifts(buf, sh)
        _make_shifts(dbuf, dsh)
        rb = ts // 4
        for lc in range(D // LANES):
            cols = pl.ds(lc * LANES, LANES)
            for r0 in range(0, ts, rb):
                d0 = dbuf[pl.ds(r0, rb), cols]
                acc = jnp.zeros((rb, LANES), F32)
                for j in range(CONV_K):
                    acc = acc + w_ref[pl.ds(j, 1), cols] * _window(dbuf, dsh, r0 + CONV_K - 1 - j, rb, cols)
                    part = d0 * _window(buf, sh, r0 + HALO - (CONV_K - 1) + j, rb, cols)
                    dw_ref[pl.ds(8 * j, 8), cols] += jnp.sum(part.reshape(rb // 8, 8, LANES), axis=0)
                hbuf[pl.ds(r0, rb), cols] = acc
        dh = hbuf[...]
        du_ref[:, pl.ds(0, D)] = (dh * sg).astype(du_ref.dtype)
        du_ref[:, pl.ds(D, D)] = (dh * a * sg * (1.0 - sg)).astype(du_ref.dtype)

    halo = lambda cb: pl.BlockSpec((HALO, D), functools.partial(lambda i, cb: (jnp.maximum(i * per - 1, 0), cb), cb=cb))
    nxt = pl.BlockSpec((HALO, D), lambda i: (jnp.minimum((i + 1) * per, s // HALO - 1), 0))
    return pl.pallas_call(
        body, name=name, grid=(n_t,),
        in_specs=[pl.BlockSpec((ts, D), lambda i: (i, 0)), nxt, pl.BlockSpec((ts, D), lambda i: (i, 0)),
                  pl.BlockSpec((ts, D), lambda i: (i, 1)), halo(0), halo(1), pl.BlockSpec((HALO, D), lambda i: (0, 0))],
        out_specs=[pl.BlockSpec((ts, 2 * D), lambda i: (i, 0)), pl.BlockSpec((8 * CONV_K, D), lambda i: (0, 0)),
                   pl.BlockSpec((1, D), lambda i: (0, 0))],
        out_shape=[jax.ShapeDtypeStruct((s, 2 * D), BF16), jax.ShapeDtypeStruct((8 * CONV_K, D), F32),
                   jax.ShapeDtypeStruct((1, D), F32)],
        scratch_shapes=[pltpu.VMEM((ts + HALO, D), F32), pltpu.VMEM((ts + HALO, D), F32), pltpu.VMEM((ts, D), F32),
                        pltpu.VMEM((7, SHIFT_ROWS, D), F32), pltpu.VMEM((7, SHIFT_ROWS, D), F32)],
        compiler_params=_params(1),
    )(dc, dc, u, u, u, u, w)


def _mix_fwd(tag, x1, h1, gate, wts, cos2, sin2, nxt, job=None):
    ident = lambda accs, _t, _v: accs
    (qkv,), got = _with_job(_mm(f"{tag}_qkv", [(h1, wts["w_qkv"])], ident, [BF16], job=job), job)
    u = _mm(f"{tag}_u", [(h1, wts["w_u"])], ident, [BF16], tn=D)[0]
    ga, gc = _mm(f"{tag}_gates", [(h1, wts["w_ga"]), (h1, wts["w_gc"])], ident, [BF16] * 2)
    qd, kd, vd = _rope_split(f"{tag}_rope", qkv, cos2, sin2)
    per_group = [_attn_fwd(f"{tag}_attn{g}", qd[g], kd[g], vd[g]) for g in range(len(ATTN_DILATIONS))]
    o, lse = _attn_merge(f"{tag}_merge", [p[0] for p in per_group], [p[1] for p in per_group])
    cpre, act = _conv_fwd(f"{tag}_conv", u, wts["conv_w"], wts["conv_b"], wts["ln_g"], wts["ln_b"])

    def gate_epi(accs, tl, _v):
        ya, yc = accs
        return [_sigmoid(tl[0].astype(F32)) * ya + _sigmoid(tl[1].astype(F32)) * yc, ya, yc]

    y, ya, yc = _mm(f"{tag}_branch", [(o, wts["attn_wo"]), (act, wts["conv_wo"])], gate_epi, [BF16] * 3,
                    tiles=[(ga, 0), (gc, 0)])

    x2, f1, h_next = _project_out(f"{tag}_out", y, wts["w_out"], MX_COEF, x1, gate, nxt)
    return x2, h_next, (x1, h1, u, ga, gc, qd, kd, vd, o, lse, cpre, act, y, ya, yc, f1), got


def _ln_bwd_epi(accs, tl, vs):
    cv = tl[0].astype(F32)
    g, b = vs
    xc = cv - jnp.mean(cv, axis=-1, keepdims=True)
    rstd = lax.rsqrt(jnp.mean(xc * xc, axis=-1, keepdims=True) + EPS)
    xh = xc * rstd
    ln = xh * g + b
    sg = _sigmoid(ln)
    dln = accs[0] * (sg * (1.0 + ln * (1.0 - sg)))
    dxh = dln * g
    dc = rstd * (dxh - jnp.mean(dxh, axis=-1, keepdims=True) - xh * jnp.mean(dxh * xh, axis=-1, keepdims=True))
    return [dc, _colsum(dln * xh), _colsum(dln)]


def _mix_bwd(tag, dx2, dfm, saved, ng, sc, wts, cos2, sin2, prev, exchange_job):
    x1, h1, u, ga, gc, qd, kd, vd, o, lse, cpre, act, y, ya, yc, _f1 = saved

    def epi(accs, tl, _v):
        dy = accs[0]
        sa, sc_ = _sigmoid(tl[0].astype(F32)), _sigmoid(tl[1].astype(F32))
        return [dy * sa, dy * sc_, dy * tl[2].astype(F32) * sa * (1.0 - sa), dy * tl[3].astype(F32) * sc_ * (1.0 - sc_)]

    dya, dyc, dga, dgc = _mm(f"{tag}_bout", [(dfm, wts["w_out"])], epi, [BF16] * 4, nt=True,
                             tiles=[(ga, 0), (gc, 0), (ya, 0), (yc, 0)], chunk=2 * LANES)
    grads = {"w_out": _mm_tn(f"{tag}_dwout", y, dfm), "attn_wo": _mm_tn(f"{tag}_dwattn", o, dya),
             "conv_wo": _mm_tn(f"{tag}_dwconv", act, dyc)}
    ident = lambda accs, _t, _v: accs
    do = _mm(f"{tag}_battn", [(dya, wts["attn_wo"])], ident, [BF16], nt=True)[0]
    dc, dlg, dlb = _mm(f"{tag}_bconv", [(dyc, wts["conv_wo"])], _ln_bwd_epi, [F32], nt=True, tiles=[(cpre, 0)],
                       vecs=[wts["ln_g"], wts["ln_b"]], n_sums=2, tm=BWD_LAST_TM)
    du, dw8, dcb = _conv_bwd(f"{tag}_convb", dc, u, wts["conv_w"])
    dod, deltad, lsed = _grad_split(f"{tag}_gsplit", do, o, lse)
    dqs, dks, dvs = [], [], []
    for g in range(len(ATTN_DILATIONS)):
        dq, dk, dv = _attn_bwd(f"{tag}_attnb{g}", qd[g], kd[g], vd[g], dod[g], deltad[g], lsed[g])
        dqs.append(dq); dks.append(dk); dvs.append(dv)
    dqkv = _rope_join(f"{tag}_ropeb", dqs, dks, dvs, cos2, sin2)
    grads["w_in"] = jnp.concatenate([_mm_tn(f"{tag}_dwqkv", h1, dqkv), _mm_tn(f"{tag}_dwu", h1, du),
                                     _mm_tn(f"{tag}_dwga", h1, dga), _mm_tn(f"{tag}_dwgc", h1, dgc)], axis=1)
    job = exchange_job([_split(grads[k], ax) for k, ax in zip(MX_NAMES, MX_AXES)])
    prods = [(dqkv, wts["w_qkv"]), (du, wts["w_u"]), (dga, wts["w_ga"]), (dgc, wts["w_gc"])]
    dx1, df_prev, sums, dgate_prev, got = _bwd_last(f"{tag}_bin", prods, x1, dx2, ng, sc, prev, job)
    small = {"conv_w": dw8.reshape(CONV_K, 8, D).sum(axis=1), "conv_b": dcb, "ln_g": dlg, "ln_b": dlb}
    return dx1, df_prev, sums, dgate_prev, small, got


def _loss_head(name, x, target, fg, f_last, gate_last, coef_last):
    def body(rows, vecs):
        (xv, tv, fv), (g, gl) = rows, vecs
        xhat, r = _rms_parts(xv)
        err = xhat * g - tv
        dy = err * (1.0 / D)
        dx = _rms_bwd(dy * g, xhat, r)
        return [dx, coef_last * gl * dx], [_colsum(err * err), _colsum(dy * xhat), _colsum(coef_last * dx * fv.astype(F32))]

    return _rowwise(name, body, [x, target, f_last], [fg, gate_last], [(D, F32), (D, BF16)], [(1, D)] * 3)


FF_NAMES, FF_AXES = ("ffn_wg", "ffn_wu", "ffn_wd"), (1, 1, 0)
MX_NAMES, MX_AXES = ("w_in", "attn_wo", "conv_wo", "w_out"), (1, 1, 0, 0)
GROUPS = (("ff", 0, 0), ("mx", 0), ("ff", 0, 1), ("ff", 1, 0), ("mx", 1), ("ff", 1, 1))


def _group_name(grp):
    return "_".join(str(p) for p in grp)


def _mix_weights(blocks, small):
    w_in, attn_wo, conv_wo, w_out = [_join(b, a) for b, a in zip(blocks, MX_AXES)]
    return dict(small, w_qkv=w_in[:, :3 * QKV_W], w_u=w_in[:, 3 * QKV_W:3 * QKV_W + 2 * D],
                w_ga=w_in[:, 3 * QKV_W + 2 * D:3 * QKV_W + 3 * D], w_gc=w_in[:, 3 * QKV_W + 3 * D:],
                attn_wo=attn_wo, conv_wo=conv_wo, w_out=w_out)


def _local_step(x, target, mod, norm_g, shards, small_w, final_g, gather_job, exchange_job):
    cos2, sin2 = _rope_tables(x.shape[0])
    row = lambda a: a[None, :]
    sub = lambda grp: (grp[1], 2 * grp[2] if grp[0] == "ff" else 1)
    coef = lambda grp: FF_COEF if grp[0] == "ff" else MX_COEF
    ng = lambda grp: row(norm_g[sub(grp)])
    shift = lambda grp: row(mod[sub(grp)[0], 3 * sub(grp)[1]])
    scale = lambda grp: row(mod[sub(grp)[0], 3 * sub(grp)[1] + 1])
    gate = lambda grp: row(mod[sub(grp)[0], 3 * sub(grp)[1] + 2])
    blocks = _run_job("gather_" + _group_name(GROUPS[0]), gather_job(shards[GROUPS[0]]))
    h = _modulate("first_mod", x, ng(GROUPS[0]), scale(GROUPS[0]), shift(GROUPS[0]))
    saved, wts = [], []
    for n, grp in enumerate(GROUPS):
        last = n + 1 == len(GROUPS)
        job = None if last else gather_job(shards[GROUPS[n + 1]])
        nxt = None if last else (ng(GROUPS[n + 1]), scale(GROUPS[n + 1]), shift(GROUPS[n + 1]))
        if grp[0] == "ff":
            wts.append([_join(b, a) for b, a in zip(blocks, FF_AXES)])
            x, h, sv, blocks = _ffn_fwd(f"l{grp[1]}f{grp[2]}", x, h, gate(grp), *wts[-1], nxt, job=job)
        else:
            wts.append(_mix_weights(blocks, small_w[grp[1]]))
            x, h, sv, blocks = _mix_fwd(f"l{grp[1]}mx", x, h, gate(grp), wts[-1], cos2, sin2, nxt, job=job)
        saved.append(sv)
    dx, df, sq, dfg, dgate = _loss_head("loss_head", x, target, row(final_g), saved[-1][-1], gate(GROUPS[-1]), coef(GROUPS[-1]))
    loss = (0.5 / D) * jnp.sum(sq)
    dmod = [[None] * 3, [None] * 3]
    dng = [[None] * 3, [None] * 3]
    small, recv = [None, None], {}
    for n in reversed(range(len(GROUPS))):
        grp = GROUPS[n]
        l, i = sub(grp)
        prev = None if n == 0 else (saved[n - 1][-1], gate(GROUPS[n - 1]), coef(GROUPS[n - 1]))
        if grp[0] == "ff":
            dx, df_prev, sums, dgate_prev, recv[grp] = _ffn_bwd(
                f"l{l}f{grp[2]}", dx, df, saved[n], ng(grp), scale(grp), *wts[n], prev, exchange_job)
        else:
            dx, df_prev, sums, dgate_prev, small[l], recv[grp] = _mix_bwd(
                f"l{l}mx", dx, df, saved[n], ng(grp), scale(grp), wts[n], cos2, sin2, prev, exchange_job)
        dmod[l][i] = jnp.concatenate([sums[0], sums[1], dgate], axis=0)
        dng[l][i] = sums[2]
        df, dgate = df_prev, dgate_prev
    dmod = jnp.stack([jnp.concatenate(d, axis=0) for d in dmod])
    small = {k: jnp.stack([small[0][k].reshape(-1, D), small[1][k].reshape(-1, D)]) for k in small[0]}
    small = dict(small, norm_g=jnp.stack([jnp.concatenate(d, axis=0) for d in dng]), final_g=dfg.reshape(D))
    return loss, dx, dmod, small, recv


HBM_SPEC = pl.BlockSpec(memory_space=pl.ANY)


def _place():
    return lax.axis_index("x"), lax.axis_index("y"), lax.axis_index("c")


N_PEERS = N_DEV - 1


def _gather_job(arrays):
    n_a = len(arrays)

    def run(phase, x_refs, out_refs, send_sems, recv_sems, local_sems):
        x, y, c = _place()
        me, sibling = (x, y, c), (x, y, 1 - c)
        chips = [(1 - x, y), (x, 1 - y), (1 - x, 1 - y)]

        def copy(a, k, block, to, from_input=False):
            px, py, pc = block
            rows = out_refs[a].at[4 * px + 2 * py + pc]
            return pltpu.make_async_remote_copy(
                src_ref=x_refs[a] if from_input else rows, dst_ref=rows, send_sem=send_sems.at[a * N_PEERS + k],
                recv_sem=recv_sems.at[a * N_PEERS + k], device_id=to, device_id_type=pl.DeviceIdType.MESH)

        mine = [pltpu.make_async_copy(x_refs[a], out_refs[a].at[4 * x + 2 * y + c], local_sems.at[a]) for a in range(n_a)]
        first = []
        for j, chip in enumerate(chips):
            first += [copy(a, 1 + j, me, (*chip, c), from_input=True) for a in range(n_a)]
        first += [copy(a, 0, me, sibling, from_input=True) for a in range(n_a)]
        if phase == "start":
            for cp in mine + first:
                cp.start()
            return
        passed = []
        for j, chip in enumerate(chips):
            for a in range(n_a):
                copy(a, 1 + j, (*chip, c), me).wait_recv()
                passed.append(copy(a, 4 + j, (*chip, c), sibling))
                passed[-1].start()
        for a in range(n_a):
            copy(a, 0, sibling, me).wait_recv()
        for j, chip in enumerate(chips):
            for a in range(n_a):
                copy(a, 4 + j, (*chip, 1 - c), me).wait_recv()
        for cp in first + passed:
            cp.wait_send()
        for cp in mine:
            cp.wait()

    return dict(arrays=list(arrays), run=run,
                out_shape=[jax.ShapeDtypeStruct((N_DEV, *a.shape), a.dtype) for a in arrays])


def _job_scratch(job):
    n_a = len(job["arrays"])
    return [pltpu.SemaphoreType.DMA((n_a * N_PEERS,)), pltpu.SemaphoreType.DMA((n_a * N_PEERS,)), pltpu.SemaphoreType.DMA((n_a,))]


def _run_job(name, job):
    n_a = len(job["arrays"])

    def body(*refs):
        job["run"]("start", refs[:n_a], refs[n_a:2 * n_a], *refs[2 * n_a:])
        job["run"]("finish", refs[:n_a], refs[n_a:2 * n_a], *refs[2 * n_a:])

    return pl.pallas_call(
        body, name=name, out_shape=job["out_shape"], in_specs=[HBM_SPEC] * n_a, out_specs=[HBM_SPEC] * n_a,
        scratch_shapes=_job_scratch(job),
    )(*job["arrays"])


def _all_gather(name, arrays):
    return _run_job(name, _gather_job(arrays))


def _exchange_job(arrays):
    n_a = len(arrays)

    def run(phase, g_refs, out_refs, send_sems, recv_sems, local_sems):
        x, y, c = _place()
        my = 4 * x + 2 * y + c
        copies = [pltpu.make_async_copy(g_refs[a].at[my], out_refs[a].at[my], local_sems.at[a]) for a in range(n_a)]
        for k in (4, 2, 6, 1, 5, 3, 7):
            px = 1 - x if k & 4 else x
            py = 1 - y if k & 2 else y
            pc = 1 - c if k & 1 else c
            for a in range(n_a):
                copies.append(pltpu.make_async_remote_copy(
                    src_ref=g_refs[a].at[4 * px + 2 * py + pc], dst_ref=out_refs[a].at[my],
                    send_sem=send_sems.at[a * N_PEERS + k - 1], recv_sem=recv_sems.at[a * N_PEERS + k - 1],
                    device_id=(px, py, pc), device_id_type=pl.DeviceIdType.MESH))
        for cp in copies:
            if phase == "start":
                cp.start()
            else:
                cp.wait()

    return dict(arrays=list(arrays), run=run, out_shape=[jax.ShapeDtypeStruct(a.shape, a.dtype) for a in arrays])


SLAB_TS = 2048


def _sum_parts(name, parts):
    def body(rows, _v):
        tot = rows[0].astype(F32)
        for r in rows[1:]:
            tot = tot + r.astype(F32)
        return [tot], []

    return _rowwise(name, body, list(parts), [], [(parts[0].shape[1], F32)], ts=SLAB_TS)[0]


def _adamw(name, w, parts, m, v):
    def body(rows, _v):
        wv, mv, vv = rows[0], rows[1], rows[2]
        g = rows[3].astype(F32)
        for r in rows[4:]:
            g = g + r.astype(F32)
        m2 = ADAM_B1 * mv + (1.0 - ADAM_B1) * g
        v2 = ADAM_B2 * vv + (1.0 - ADAM_B2) * (g * g)
        m_hat = m2 / (1.0 - ADAM_B1 ** ADAM_STEP)
        v_hat = v2 / (1.0 - ADAM_B2 ** ADAM_STEP)
        delta = -ADAM_LR * (m_hat / (jnp.sqrt(v_hat) + ADAM_EPS) + ADAM_WD * wv)
        return [g, delta, m2, v2], []

    width = w.shape[1]
    return _rowwise(name, body, [w, m, v, *parts], [], [(width, F32)] * 4, ts=max(16, SLAB_TS * LANES // width))


def _to_slab(arrays, dtype):
    flat = jnp.concatenate([a.reshape(-1).astype(dtype) for a in arrays])
    rows = -(-flat.shape[0] // LANES)
    rows = -(-rows // 8) * 8 if rows <= SLAB_TS else -(-rows // SLAB_TS) * SLAB_TS
    return jnp.pad(flat, (0, rows * LANES - flat.shape[0])).reshape(rows, LANES)


def _from_slab(slab, shapes, lead=()):
    flat = slab.reshape(*lead, -1)
    out, at = [], 0
    for shp in shapes:
        size = 1
        for d in shp:
            size *= d
        out.append(flat[..., at:at + size].reshape(*lead, *shp))
        at += size
    return out


def _join(blocks, axis):
    full = jnp.moveaxis(blocks, 0, axis)
    return full.reshape(*full.shape[:axis], -1, *full.shape[axis + 2:])


def _split(full, axis):
    shp = full.shape
    return jnp.moveaxis(full.reshape(*shp[:axis], N_DEV, shp[axis] // N_DEV, *shp[axis + 1:]), axis, 0)


def kernel(x, c, ada_w, ada_b, norm_g, ffn_wg, ffn_wu, ffn_wd, w_in, attn_wo, conv_w, conv_b, conv_ln_g, conv_ln_b, conv_wo, w_out, final_g, loss_target, m_ada_w, m_ada_b, m_norm_g, m_ffn_wg, m_ffn_wu, m_ffn_wd, m_w_in, m_attn_wo, m_conv_w, m_conv_b, m_conv_ln_g, m_conv_ln_b, m_conv_wo, m_w_out, m_final_g, v_ada_w, v_ada_b, v_norm_g, v_ffn_wg, v_ffn_wu, v_ffn_wd, v_w_in, v_attn_wo, v_conv_w, v_conv_b, v_conv_ln_g, v_conv_ln_b, v_conv_wo, v_w_out, v_final_g):
    px, py, pc = _place()
    me = 4 * px + 2 * py + pc
    n_mod = ada_w.shape[2]
    big_w = dict(ffn_wg=ffn_wg, ffn_wu=ffn_wu, ffn_wd=ffn_wd, w_in=w_in, attn_wo=attn_wo, conv_wo=conv_wo, w_out=w_out)
    big_m = dict(ffn_wg=m_ffn_wg, ffn_wu=m_ffn_wu, ffn_wd=m_ffn_wd, w_in=m_w_in, attn_wo=m_attn_wo, conv_wo=m_conv_wo, w_out=m_w_out)
    big_v = dict(ffn_wg=v_ffn_wg, ffn_wu=v_ffn_wu, ffn_wd=v_ffn_wd, w_in=v_w_in, attn_wo=v_attn_wo, conv_wo=v_conv_wo, w_out=v_w_out)

    small_in = [c, norm_g, conv_w]
    g1 = _all_gather("gather_small", [_to_slab(small_in, F32)])[0]
    c_all, ng_blocks, cw_blocks = _from_slab(g1, [a.shape for a in small_in], lead=(N_DEV,))
    c_all = c_all.reshape(N_DEV, D)
    norm_g_full = _join(ng_blocks, 2)
    conv_w_full = _join(cw_blocks, 2)
    as2d = lambda a: a.reshape(-1, a.shape[-1])

    c_act = _rowwise("cond_silu", lambda rows, _v: ([rows[0] * _sigmoid(rows[0])], []), [c_all], [], [(D, BF16)])[0]
    c_pad = jnp.pad(c_act, ((0, LANES - N_DEV), (0, 0)))
    mod_cols = []
    for l in range(2):
        bias = lax.dynamic_slice_in_dim(ada_b[l], me * n_mod, n_mod)[None, :]
        out = _mm(f"mod{l}", [(c_pad, ada_w[l].astype(BF16))], lambda accs, _t, vs: [accs[0] + vs[0]], [F32], vecs=[bias])[0]
        mod_cols.append(out[:N_DEV])
    g2 = _all_gather("gather_mod", [_to_slab([jnp.stack(mod_cols)], F32)])[0]
    mod_all = _from_slab(g2, [(2, N_DEV, n_mod)], lead=(N_DEV,))[0]
    mod = lax.dynamic_index_in_dim(mod_all, me, axis=2, keepdims=False)
    mod = jnp.moveaxis(mod, 0, 1).reshape(2, 9, D)

    index = {grp: (grp[1], grp[2]) if grp[0] == "ff" else (grp[1],) for grp in GROUPS}
    names = {grp: FF_NAMES if grp[0] == "ff" else MX_NAMES for grp in GROUPS}
    shards = {grp: [big_w[k][index[grp]].astype(BF16) for k in names[grp]] for grp in GROUPS}
    small_l = [dict(conv_w=jnp.pad(conv_w_full[l], ((0, HALO - CONV_K), (0, 0))), conv_b=conv_b[l][None, :],
                    ln_g=conv_ln_g[l][None, :], ln_b=conv_ln_b[l][None, :]) for l in range(2)]

    loss, dx, dmod, small, recv = _local_step(x[0], loss_target[0], mod, norm_g_full, shards, small_l, final_g,
                                              _gather_job, _exchange_job)
    loss = lax.psum(loss, MESH_AXES)

    small_names = ["norm_g", "conv_w", "conv_b", "ln_g", "ln_b", "final_g"]
    small_parts = [dmod] + [small[k] for k in small_names]
    g3 = _all_gather("gather_small_grads", [_to_slab(small_parts, F32)])[0]
    tot = _sum_parts("sum_small_grads", [g3[k] for k in range(N_DEV)])
    _, g_ng, g_cw, g_cb, g_lg, g_lb, g_fg = _from_slab(tot, [a.shape for a in small_parts])
    g_ab = _from_slab(tot, [(2, 9 * D)])[0]
    dmod_all = _from_slab(g3, [dmod.shape], lead=(N_DEV,))[0].reshape(N_DEV, 2, 9 * D)
    dm_mine = lax.dynamic_slice_in_dim(dmod_all, me * n_mod, n_mod, axis=2)
    g_aw = jnp.stack([
        _mm_tn(f"dada_w{l}", c_pad, jnp.pad(dm_mine[:, l], ((0, LANES - N_DEV), (0, 0))).astype(BF16), out_dtype=F32)
        for l in range(2)])
    cols = lambda a: lax.dynamic_slice_in_dim(a, me * (D // N_DEV), D // N_DEV, axis=2)
    small_w = [ada_b, norm_g, conv_w, conv_b, conv_ln_g, conv_ln_b, final_g]
    small_m = [m_ada_b, m_norm_g, m_conv_w, m_conv_b, m_conv_ln_g, m_conv_ln_b, m_final_g]
    small_v = [v_ada_b, v_norm_g, v_conv_w, v_conv_b, v_conv_ln_g, v_conv_ln_b, v_final_g]
    small_g = [g_ab, cols(g_ng), cols(g_cw), g_cb, g_lg, g_lb, g_fg]
    s_shapes = [a.shape for a in small_w]
    s_out = _adamw("adamw_small", _to_slab(small_w, F32), [_to_slab(small_g, F32)], _to_slab(small_m, F32), _to_slab(small_v, F32))
    aw_out = [o.reshape(ada_w.shape) for o in _adamw("adamw_ada_w", as2d(ada_w), [as2d(g_aw)], as2d(m_ada_w), as2d(v_ada_w))]

    upd = {}
    for grp in GROUPS:
        for k, blocks in zip(names[grp], recv[grp]):
            parts = [(blocks, blocks.shape[2], 0, j) for j in range(N_DEV)]
            at = index[grp]
            upd[k, at] = _adamw(f"adamw_{k}_{_group_name(grp)}", big_w[k][at], parts, big_m[k][at], big_v[k][at])

    def stacked(k, i):
        if k in FF_NAMES:
            return jnp.stack([jnp.stack([upd[k, (l, j)][i] for j in range(2)]) for l in range(2)])
        return jnp.stack([upd[k, (l,)][i] for l in range(2)])

    def ordered(i):
        ab, ng, cw, cb, lg, lb, fg = _from_slab(s_out[i], s_shapes)
        bg = {k: stacked(k, i) for k in FF_NAMES + MX_NAMES}
        return [aw_out[i], ab, ng, bg["ffn_wg"], bg["ffn_wu"], bg["ffn_wd"], bg["w_in"], bg["attn_wo"], cw, cb, lg, lb,
                bg["conv_wo"], bg["w_out"], fg]

    return (loss, dx[None], *ordered(0), *ordered(1), *ordered(2), *ordered(3))
```
